```python
import math
import jax, jax.numpy as jnp
from jax import lax
import numpy as np

D_MODEL = 1024
BATCH = 32
SEQ = 2048
DEPTH = 1
DEC_BATCH = 2
DEC_SEQ = 16384
PAST_LEN = 128

HEAD_DIM = 64
N_HEADS_A = 8
N_KV_A = 2
GQA_GROUP = N_HEADS_A // N_KV_A
N_HEADS_B = 4
AXIS_DIM = HEAD_DIM // 2
ROPE_THETA = 10000.0
GRID_W = 64
N_META = 16
Q_BLOCK = 128
N_BUCKETS = 32
MAX_DISTANCE = 128
WIDTH_A = N_HEADS_A * HEAD_DIM
WIDTH_B = N_HEADS_B * 2 * HEAD_DIM
MIX_WIDTH = WIDTH_A + WIDTH_B
KV_A = N_KV_A * HEAD_DIM
IN_COLS = WIDTH_A + 2 * KV_A + 3 * WIDTH_B
N_GROUPS = 4
EXPERTS_PER_GROUP = 8
N_EXPERTS = N_GROUPS * EXPERTS_PER_GROUP
TOP_K = 2
D_EXPERT = D_MODEL // 2
MOE_BLOCK = 128
EPS = 1e-6

kernel_name = 'hymba_gqa_diffattn_hmoe_encoder'


def rmsnorm(x, g):
    xf = x.astype(jnp.float32)
    y = xf * lax.rsqrt(jnp.mean(xf * xf, axis=-1, keepdims=True) + EPS)
    return (y * g.astype(jnp.float32)).astype(x.dtype)


def axial_rope(n_tok):
    rows = n_tok // GRID_W
    row = jnp.repeat(jnp.arange(rows, dtype=jnp.float32), GRID_W)
    col = jnp.tile(jnp.arange(GRID_W, dtype=jnp.float32), rows)
    inv = ROPE_THETA ** (-jnp.arange(0, AXIS_DIM, 2, dtype=jnp.float32) / AXIS_DIM)
    ar = row[:, None] * inv
    ac = col[:, None] * inv
    ang = jnp.concatenate([ar, ar, ac, ac], axis=-1)
    ang = jnp.concatenate([jnp.zeros((N_META, HEAD_DIM), jnp.float32), ang], axis=0)
    return jnp.cos(ang), jnp.sin(ang)


def apply_rope(x, cos, sin):
    h = AXIS_DIM // 2
    xf = x.astype(jnp.float32)
    rot = jnp.concatenate([-xf[..., h:AXIS_DIM], xf[..., :h],
                           -xf[..., AXIS_DIM + h:], xf[..., AXIS_DIM:AXIS_DIM + h]], axis=-1)
    return (xf * cos[None, :, None, :] + rot * sin[None, :, None, :]).astype(x.dtype)


def t5_bucket(rp):
    nb = N_BUCKETS // 2
    max_exact = nb // 2
    ret = (rp > 0).astype(jnp.int32) * nb
    n = jnp.abs(rp)
    large = max_exact + (jnp.log(jnp.maximum(n, 1).astype(jnp.float32) / max_exact)
                         / math.log(MAX_DISTANCE / max_exact) * (nb - max_exact)).astype(jnp.int32)
    large = jnp.minimum(large, nb - 1)
    return ret + jnp.where(n < max_exact, n, large)


def sweep(block_fn, batch, length):
    n_real = (length - N_META) // Q_BLOCK
    head = block_fn(0, N_META)
    body = lax.map(lambda b: block_fn(N_META + b * Q_BLOCK, Q_BLOCK), jnp.arange(n_real, dtype=jnp.int32))
    body = jnp.moveaxis(body, 0, 1).reshape((batch, n_real * Q_BLOCK) + body.shape[3:])
    return jnp.concatenate([head, body], axis=1)


def mixer_sublayer(x, layer, cos, sin, rel_bias, norm1, w_in, q_norm, k_norm,
                   lambda_q1, lambda_k1, lambda_q2, lambda_k2, sub_norm, w_out):
    bsz, length, _ = x.shape
    n = rmsnorm(x, norm1)
    p = jnp.einsum('bld,dc->blc', n, w_in)
    o0 = WIDTH_A
    o1 = o0 + KV_A
    o2 = o1 + KV_A
    o3 = o2 + WIDTH_B
    o4 = o3 + WIDTH_B
    qa = p[..., :o0].reshape(bsz, length, N_HEADS_A, HEAD_DIM)
    ka = p[..., o0:o1].reshape(bsz, length, N_KV_A, HEAD_DIM)
    va = p[..., o1:o2].reshape(bsz, length, N_KV_A, HEAD_DIM)
    qb = p[..., o2:o3].reshape(bsz, length, N_HEADS_B, 2, HEAD_DIM)
    kb = p[..., o3:o4].reshape(bsz, length, N_HEADS_B, 2, HEAD_DIM)
    vb = p[..., o4:].reshape(bsz, length, N_HEADS_B, 2 * HEAD_DIM)
    qa = apply_rope(rmsnorm(qa, q_norm), cos, sin)
    ka = apply_rope(rmsnorm(ka, k_norm), cos, sin)
    qb1, qb2 = qb[..., 0, :], qb[..., 1, :]
    kb1, kb2 = kb[..., 0, :], kb[..., 1, :]
    scale = HEAD_DIM ** -0.5
    lam_init = 0.8 - 0.6 * math.exp(-0.3 * layer)
    lam = (jnp.exp(jnp.sum(lambda_q1.astype(jnp.float32) * lambda_k1.astype(jnp.float32)))
           - jnp.exp(jnp.sum(lambda_q2.astype(jnp.float32) * lambda_k2.astype(jnp.float32))) + lam_init)
    kpos = jnp.arange(length, dtype=jnp.int32)

    def block_a(start, nq):
        q = lax.dynamic_slice_in_dim(qa, start, nq, axis=1).reshape(bsz, nq, N_KV_A, GQA_GROUP, HEAD_DIM)
        s = jnp.einsum('bqkgd,blkd->bkgql', q, ka, preferred_element_type=jnp.float32) * scale
        pr = jax.nn.softmax(s, axis=-1).astype(va.dtype)
        o = jnp.einsum('bkgql,blkd->bqkgd', pr, va)
        return o.reshape(bsz, nq, N_HEADS_A, HEAD_DIM)

    def block_b(start, nq):
        q1 = lax.dynamic_slice_in_dim(qb1, start, nq, axis=1)
        q2 = lax.dynamic_slice_in_dim(qb2, start, nq, axis=1)
        qpos = start + jnp.arange(nq, dtype=jnp.int32)
        bucket = t5_bucket(kpos[None, :] - qpos[:, None])
        bias = jnp.transpose(rel_bias[bucket].astype(jnp.float32), (2, 0, 1))
        s1 = jnp.einsum('bqhd,blhd->bhql', q1, kb1, preferred_element_type=jnp.float32) * scale + bias
        s2 = jnp.einsum('bqhd,blhd->bhql', q2, kb2, preferred_element_type=jnp.float32) * scale + bias
        a = jax.nn.softmax(s1, axis=-1) - lam * jax.nn.softmax(s2, axis=-1)
        return jnp.einsum('bhql,blhe->bqhe', a.astype(vb.dtype), vb)

    oa = sweep(block_a, bsz, length)
    ob = sweep(block_b, bsz, length)
    ob = rmsnorm(ob, sub_norm) * (1.0 - lam_init)
    o = jnp.concatenate([oa.reshape(bsz, length, WIDTH_A), ob.reshape(bsz, length, WIDTH_B)], axis=-1)
    return x + jnp.einsum('blc,cd->bld', o, w_out)


def hier_moe(xt, w_rg, b_rg, w_re, b_re, w_gate, w_up, w_down):
    t_count, d = xt.shape
    gl = (xt @ w_rg + b_rg).astype(jnp.float32)
    pg = jax.nn.softmax(gl, axis=-1)
    g_idx = jnp.argmax(pg, axis=-1).astype(jnp.int32)
    g_w = jnp.take_along_axis(pg, g_idx[:, None], axis=1)[:, 0]
    el = (xt @ w_re + b_re).astype(jnp.float32).reshape(t_count, N_GROUPS, EXPERTS_PER_GROUP)
    el = jnp.take_along_axis(el, g_idx[:, None, None], axis=1)[:, 0]
    pe = jax.nn.softmax(el, axis=-1)
    top_p, top_i = lax.top_k(pe, TOP_K)
    gates = g_w[:, None] * top_p / jnp.sum(top_p, axis=-1, keepdims=True)
    eid = (g_idx[:, None] * EXPERTS_PER_GROUP + top_i.astype(jnp.int32)).reshape(-1)
    tok = jnp.repeat(jnp.arange(t_count, dtype=jnp.int32), TOP_K)
    gw = gates.reshape(-1)
    n_assign = t_count * TOP_K
    order = jnp.argsort(eid)
    se, stok, sgw = eid[order], tok[order], gw[order]
    counts = jnp.bincount(eid, length=N_EXPERTS).astype(jnp.int32)
    padded = (counts + MOE_BLOCK - 1) // MOE_BLOCK * MOE_BLOCK
    off = jnp.cumsum(counts) - counts
    pend = jnp.cumsum(padded)
    poff = pend - padded
    dest = poff[se] + (jnp.arange(n_assign, dtype=jnp.int32) - off[se])
    n_blocks = (n_assign + N_EXPERTS * (MOE_BLOCK - 1) + MOE_BLOCK - 1) // MOE_BLOCK
    cap = n_blocks * MOE_BLOCK
    buf_tok = jnp.full((cap,), t_count, jnp.int32).at[dest].set(stok)
    buf_gw = jnp.zeros((cap,), xt.dtype).at[dest].set(sgw.astype(xt.dtype))
    block_start = jnp.arange(n_blocks, dtype=jnp.int32) * MOE_BLOCK
    block_e = jnp.minimum(jnp.searchsorted(pend, block_start, side='right'), N_EXPERTS - 1)
    x_pad = jnp.concatenate([xt, jnp.zeros((1, d), xt.dtype)], axis=0)

    def run(args):
        tok_b, e = args
        xb = x_pad[tok_b]
        h = jax.nn.silu(xb @ w_gate[e]) * (xb @ w_up[e])
        return h @ w_down[e]

    out = lax.map(run, (buf_tok.reshape(n_blocks, MOE_BLOCK), block_e)).reshape(cap, d)
    y = jnp.zeros((t_count + 1, d), xt.dtype).at[buf_tok].add(out * buf_gw[:, None])
    return y[:t_count]


def encode(x, meta_tokens, rel_bias, norm1, w_in, q_norm, k_norm, lambda_q1, lambda_k1,
           lambda_q2, lambda_k2, sub_norm, w_out, norm2, w_router_group, b_router_group,
           w_router_expert, b_router_expert, w_gate, w_up, w_down, final_norm):
    bsz, n_tok, d = x.shape
    cos, sin = axial_rope(n_tok)
    h = jnp.concatenate([jnp.broadcast_to(meta_tokens[None].astype(x.dtype), (bsz, N_META, d)), x], axis=1)
    length = n_tok + N_META
    for i in range(DEPTH):
        h = mixer_sublayer(h, i, cos, sin, rel_bias, norm1[i], w_in[i], q_norm[i], k_norm[i],
                           lambda_q1[i], lambda_k1[i], lambda_q2[i], lambda_k2[i], sub_norm[i], w_out[i])
        m = hier_moe(rmsnorm(h, norm2[i]).reshape(bsz * length, d), w_router_group[i], b_router_group[i],
                     w_router_expert[i], b_router_expert[i], w_gate[i], w_up[i], w_down[i])
        h = h + m.reshape(bsz, length, d)
    h = rmsnorm(h, final_norm)
    return h[:, N_META:]


def setup_inputs(seed: int = 0) -> dict:
    key = jax.random.key(seed)
    ks = jax.random.split(key, 24)
    f32 = jnp.float32
    nrm = lambda k, shape, s: jax.random.normal(k, shape, f32) * s
    gain = lambda k, shape: 1.0 + 0.02 * jax.random.normal(k, shape, f32)
    return {
        'x_prompt': nrm(ks[0], (BATCH, SEQ, D_MODEL), 1.0),
        'x_sample': nrm(ks[1], (DEC_BATCH, DEC_SEQ, D_MODEL), 1.0),
        'meta_tokens': nrm(ks[2], (N_META, D_MODEL), 1.0),
        'rel_bias': nrm(ks[3], (N_BUCKETS, N_HEADS_B), 0.2),
        'norm1': gain(ks[4], (DEPTH, D_MODEL)),
        'w_in': nrm(ks[5], (DEPTH, D_MODEL, IN_COLS), D_MODEL ** -0.5),
        'q_norm': gain(ks[6], (DEPTH, HEAD_DIM)),
        'k_norm': gain(ks[7], (DEPTH, HEAD_DIM)),
        'lambda_q1': nrm(ks[8], (DEPTH, HEAD_DIM), 0.1),
        'lambda_k1': nrm(ks[9], (DEPTH, HEAD_DIM), 0.1),
        'lambda_q2': nrm(ks[10], (DEPTH, HEAD_DIM), 0.1),
        'lambda_k2': nrm(ks[11], (DEPTH, HEAD_DIM), 0.1),
        'sub_norm': gain(ks[12], (DEPTH, 2 * HEAD_DIM)),
        'w_out': nrm(ks[13], (DEPTH, MIX_WIDTH, D_MODEL), MIX_WIDTH ** -0.5),
        'norm2': gain(ks[14], (DEPTH, D_MODEL)),
        'w_router_group': nrm(ks[15], (DEPTH, D_MODEL, N_GROUPS), D_MODEL ** -0.5),
        'b_router_group': nrm(ks[16], (DEPTH, N_GROUPS), 0.01),
        'w_router_expert': nrm(ks[17], (DEPTH, D_MODEL, N_EXPERTS), D_MODEL ** -0.5),
        'b_router_expert': nrm(ks[18], (DEPTH, N_EXPERTS), 0.01),
        'w_gate': nrm(ks[19], (DEPTH, N_EXPERTS, D_MODEL, D_EXPERT), D_MODEL ** -0.5),
        'w_up': nrm(ks[20], (DEPTH, N_EXPERTS, D_MODEL, D_EXPERT), D_MODEL ** -0.5),
        'w_down': nrm(ks[21], (DEPTH, N_EXPERTS, D_EXPERT, D_MODEL), D_EXPERT ** -0.5),
        'final_norm': gain(ks[22], (D_MODEL,)),
    }


def reference(x_prompt, x_sample, meta_tokens, rel_bias, norm1, w_in, q_norm, k_norm,
              lambda_q1, lambda_k1, lambda_q2, lambda_k2, sub_norm, w_out, norm2,
              w_router_group, b_router_group, w_router_expert, b_router_expert,
              w_gate, w_up, w_down, final_norm):
    y_prompt = encode(x_prompt, meta_tokens, rel_bias, norm1, w_in, q_norm, k_norm, lambda_q1, lambda_k1,
                      lambda_q2, lambda_k2, sub_norm, w_out, norm2, w_router_group, b_router_group,
                      w_router_expert, b_router_expert, w_gate, w_up, w_down, final_norm)
    y_sample = encode(x_sample, meta_tokens, rel_bias, norm1, w_in, q_norm, k_norm, lambda_q1, lambda_k1,
                      lambda_q2, lambda_k2, sub_norm, w_out, norm2, w_router_group, b_router_group,
                      w_router_expert, b_router_expert, w_gate, w_up, w_down, final_norm)
    return (y_prompt, y_sample)
```

```python
import functools
import math

import jax
import jax.numpy as jnp
from jax import lax
from jax.experimental import pallas as pl
from jax.experimental.pallas import tpu as pltpu

D_MODEL = 1024
HEAD_DIM = 64
N_HEADS_A = 8
N_KV_A = 2
GQA_GROUP = N_HEADS_A // N_KV_A
N_HEADS_B = 4
AXIS_DIM = HEAD_DIM // 2
ROPE_THETA = 10000.0
GRID_W = 64
N_META = 16
N_BUCKETS = 32
MAX_DISTANCE = 128
WIDTH_A = N_HEADS_A * HEAD_DIM
WIDTH_B = N_HEADS_B * 2 * HEAD_DIM
KV_A = N_KV_A * HEAD_DIM
N_GROUPS = 4
EXPERTS_PER_GROUP = 8
N_EXPERTS = N_GROUPS * EXPERTS_PER_GROUP
TOP_K = 2
D_EXPERT = D_MODEL // 2
EPS = 1e-6
LAYER = 0
LAM_INIT = 0.8 - 0.6 * math.exp(-0.3 * LAYER)

LOG2E = 1.4426950408889634
Q_SCALE = (HEAD_DIM ** -0.5) * LOG2E
MASK_VALUE = -1e30

LANE = 128
META_PAD = LANE
KEY_CHUNK = 256
BAND_PAD = KEY_CHUNK
TOKEN_TILE = 512
Q_TILE = 512
MOE_ROWS = 512
ROUTER_ROWS = 40
VMEM_LIMIT = 48 * 1024 * 1024

F32 = jnp.float32
BF16 = jnp.bfloat16


def _nt_dot(a, b):
    return lax.dot_general(a, b, (((1,), (1,)), ((), ())), preferred_element_type=F32)


def _inproj_kernel(x_ref, g1_ref, wt_ref, wkb_ref, cos_ref, sin_ref, qn_ref, kn_ref,
                   qa_ref, ka_ref, va_ref, qb_ref, kb_ref, vb_ref, *, chunk):
    x = x_ref[0]
    ms = jnp.mean(x * x, axis=-1, keepdims=True)
    n = (x * lax.rsqrt(ms + EPS) * g1_ref[...]).astype(BF16)
    pt = _nt_dot(wt_ref[...], n)
    kb_ref[0] = jnp.dot(n, wkb_ref[...], preferred_element_type=F32).astype(BF16)
    cos = cos_ref[...]
    sin = sin_ref[...]
    q = HEAD_DIM // 4

    def norm_rope(blk, gain):
        ss = jnp.mean(blk * blk, axis=0, keepdims=True)
        y = blk * lax.rsqrt(ss + EPS) * gain
        swapped = jnp.concatenate([y[q:2 * q], y[0:q], y[3 * q:4 * q], y[2 * q:3 * q]], axis=0)
        return y * cos + swapped * sin

    qn = qn_ref[...]
    kn = kn_ref[...]
    for h in range(N_HEADS_A):
        blk = pt[h * HEAD_DIM:(h + 1) * HEAD_DIM]
        qa_ref[0, h * HEAD_DIM:(h + 1) * HEAD_DIM, :] = (norm_rope(blk, qn) * Q_SCALE).astype(BF16)
    o0 = WIDTH_A
    ka = jnp.concatenate(
        [norm_rope(pt[o0 + j * HEAD_DIM:o0 + (j + 1) * HEAD_DIM], kn) for j in range(N_KV_A)], axis=0)
    ka_ref[0] = ka.T.astype(BF16)
    o1 = o0 + KV_A
    o2 = o1 + KV_A
    o3 = o2 + WIDTH_B
    tm = x.shape[0]
    for j in range(tm // chunk):
        va_ref[0, j] = pt[o1:o2, j * chunk:(j + 1) * chunk].astype(BF16)
        vb_ref[0, j] = pt[o3:o3 + WIDTH_B, j * chunk:(j + 1) * chunk].astype(BF16)
    qb_ref[0] = (pt[o2:o3] * Q_SCALE).astype(BF16)


def _inproj(x, g1, wt, wkb, cos_t, sin_t, qn, kn, *, tm, chunk):
    bsz, n, d = x.shape
    rows_t = wt.shape[0]
    grid = (bsz, n // tm)
    full = lambda shape: pl.BlockSpec(shape, lambda b, i: (0,) * len(shape))
    out_shape = (
        jax.ShapeDtypeStruct((bsz, WIDTH_A, n), BF16),
        jax.ShapeDtypeStruct((bsz, n, KV_A), BF16),
        jax.ShapeDtypeStruct((bsz, n // chunk, KV_A, chunk), BF16),
        jax.ShapeDtypeStruct((bsz, WIDTH_B, n), BF16),
        jax.ShapeDtypeStruct((bsz, n, WIDTH_B), BF16),
        jax.ShapeDtypeStruct((bsz, n // chunk, WIDTH_B, chunk), BF16),
    )
    return pl.pallas_call(
        functools.partial(_inproj_kernel, chunk=chunk),
        grid=grid,
        in_specs=[
            pl.BlockSpec((1, tm, d), lambda b, i: (b, i, 0)),
            full((1, d)),
            full((rows_t, d)),
            full((d, WIDTH_B)),
            pl.BlockSpec((HEAD_DIM, tm), lambda b, i: (0, i)),
            pl.BlockSpec((HEAD_DIM, tm), lambda b, i: (0, i)),
            full((HEAD_DIM, 1)),
            full((HEAD_DIM, 1)),
        ],
        out_specs=(
            pl.BlockSpec((1, WIDTH_A, tm), lambda b, i: (b, 0, i)),
            pl.BlockSpec((1, tm, KV_A), lambda b, i: (b, i, 0)),
            pl.BlockSpec((1, tm // chunk, KV_A, chunk), lambda b, i: (b, i, 0, 0)),
            pl.BlockSpec((1, WIDTH_B, tm), lambda b, i: (b, 0, i)),
            pl.BlockSpec((1, tm, WIDTH_B), lambda b, i: (b, i, 0)),
            pl.BlockSpec((1, tm // chunk, WIDTH_B, chunk), lambda b, i: (b, i, 0, 0)),
        ),
        out_shape=out_shape,
        compiler_params=pltpu.CompilerParams(
            dimension_semantics=("arbitrary", "arbitrary"), vmem_limit_bytes=VMEM_LIMIT),
        name="inproj",
    )(x, g1, wt, wkb, cos_t, sin_t, qn, kn)


def _online_update(s, c_shift, m_ref, l_ref, acc_ref, vblk, idx, rows):
    mc = jnp.max(s, axis=0, keepdims=True)
    if c_shift is not None:
        mc = mc + c_shift
    m_old = m_ref[idx]
    m_new = jnp.maximum(m_old, mc)
    alpha = jnp.exp2(m_old - m_new)
    shift = m_new if c_shift is None else m_new - c_shift
    p = jnp.exp2(s - shift)
    l_ref[idx] = alpha * l_ref[idx] + jnp.sum(p, axis=0, keepdims=True)
    pv = jnp.dot(vblk, p.astype(BF16), preferred_element_type=F32)
    acc_ref[rows] = alpha * acc_ref[rows] + pv
    m_ref[idx] = m_new


def _init_from_meta(s, m_ref, l_ref, acc_ref, vmeta, idx, rows):
    m0 = jnp.max(s, axis=0, keepdims=True)
    p = jnp.exp2(s - m0)
    m_ref[idx] = m0
    l_ref[idx] = jnp.sum(p, axis=0, keepdims=True)
    acc_ref[rows] = jnp.dot(vmeta, p.astype(BF16), preferred_element_type=F32)


def _attn_a_kernel(q_ref, k_ref, v_ref, km_ref, vm_ref, o_ref, qpad_ref, m_ref, l_ref, acc_ref, *, tq, nk):
    g = pl.program_id(1)
    zero = jnp.zeros((HEAD_DIM, tq), BF16)
    for h in range(GQA_GROUP):
        qh = q_ref[0, h * HEAD_DIM:(h + 1) * HEAD_DIM, :]
        qpad_ref[0:HEAD_DIM, h * tq:(h + 1) * tq] = jnp.where(g == 0, qh, zero)
        qpad_ref[HEAD_DIM:2 * HEAD_DIM, h * tq:(h + 1) * tq] = jnp.where(g == 1, qh, zero)

    valid = lax.broadcasted_iota(jnp.int32, (META_PAD, 1), 0) < N_META
    s = jnp.dot(km_ref[...], qpad_ref[...], preferred_element_type=F32)
    s = jnp.where(valid, s, MASK_VALUE)
    vmeta = vm_ref[...]
    for h in range(GQA_GROUP):
        rows = pl.ds(h * HEAD_DIM, HEAD_DIM)
        _init_from_meta(s[:, h * tq:(h + 1) * tq], m_ref, l_ref, acc_ref, vmeta, h, rows)

    def body(kc, carry):
        s = jnp.dot(k_ref[0, kc], qpad_ref[...], preferred_element_type=F32)
        vblk = v_ref[0, kc]
        for h in range(GQA_GROUP):
            rows = pl.ds(h * HEAD_DIM, HEAD_DIM)
            _online_update(s[:, h * tq:(h + 1) * tq], None, m_ref, l_ref, acc_ref, vblk, h, rows)
        return carry

    lax.fori_loop(0, nk, body, 0)

    outs = []
    for h in range(GQA_GROUP):
        outs.append(acc_ref[pl.ds(h * HEAD_DIM, HEAD_DIM)] / l_ref[h])
    o = jnp.concatenate(outs, axis=0)
    o_ref[0] = o.T.astype(BF16)


def _attn_a(qa_t, ka, va_t, km, vm_t, *, tq):
    bsz, _, n = qa_t.shape
    nk = n // KEY_CHUNK
    ka4 = ka.reshape(bsz, nk, KEY_CHUNK, KV_A)
    gw = GQA_GROUP * HEAD_DIM
    return pl.pallas_call(
        functools.partial(_attn_a_kernel, tq=tq, nk=nk),
        grid=(bsz, N_KV_A, n // tq),
        in_specs=[
            pl.BlockSpec((1, gw, tq), lambda b, g, i: (b, g, i)),
            pl.BlockSpec((1, nk, KEY_CHUNK, KV_A), lambda b, g, i: (b, 0, 0, 0)),
            pl.BlockSpec((1, nk, HEAD_DIM, KEY_CHUNK), lambda b, g, i: (b, 0, g, 0)),
            pl.BlockSpec((META_PAD, KV_A), lambda b, g, i: (0, 0)),
            pl.BlockSpec((HEAD_DIM, META_PAD), lambda b, g, i: (g, 0)),
        ],
        out_specs=pl.BlockSpec((1, tq, gw), lambda b, g, i: (b, i, g)),
        out_shape=jax.ShapeDtypeStruct((bsz, n, WIDTH_A), BF16),
        scratch_shapes=[
            pltpu.VMEM((KV_A, GQA_GROUP * tq), BF16),
            pltpu.VMEM((GQA_GROUP, 1, tq), F32),
            pltpu.VMEM((GQA_GROUP, 1, tq), F32),
            pltpu.VMEM((gw, tq), F32),
        ],
        compiler_params=pltpu.CompilerParams(
            dimension_semantics=("arbitrary", "arbitrary", "arbitrary"), vmem_limit_bytes=VMEM_LIMIT),
        name="attn_a",
    )(qa_t, ka4, va_t, km, vm_t)


def _attn_b_kernel(cfar_ref, q_ref, k_ref, v_ref, km_ref, vm_ref, band_ref, mbias_ref,
                   lq1_ref, lk1_ref, lq2_ref, lk2_ref, sn_ref, o_ref,
                   qblk_ref, m_ref, l_ref, acc_ref, *, tq, nk):
    h = pl.program_id(1)
    qi = pl.program_id(2)
    vd = 2 * HEAD_DIM
    zero = jnp.zeros((HEAD_DIM, tq), BF16)
    qblk_ref[0:HEAD_DIM, 0:tq] = q_ref[0, 0:HEAD_DIM, :]
    qblk_ref[0:HEAD_DIM, tq:2 * tq] = zero
    qblk_ref[HEAD_DIM:vd, 0:tq] = zero
    qblk_ref[HEAD_DIM:vd, tq:2 * tq] = q_ref[0, HEAD_DIM:vd, :]

    s = jnp.dot(km_ref[...], qblk_ref[...], preferred_element_type=F32)
    mb = mbias_ref[0]
    vmeta = vm_ref[...]
    for j in range(2):
        _init_from_meta(s[:, j * tq:(j + 1) * tq] + mb, m_ref, l_ref, acc_ref, vmeta, j,
                        (j, slice(None), slice(None)))

    def step(kc, bias, c_shift):
        s = jnp.dot(k_ref[0, kc], qblk_ref[...], preferred_element_type=F32)
        vblk = v_ref[0, kc]
        for j in range(2):
            sj = s[:, j * tq:(j + 1) * tq]
            if bias is not None:
                sj = sj + bias
            _online_update(sj, c_shift, m_ref, l_ref, acc_ref, vblk, j, (j, slice(None), slice(None)))

    per_tile = tq // KEY_CHUNK
    lo = jnp.maximum(qi * per_tile - 1, 0)
    hi = jnp.minimum(qi * per_tile + per_tile + 1, nk)
    c_neg = cfar_ref[h, 0]
    c_pos = cfar_ref[h, 1]

    def far_left(kc, carry):
        step(kc, None, c_neg)
        return carry

    def near(kc, carry):
        r0 = pl.multiple_of((kc - qi * per_tile + 1) * KEY_CHUNK, KEY_CHUNK)
        step(kc, band_ref[0, pl.ds(r0, KEY_CHUNK), :], None)
        return carry

    def far_right(kc, carry):
        step(kc, None, c_pos)
        return carry

    lax.fori_loop(0, lo, far_left, 0)
    lax.fori_loop(lo, hi, near, 0)
    lax.fori_loop(hi, nk, far_right, 0)

    lam = (jnp.exp(jnp.sum(lq1_ref[...] * lk1_ref[...], axis=-1, keepdims=True))
           - jnp.exp(jnp.sum(lq2_ref[...] * lk2_ref[...], axis=-1, keepdims=True)) + LAM_INIT)
    o = acc_ref[0] / l_ref[0] - lam * (acc_ref[1] / l_ref[1])
    ms = jnp.mean(o * o, axis=0, keepdims=True)
    o = o * lax.rsqrt(ms + EPS) * sn_ref[...] * (1.0 - LAM_INIT)
    o_ref[0] = o.T.astype(BF16)


def _attn_b(qb_t, kb, vb_t, kmb, vmb_t, band, mbias, cfar, lq1, lk1, lq2, lk2, sn, *, tq):
    bsz, _, n = qb_t.shape
    nk = n // KEY_CHUNK
    vd = 2 * HEAD_DIM
    kb4 = kb.reshape(bsz, nk, KEY_CHUNK, WIDTH_B)
    band_rows = band.shape[1]
    vec = lambda: pl.BlockSpec((1, HEAD_DIM), lambda b, h, i, *_: (0, 0))
    grid_spec = pltpu.PrefetchScalarGridSpec(
        num_scalar_prefetch=0,
        grid=(bsz, N_HEADS_B, n // tq),
        in_specs=[
            pl.BlockSpec(memory_space=pltpu.SMEM),
            pl.BlockSpec((1, vd, tq), lambda b, h, i: (b, h, i)),
            pl.BlockSpec((1, nk, KEY_CHUNK, vd), lambda b, h, i: (b, 0, 0, h)),
            pl.BlockSpec((1, nk, vd, KEY_CHUNK), lambda b, h, i: (b, 0, h, 0)),
            pl.BlockSpec((META_PAD, vd), lambda b, h, i: (0, h)),
            pl.BlockSpec((vd, META_PAD), lambda b, h, i: (h, 0)),
            pl.BlockSpec((1, band_rows, tq), lambda b, h, i: (h, 0, 0)),
            pl.BlockSpec((1, META_PAD, tq), lambda b, h, i: (h, 0, i)),
            vec(), vec(), vec(), vec(),
            pl.BlockSpec((vd, 1), lambda b, h, i: (0, 0)),
        ],
        out_specs=pl.BlockSpec((1, tq, vd), lambda b, h, i: (b, i, h)),
        scratch_shapes=[
            pltpu.VMEM((vd, 2 * tq), BF16),
            pltpu.VMEM((2, 1, tq), F32),
            pltpu.VMEM((2, 1, tq), F32),
            pltpu.VMEM((2, vd, tq), F32),
        ],
    )
    return pl.pallas_call(
        functools.partial(_attn_b_kernel, tq=tq, nk=nk),
        grid_spec=grid_spec,
        out_shape=jax.ShapeDtypeStruct((bsz, n, WIDTH_B), BF16),
        compiler_params=pltpu.CompilerParams(
            dimension_semantics=("arbitrary", "arbitrary", "arbitrary"), vmem_limit_bytes=VMEM_LIMIT),
        name="attn_b",
    )(cfar, qb_t, kb4, vb_t, kmb, vmb_t, band, mbias, lq1, lk1, lq2, lk2, sn)


def _outproj_kernel(oa_ref, ob_ref, x_ref, wa_ref, wb_ref, g2_ref, wrh_ref, wrl_ref, br_ref,
                    h_ref, n2_ref, eid_ref, gate_ref):
    att = (jnp.dot(oa_ref[0], wa_ref[...], preferred_element_type=F32)
           + jnp.dot(ob_ref[0], wb_ref[...], preferred_element_type=F32))
    hres = x_ref[0] + att
    h_ref[0] = hres
    ms = jnp.mean(hres * hres, axis=-1, keepdims=True)
    n2 = hres * lax.rsqrt(ms + EPS) * g2_ref[...]
    n2_hi = n2.astype(BF16)
    n2_lo = (n2 - n2_hi.astype(F32)).astype(BF16)
    n2_ref[0] = n2_hi
    wrh = wrh_ref[...]
    logits = _nt_dot(wrh, n2_hi) + _nt_dot(wrh, n2_lo) + _nt_dot(wrl_ref[...], n2_hi) + br_ref[...]
    tm = hres.shape[0]
    gl = logits[N_EXPERTS:N_EXPERTS + N_GROUPS]
    gmax = jnp.max(gl, axis=0, keepdims=True)
    giota = lax.broadcasted_iota(jnp.int32, (N_GROUPS, tm), 0)
    g_idx = jnp.min(jnp.where(gl == gmax, giota, N_GROUPS), axis=0, keepdims=True)
    g_w = 1.0 / jnp.sum(jnp.exp(gl - gmax), axis=0, keepdims=True)
    sel = jnp.zeros((EXPERTS_PER_GROUP, tm), F32)
    for g in range(N_GROUPS):
        sel = jnp.where(g_idx == g, logits[g * EXPERTS_PER_GROUP:(g + 1) * EXPERTS_PER_GROUP], sel)
    eiota = lax.broadcasted_iota(jnp.int32, (EXPERTS_PER_GROUP, tm), 0)
    e1 = jnp.max(sel, axis=0, keepdims=True)
    i1 = jnp.min(jnp.where(sel == e1, eiota, EXPERTS_PER_GROUP), axis=0, keepdims=True)
    sel2 = jnp.where(eiota == i1, -jnp.inf, sel)
    e2 = jnp.max(sel2, axis=0, keepdims=True)
    i2 = jnp.min(jnp.where(sel2 == e2, eiota, EXPERTS_PER_GROUP), axis=0, keepdims=True)
    r = jnp.exp(e2 - e1)
    w1 = g_w / (1.0 + r)
    eid_ref[0] = jnp.concatenate([g_idx * EXPERTS_PER_GROUP + i1, g_idx * EXPERTS_PER_GROUP + i2], axis=0)
    gate_ref[0] = jnp.concatenate([w1, w1 * r], axis=0)


def _outproj(oa, ob, x, wa, wb, g2, wrh, wrl, br, *, tm):
    bsz, n, d = x.shape
    full = lambda shape: pl.BlockSpec(shape, lambda b, i: (0,) * len(shape))
    tok = lambda w: pl.BlockSpec((1, tm, w), lambda b, i: (b, i, 0))
    lane = lambda: pl.BlockSpec((1, TOP_K, tm), lambda b, i: (b, 0, i))
    return pl.pallas_call(
        _outproj_kernel,
        grid=(bsz, n // tm),
        in_specs=[tok(WIDTH_A), tok(WIDTH_B), tok(d), full((WIDTH_A, d)), full((WIDTH_B, d)), full((1, d)),
                  full((ROUTER_ROWS, d)), full((ROUTER_ROWS, d)), full((ROUTER_ROWS, 1))],
        out_specs=(tok(d), tok(d), lane(), lane()),
        out_shape=(
            jax.ShapeDtypeStruct((bsz, n, d), F32),
            jax.ShapeDtypeStruct((bsz, n, d), BF16),
            jax.ShapeDtypeStruct((bsz, TOP_K, n), jnp.int32),
            jax.ShapeDtypeStruct((bsz, TOP_K, n), F32),
        ),
        compiler_params=pltpu.CompilerParams(
            dimension_semantics=("arbitrary", "arbitrary"), vmem_limit_bytes=VMEM_LIMIT),
        name="outproj_router",
    )(oa, ob, x, wa, wb, g2, wrh, wrl, br)


def _expert_kernel(be_ref, nb_ref, x_ref, wg_ref, wu_ref, wd_ref, y_ref):
    i = pl.program_id(0)

    @pl.when(i < nb_ref[0])
    def _():
        x = x_ref[...]
        gt = jnp.dot(x, wg_ref[0], preferred_element_type=F32)
        up = jnp.dot(x, wu_ref[0], preferred_element_type=F32)
        act = gt * (1.0 / (1.0 + jnp.exp(-gt))) * up
        y_ref[...] = jnp.dot(act.astype(BF16), wd_ref[0], preferred_element_type=F32).astype(BF16)

    @pl.when(i >= nb_ref[0])
    def _():
        y_ref[...] = jnp.zeros_like(y_ref)


def _experts(block_e, n_used, xs, wg, wu, wd, *, rows):
    cap, d = xs.shape
    grid_spec = pltpu.PrefetchScalarGridSpec(
        num_scalar_prefetch=2,
        grid=(cap // rows,),
        in_specs=[
            pl.BlockSpec((rows, d), lambda i, be, nb: (i, 0)),
            pl.BlockSpec((1, d, D_EXPERT), lambda i, be, nb: (be[i], 0, 0)),
            pl.BlockSpec((1, d, D_EXPERT), lambda i, be, nb: (be[i], 0, 0)),
            pl.BlockSpec((1, D_EXPERT, d), lambda i, be, nb: (be[i], 0, 0)),
        ],
        out_specs=pl.BlockSpec((rows, d), lambda i, be, nb: (i, 0)),
    )
    return pl.pallas_call(
        _expert_kernel,
        grid_spec=grid_spec,
        out_shape=jax.ShapeDtypeStruct((cap, d), BF16),
        compiler_params=pltpu.CompilerParams(
            dimension_semantics=("arbitrary",), vmem_limit_bytes=VMEM_LIMIT),
        name="experts",
    )(block_e, n_used, xs, wg, wu, wd)


def _final_kernel(h_ref, y0_ref, y1_ref, gate_ref, gf_ref, o_ref):
    gates = gate_ref[...]
    hres = (h_ref[...] + gates[:, 0:1] * y0_ref[...].astype(F32)
            + gates[:, 1:2] * y1_ref[...].astype(F32))
    ms = jnp.mean(hres * hres, axis=-1, keepdims=True)
    o_ref[...] = hres * lax.rsqrt(ms + EPS) * gf_ref[...]


def _final(h, y0, y1, gates, gf, *, tm):
    t, d = h.shape
    tok = lambda w: pl.BlockSpec((tm, w), lambda i: (i, 0))
    return pl.pallas_call(
        _final_kernel,
        grid=(t // tm,),
        in_specs=[tok(d), tok(d), tok(d), tok(TOP_K), pl.BlockSpec((1, d), lambda i: (0, 0))],
        out_specs=tok(d),
        out_shape=jax.ShapeDtypeStruct((t, d), F32),
        compiler_params=pltpu.CompilerParams(
            dimension_semantics=("arbitrary",), vmem_limit_bytes=VMEM_LIMIT),
        name="combine_final_norm",
    )(h, y0, y1, gates, gf)


def _rope_tables(n_tok):
    rows = n_tok // GRID_W
    row = jnp.repeat(jnp.arange(rows, dtype=F32), GRID_W)
    col = jnp.tile(jnp.arange(GRID_W, dtype=F32), rows)
    inv = ROPE_THETA ** (-jnp.arange(0, AXIS_DIM, 2, dtype=F32) / AXIS_DIM)
    ar = row[:, None] * inv
    ac = col[:, None] * inv
    ang = jnp.concatenate([ar, ar, ac, ac], axis=-1)
    q = HEAD_DIM // 4
    sign = jnp.concatenate([-jnp.ones((q,), F32), jnp.ones((q,), F32)] * 2)
    return jnp.cos(ang).T, (jnp.sin(ang) * sign).T


def _t5_bucket(rp):
    nb = N_BUCKETS // 2
    max_exact = nb // 2
    ret = (rp > 0).astype(jnp.int32) * nb
    n = jnp.abs(rp)
    large = max_exact + (jnp.log(jnp.maximum(n, 1).astype(F32) / max_exact)
                         / math.log(MAX_DISTANCE / max_exact) * (nb - max_exact)).astype(jnp.int32)
    large = jnp.minimum(large, nb - 1)
    return ret + jnp.where(n < max_exact, n, large)


def _bias_tables(rel_bias, n_tok, tq):
    rb = rel_bias.astype(F32) * LOG2E
    band_rows = tq + 2 * BAND_PAD
    r = jnp.arange(band_rows, dtype=jnp.int32)[:, None]
    c = jnp.arange(tq, dtype=jnp.int32)[None, :]
    band = jnp.transpose(rb[_t5_bucket(r - BAND_PAD - c)], (2, 0, 1))
    far = jnp.array([-(MAX_DISTANCE + BAND_PAD), MAX_DISTANCE + BAND_PAD], jnp.int32)
    cfar = rb[_t5_bucket(far)].T
    kpos = jnp.arange(N_META, dtype=jnp.int32)[:, None]
    qpos = N_META + jnp.arange(n_tok, dtype=jnp.int32)[None, :]
    mb = jnp.transpose(rb[_t5_bucket(kpos - qpos)], (2, 0, 1))
    pad = jnp.full((N_HEADS_B, META_PAD - N_META, n_tok), MASK_VALUE, F32)
    return band, cfar, jnp.concatenate([mb, pad], axis=1)


def _dispatch_plan(eid, rows):
    t = eid.shape[1]
    n_assign = TOP_K * t
    flat = eid.reshape(-1)
    a_idx = jnp.arange(n_assign, dtype=jnp.int32)
    skey = jnp.sort(flat * n_assign + a_idx)
    s_assign = skey % n_assign
    onehot = (flat[:, None] == jnp.arange(N_EXPERTS, dtype=jnp.int32)[None, :]).astype(jnp.int32)
    csum = jnp.cumsum(onehot, axis=0)
    counts = csum[-1]
    rank = jnp.take_along_axis(csum, flat[:, None], axis=1)[:, 0] - 1
    padded = (counts + rows - 1) // rows * rows
    pend = jnp.cumsum(padded)
    poff = pend - padded
    off = jnp.cumsum(counts) - counts
    pos = poff[flat] + rank
    n_blocks = (n_assign + N_EXPERTS * (rows - 1) + rows - 1) // rows
    cap = n_blocks * rows
    block_start = jnp.arange(n_blocks, dtype=jnp.int32) * rows
    block_e = jnp.minimum(jnp.searchsorted(pend, block_start, side='right'), N_EXPERTS - 1).astype(jnp.int32)
    d = jnp.arange(cap, dtype=jnp.int32)
    e_d = jnp.repeat(block_e, rows)
    j = d - poff[e_d]
    valid = j < counts[e_d]
    src = jnp.clip(off[e_d] + j, 0, n_assign - 1)
    row_tok = jnp.where(valid, s_assign[src] % t, 0)
    n_used = (pend[-1] // rows).astype(jnp.int32).reshape(1)
    return row_tok, block_e, n_used, pos.reshape(TOP_K, t)


def _encode(x, shared):
    bsz, n_tok, d = x.shape
    tq = min(Q_TILE, n_tok)
    tm = min(TOKEN_TILE, n_tok)
    cos_t, sin_t = _rope_tables(n_tok)
    qa_t, ka, va_t, qb_t, kb, vb_t = _inproj(
        x, shared['g1'], shared['wt'], shared['wkb'], cos_t, sin_t, shared['qn'], shared['kn'],
        tm=tm, chunk=KEY_CHUNK)
    oa = _attn_a(qa_t, ka, va_t, shared['km_a'], shared['vm_a'], tq=tq)
    band, cfar, mbias = _bias_tables(shared['rel_bias'], n_tok, tq)
    ob = _attn_b(qb_t, kb, vb_t, shared['km_b'], shared['vm_b'], band, mbias, cfar,
                 shared['lq1'], shared['lk1'], shared['lq2'], shared['lk2'], shared['sn'], tq=tq)
    h, n2, eid, gates = _outproj(oa, ob, x, shared['wa'], shared['wb'], shared['g2'],
                                 shared['wrh'], shared['wrl'], shared['br'], tm=tm)
    t = bsz * n_tok
    eid = jnp.transpose(eid, (1, 0, 2)).reshape(TOP_K, t)
    gates = jnp.transpose(gates, (1, 0, 2)).reshape(TOP_K, t)
    row_tok, block_e, n_used, pos = _dispatch_plan(eid, MOE_ROWS)
    xs = jnp.take(n2.reshape(t, d), row_tok, axis=0)
    y = _experts(block_e, n_used, xs, shared['wg'], shared['wu'], shared['wd'], rows=MOE_ROWS)
    y0 = jnp.take(y, pos[0], axis=0)
    y1 = jnp.take(y, pos[1], axis=0)
    out = _final(h.reshape(t, d), y0, y1, gates.T, shared['gf'], tm=tm)
    return out.reshape(bsz, n_tok, d)


def kernel(x_prompt, x_sample, meta_tokens, rel_bias, norm1, w_in, q_norm, k_norm, lambda_q1, lambda_k1,
           lambda_q2, lambda_k2, sub_norm, w_out, norm2, w_router_group, b_router_group, w_router_expert,
           b_router_expert, w_gate, w_up, w_down, final_norm):
    i = LAYER
    d = D_MODEL
    o0 = WIDTH_A
    o1 = o0 + KV_A
    o2 = o1 + KV_A
    o3 = o2 + WIDTH_B
    o4 = o3 + WIDTH_B
    w = w_in[i]
    wt = jnp.concatenate([w[:, :o2], w[:, o2:o3], w[:, o4:]], axis=1).T.astype(BF16)
    wr = jnp.concatenate([w_router_expert[i], w_router_group[i],
                          jnp.zeros((d, ROUTER_ROWS - N_EXPERTS - N_GROUPS), F32)], axis=1).T
    wrh = wr.astype(BF16)
    br = jnp.concatenate([b_router_expert[i], b_router_group[i],
                          jnp.zeros((ROUTER_ROWS - N_EXPERTS - N_GROUPS,), F32)]).reshape(ROUTER_ROWS, 1)
    shared = {
        'g1': norm1[i].reshape(1, d), 'wt': wt, 'wkb': w[:, o3:o4].astype(BF16),
        'qn': q_norm[i].reshape(HEAD_DIM, 1), 'kn': k_norm[i].reshape(HEAD_DIM, 1),
        'rel_bias': rel_bias,
        'lq1': lambda_q1[i].reshape(1, HEAD_DIM), 'lk1': lambda_k1[i].reshape(1, HEAD_DIM),
        'lq2': lambda_q2[i].reshape(1, HEAD_DIM), 'lk2': lambda_k2[i].reshape(1, HEAD_DIM),
        'sn': sub_norm[i].reshape(2 * HEAD_DIM, 1),
        'wa': w_out[i][:WIDTH_A].astype(BF16), 'wb': w_out[i][WIDTH_A:].astype(BF16),
        'g2': norm2[i].reshape(1, d), 'wrh': wrh, 'wrl': (wr - wrh.astype(F32)).astype(BF16), 'br': br,
        'wg': w_gate[i].astype(BF16), 'wu': w_up[i].astype(BF16), 'wd': w_down[i].astype(BF16),
        'gf': final_norm.reshape(1, d),
    }
    xm = jnp.concatenate([meta_tokens.astype(F32), jnp.zeros((META_PAD - N_META, d), F32)], axis=0)[None]
    ones = jnp.ones((HEAD_DIM, META_PAD), F32)
    _, km_a, vm_a, _, km_b, vm_b = _inproj(
        xm, shared['g1'], shared['wt'], shared['wkb'], ones, 0.0 * ones, shared['qn'], shared['kn'],
        tm=META_PAD, chunk=META_PAD)
    shared['km_a'] = km_a[0]
    shared['vm_a'] = vm_a[0, 0]
    shared['km_b'] = km_b[0]
    shared['vm_b'] = vm_b[0, 0]
    return _encode(x_prompt, shared), _encode(x_sample, shared)
```

```python
import functools
import math

import jax
import jax.numpy as jnp
from jax import lax
from jax.experimental import pallas as pl
from jax.experimental.pallas import tpu as pltpu

D_MODEL = 1024
HEAD_DIM = 64
N_HEADS_A = 8
N_KV_A = 2
GQA_GROUP = N_HEADS_A // N_KV_A
N_HEADS_B = 4
AXIS_DIM = HEAD_DIM // 2
ROPE_THETA = 10000.0
GRID_W = 64
N_META = 16
N_BUCKETS = 32
MAX_DISTANCE = 128
WIDTH_A = N_HEADS_A * HEAD_DIM
WIDTH_B = N_HEADS_B * 2 * HEAD_DIM
KV_A = N_KV_A * HEAD_DIM
N_GROUPS = 4
EXPERTS_PER_GROUP = 8
N_EXPERTS = N_GROUPS * EXPERTS_PER_GROUP
TOP_K = 2
D_EXPERT = D_MODEL // 2
EPS = 1e-6
LAYER = 0
LAM_INIT = 0.8 - 0.6 * math.exp(-0.3 * LAYER)

LOG2E = 1.4426950408889634
Q_SCALE = (HEAD_DIM ** -0.5) * LOG2E
MASK_VALUE = -1e30

LANE = 128
BF16_ROWS = 16
META_PAD = LANE
KEY_CHUNK = 256
TOKEN_TILE = 512
Q_TILE = 512
MOE_ROWS = 512
ROUTER_ROWS = 40
VMEM_LIMIT = 48 * 1024 * 1024

VA_ROWS = HEAD_DIM + BF16_ROWS
VB_ROWS = 2 * HEAD_DIM + BF16_ROWS
TILE_CHUNKS = Q_TILE // KEY_CHUNK
NEAR_CHUNKS = TILE_CHUNKS + 2
BAND_SHIFT = 2
BAND_ROWS = (TILE_CHUNKS + 5) * KEY_CHUNK

F32 = jnp.float32
BF16 = jnp.bfloat16


def _nt_dot(a, b):
    return lax.dot_general(a, b, (((1,), (1,)), ((), ())), preferred_element_type=F32)


def _inproj_kernel(x_ref, g1_ref, wt_ref, wkb_ref, cos_ref, sin_ref, qn_ref, kn_ref,
                   qa_ref, ka_ref, va_ref, qb_ref, kb_ref, vb_ref, *, chunk):
    x = x_ref[0]
    ms = jnp.mean(x * x, axis=-1, keepdims=True)
    n = (x * lax.rsqrt(ms + EPS) * g1_ref[...]).astype(BF16)
    pt = _nt_dot(wt_ref[...], n)
    kb_ref[0] = jnp.dot(n, wkb_ref[...], preferred_element_type=F32).astype(BF16)
    cos = cos_ref[...]
    sin = sin_ref[...]
    q = HEAD_DIM // 4

    def norm_rope(blk, gain):
        ss = jnp.mean(blk * blk, axis=0, keepdims=True)
        y = blk * lax.rsqrt(ss + EPS) * gain
        swapped = jnp.concatenate([y[q:2 * q], y[0:q], y[3 * q:4 * q], y[2 * q:3 * q]], axis=0)
        return y * cos + swapped * sin

    qn = qn_ref[...]
    kn = kn_ref[...]
    for h in range(N_HEADS_A):
        blk = pt[h * HEAD_DIM:(h + 1) * HEAD_DIM]
        qa_ref[0, h * HEAD_DIM:(h + 1) * HEAD_DIM, :] = (norm_rope(blk, qn) * Q_SCALE).astype(BF16)
    o0 = WIDTH_A
    ka = jnp.concatenate(
        [norm_rope(pt[o0 + j * HEAD_DIM:o0 + (j + 1) * HEAD_DIM], kn) for j in range(N_KV_A)], axis=0)
    ka_ref[0] = ka.T.astype(BF16)
    o1 = o0 + KV_A
    o2 = o1 + KV_A
    o3 = o2 + WIDTH_B
    tm = x.shape[0]
    vd = 2 * HEAD_DIM
    ones_rows = (lax.broadcasted_iota(jnp.int32, (BF16_ROWS, chunk), 0) == 0).astype(BF16)
    for j in range(tm // chunk):
        cols = slice(j * chunk, (j + 1) * chunk)
        for g in range(N_KV_A):
            va_ref[0, j, g * VA_ROWS:g * VA_ROWS + HEAD_DIM] = (
                pt[o1 + g * HEAD_DIM:o1 + (g + 1) * HEAD_DIM, cols].astype(BF16))
            va_ref[0, j, g * VA_ROWS + HEAD_DIM:(g + 1) * VA_ROWS] = ones_rows
        for h in range(N_HEADS_B):
            vb_ref[0, j, h * VB_ROWS:h * VB_ROWS + vd] = pt[o3 + h * vd:o3 + (h + 1) * vd, cols].astype(BF16)
            vb_ref[0, j, h * VB_ROWS + vd:(h + 1) * VB_ROWS] = ones_rows
    qb_ref[0] = (pt[o2:o3] * Q_SCALE).astype(BF16)


def _inproj(x, g1, wt, wkb, cos_t, sin_t, qn, kn, *, tm, chunk):
    bsz, n, d = x.shape
    rows_t = wt.shape[0]
    grid = (bsz, n // tm)
    full = lambda shape: pl.BlockSpec(shape, lambda b, i: (0,) * len(shape))
    va_rows = N_KV_A * VA_ROWS
    vb_rows = N_HEADS_B * VB_ROWS
    out_shape = (
        jax.ShapeDtypeStruct((bsz, WIDTH_A, n), BF16),
        jax.ShapeDtypeStruct((bsz, n, KV_A), BF16),
        jax.ShapeDtypeStruct((bsz, n // chunk, va_rows, chunk), BF16),
        jax.ShapeDtypeStruct((bsz, WIDTH_B, n), BF16),
        jax.ShapeDtypeStruct((bsz, n, WIDTH_B), BF16),
        jax.ShapeDtypeStruct((bsz, n // chunk, vb_rows, chunk), BF16),
    )
    return pl.pallas_call(
        functools.partial(_inproj_kernel, chunk=chunk),
        grid=grid,
        in_specs=[
            pl.BlockSpec((1, tm, d), lambda b, i: (b, i, 0)),
            full((1, d)),
            full((rows_t, d)),
            full((d, WIDTH_B)),
            pl.BlockSpec((HEAD_DIM, tm), lambda b, i: (0, i)),
            pl.BlockSpec((HEAD_DIM, tm), lambda b, i: (0, i)),
            full((HEAD_DIM, 1)),
            full((HEAD_DIM, 1)),
        ],
        out_specs=(
            pl.BlockSpec((1, WIDTH_A, tm), lambda b, i: (b, 0, i)),
            pl.BlockSpec((1, tm, KV_A), lambda b, i: (b, i, 0)),
            pl.BlockSpec((1, tm // chunk, va_rows, chunk), lambda b, i: (b, i, 0, 0)),
            pl.BlockSpec((1, WIDTH_B, tm), lambda b, i: (b, 0, i)),
            pl.BlockSpec((1, tm, WIDTH_B), lambda b, i: (b, i, 0)),
            pl.BlockSpec((1, tm // chunk, vb_rows, chunk), lambda b, i: (b, i, 0, 0)),
        ),
        out_shape=out_shape,
        compiler_params=pltpu.CompilerParams(
            dimension_semantics=("arbitrary", "arbitrary"), vmem_limit_bytes=VMEM_LIMIT),
        name="inproj",
    )(x, g1, wt, wkb, cos_t, sin_t, qn, kn)


def _online_update(s, c_shift, m_ref, acc_ref, vblk, idx):
    mc = jnp.max(s, axis=0, keepdims=True)
    if c_shift is not None:
        mc = mc + c_shift
    m_old = m_ref[idx]
    m_new = jnp.maximum(m_old, mc)
    alpha = jnp.exp2(m_old - m_new)
    shift = m_new if c_shift is None else m_new - c_shift
    p = jnp.exp2(s - shift).astype(BF16)
    pv = jnp.dot(vblk, p, preferred_element_type=F32)
    acc_ref[idx] = alpha * acc_ref[idx] + pv
    m_ref[idx] = m_new


def _init_from_meta(s, m_ref, acc_ref, vmeta, idx):
    m0 = jnp.max(s, axis=0, keepdims=True)
    p = jnp.exp2(s - m0).astype(BF16)
    m_ref[idx] = m0
    acc_ref[idx] = jnp.dot(vmeta, p, preferred_element_type=F32)


def _pipelined_chunks(count, chunk_of, qk_into, consume, sa_ref, sb_ref):
    assert count >= 2 and count % 2 == 0
    qk_into(sa_ref, chunk_of(0))

    def body(j, carry):
        i = 2 * j
        qk_into(sb_ref, chunk_of(i + 1))
        consume(sa_ref, chunk_of(i))
        qk_into(sa_ref, chunk_of(i + 2))
        consume(sb_ref, chunk_of(i + 1))
        return carry

    lax.fori_loop(0, count // 2 - 1, body, 0)
    qk_into(sb_ref, chunk_of(count - 1))
    consume(sa_ref, chunk_of(count - 2))
    consume(sb_ref, chunk_of(count - 1))


def _attn_a_kernel(q_ref, k_ref, v_ref, km_ref, vm_ref, o_ref, qpad_ref, sa_ref, sb_ref, m_ref, acc_ref,
                   *, tq, nk):
    g = pl.program_id(1)
    zero = jnp.zeros((HEAD_DIM, tq), BF16)
    for h in range(GQA_GROUP):
        qh = q_ref[0, h * HEAD_DIM:(h + 1) * HEAD_DIM, :]
        qpad_ref[0:HEAD_DIM, h * tq:(h + 1) * tq] = jnp.where(g == 0, qh, zero)
        qpad_ref[HEAD_DIM:2 * HEAD_DIM, h * tq:(h + 1) * tq] = jnp.where(g == 1, qh, zero)

    valid = lax.broadcasted_iota(jnp.int32, (META_PAD, 1), 0) < N_META
    s = jnp.dot(km_ref[...], qpad_ref[...], preferred_element_type=F32)
    s = jnp.where(valid, s, MASK_VALUE)
    vmeta = vm_ref[...]
    for h in range(GQA_GROUP):
        _init_from_meta(s[:, h * tq:(h + 1) * tq], m_ref, acc_ref, vmeta, h)

    def qk_into(dst, kc):
        dst[...] = jnp.dot(k_ref[0, kc], qpad_ref[...], preferred_element_type=F32)

    def consume(src, kc):
        vblk = v_ref[0, kc]
        for h in range(GQA_GROUP):
            _online_update(src[:, h * tq:(h + 1) * tq], None, m_ref, acc_ref, vblk, h)

    _pipelined_chunks(nk, lambda i: i, qk_into, consume, sa_ref, sb_ref)

    outs = []
    for h in range(GQA_GROUP):
        outs.append(acc_ref[h, 0:HEAD_DIM, :] / acc_ref[h, HEAD_DIM:HEAD_DIM + 1, :])
    o = jnp.concatenate(outs, axis=0)
    o_ref[0] = o.T.astype(BF16)


def _attn_a(qa_t, ka, va_t, km, vm_t, *, tq):
    bsz, _, n = qa_t.shape
    nk = n // KEY_CHUNK
    ka4 = ka.reshape(bsz, nk, KEY_CHUNK, KV_A)
    gw = GQA_GROUP * HEAD_DIM
    return pl.pallas_call(
        functools.partial(_attn_a_kernel, tq=tq, nk=nk),
        grid=(bsz, N_KV_A, n // tq),
        in_specs=[
            pl.BlockSpec((1, gw, tq), lambda b, g, i: (b, g, i)),
            pl.BlockSpec((1, nk, KEY_CHUNK, KV_A), lambda b, g, i: (b, 0, 0, 0)),
            pl.BlockSpec((1, nk, VA_ROWS, KEY_CHUNK), lambda b, g, i: (b, 0, g, 0)),
            pl.BlockSpec((META_PAD, KV_A), lambda b, g, i: (0, 0)),
            pl.BlockSpec((VA_ROWS, META_PAD), lambda b, g, i: (g, 0)),
        ],
        out_specs=pl.BlockSpec((1, tq, gw), lambda b, g, i: (b, i, g)),
        out_shape=jax.ShapeDtypeStruct((bsz, n, WIDTH_A), BF16),
        scratch_shapes=[
            pltpu.VMEM((KV_A, GQA_GROUP * tq), BF16),
            pltpu.VMEM((KEY_CHUNK, GQA_GROUP * tq), F32),
            pltpu.VMEM((KEY_CHUNK, GQA_GROUP * tq), F32),
            pltpu.VMEM((GQA_GROUP, 1, tq), F32),
            pltpu.VMEM((GQA_GROUP, VA_ROWS, tq), F32),
        ],
        compiler_params=pltpu.CompilerParams(
            dimension_semantics=("arbitrary", "arbitrary", "arbitrary"), vmem_limit_bytes=VMEM_LIMIT),
        name="attn_a",
    )(qa_t, ka4, va_t, km, vm_t)


def _attn_b_kernel(cfar_ref, q_ref, k_ref, v_ref, km_ref, vm_ref, band_ref, mbias_ref,
                   lq1_ref, lk1_ref, lq2_ref, lk2_ref, sn_ref, o_ref,
                   qblk_ref, sa_ref, sb_ref, m_ref, acc_ref, *, tq, nk):
    h = pl.program_id(1)
    qi = pl.program_id(2)
    vd = 2 * HEAD_DIM
    zero = jnp.zeros((HEAD_DIM, tq), BF16)
    qblk_ref[0:HEAD_DIM, 0:tq] = q_ref[0, 0:HEAD_DIM, :]
    qblk_ref[0:HEAD_DIM, tq:2 * tq] = zero
    qblk_ref[HEAD_DIM:vd, 0:tq] = zero
    qblk_ref[HEAD_DIM:vd, tq:2 * tq] = q_ref[0, HEAD_DIM:vd, :]

    s = jnp.dot(km_ref[...], qblk_ref[...], preferred_element_type=F32)
    mb = mbias_ref[0]
    vmeta = vm_ref[...]
    for j in range(2):
        _init_from_meta(s[:, j * tq:(j + 1) * tq] + mb, m_ref, acc_ref, vmeta, j)

    def qk_into(dst, info):
        dst[...] = jnp.dot(k_ref[0, info[0]], qblk_ref[...], preferred_element_type=F32)

    near_lo = jnp.clip(qi * TILE_CHUNKS - 1, 0, nk - NEAR_CHUNKS)
    c_neg = cfar_ref[h, 0]
    c_pos = cfar_ref[h, 1]

    def far_chunk(i):
        right = i >= near_lo
        return jnp.where(right, i + NEAR_CHUNKS, i), jnp.where(right, c_pos, c_neg)

    def consume_far(src, info):
        kc, c = info
        vblk = v_ref[0, kc]
        for j in range(2):
            _online_update(src[:, j * tq:(j + 1) * tq], c, m_ref, acc_ref, vblk, j)

    def near_chunk(i):
        kc = near_lo + i
        return kc, pl.multiple_of((kc - qi * TILE_CHUNKS + BAND_SHIFT) * KEY_CHUNK, KEY_CHUNK)

    def consume_near(src, info):
        kc, r0 = info
        vblk = v_ref[0, kc]
        bias = band_ref[0, pl.ds(r0, KEY_CHUNK), :]
        for j in range(2):
            _online_update(src[:, j * tq:(j + 1) * tq] + bias, None, m_ref, acc_ref, vblk, j)

    _pipelined_chunks(nk - NEAR_CHUNKS, far_chunk, qk_into, consume_far, sa_ref, sb_ref)
    _pipelined_chunks(NEAR_CHUNKS, near_chunk, qk_into, consume_near, sa_ref, sb_ref)

    lam = (jnp.exp(jnp.sum(lq1_ref[...] * lk1_ref[...], axis=-1, keepdims=True))
           - jnp.exp(jnp.sum(lq2_ref[...] * lk2_ref[...], axis=-1, keepdims=True)) + LAM_INIT)
    o = (acc_ref[0, 0:vd, :] / acc_ref[0, vd:vd + 1, :]
         - lam * (acc_ref[1, 0:vd, :] / acc_ref[1, vd:vd + 1, :]))
    ms = jnp.mean(o * o, axis=0, keepdims=True)
    o = o * lax.rsqrt(ms + EPS) * sn_ref[...] * (1.0 - LAM_INIT)
    o_ref[0] = o.T.astype(BF16)


def _attn_b(qb_t, kb, vb_t, kmb, vmb_t, band, mbias, cfar, lq1, lk1, lq2, lk2, sn, *, tq):
    bsz, _, n = qb_t.shape
    nk = n // KEY_CHUNK
    assert tq == Q_TILE and NEAR_CHUNKS % 2 == 0 and (nk - NEAR_CHUNKS) % 2 == 0 and nk - NEAR_CHUNKS >= 2
    vd = 2 * HEAD_DIM
    kb4 = kb.reshape(bsz, nk, KEY_CHUNK, WIDTH_B)
    vec = lambda: pl.BlockSpec((1, HEAD_DIM), lambda b, h, i: (0, 0))
    return pl.pallas_call(
        functools.partial(_attn_b_kernel, tq=tq, nk=nk),
        grid=(bsz, N_HEADS_B, n // tq),
        in_specs=[
            pl.BlockSpec(memory_space=pltpu.SMEM),
            pl.BlockSpec((1, vd, tq), lambda b, h, i: (b, h, i)),
            pl.BlockSpec((1, nk, KEY_CHUNK, vd), lambda b, h, i: (b, 0, 0, h)),
            pl.BlockSpec((1, nk, VB_ROWS, KEY_CHUNK), lambda b, h, i: (b, 0, h, 0)),
            pl.BlockSpec((META_PAD, vd), lambda b, h, i: (0, h)),
            pl.BlockSpec((VB_ROWS, META_PAD), lambda b, h, i: (h, 0)),
            pl.BlockSpec((1, BAND_ROWS, tq), lambda b, h, i: (h, 0, 0)),
            pl.BlockSpec((1, META_PAD, tq), lambda b, h, i: (h, 0, i)),
            vec(), vec(), vec(), vec(),
            pl.BlockSpec((vd, 1), lambda b, h, i: (0, 0)),
        ],
        out_specs=pl.BlockSpec((1, tq, vd), lambda b, h, i: (b, i, h)),
        out_shape=jax.ShapeDtypeStruct((bsz, n, WIDTH_B), BF16),
        scratch_shapes=[
            pltpu.VMEM((vd, 2 * tq), BF16),
            pltpu.VMEM((KEY_CHUNK, 2 * tq), F32),
            pltpu.VMEM((KEY_CHUNK, 2 * tq), F32),
            pltpu.VMEM((2, 1, tq), F32),
            pltpu.VMEM((2, VB_ROWS, tq), F32),
        ],
        compiler_params=pltpu.CompilerParams(
            dimension_semantics=("arbitrary", "arbitrary", "arbitrary"), vmem_limit_bytes=VMEM_LIMIT),
        name="attn_b",
    )(cfar, qb_t, kb4, vb_t, kmb, vmb_t, band, mbias, lq1, lk1, lq2, lk2, sn)


def _outproj_kernel(oa_ref, ob_ref, x_ref, wa_ref, wb_ref, g2_ref, wrh_ref, wrl_ref, br_ref,
                    h_ref, n2_ref, eid_ref, gate_ref):
    att = (jnp.dot(oa_ref[0], wa_ref[...], preferred_element_type=F32)
           + jnp.dot(ob_ref[0], wb_ref[...], preferred_element_type=F32))
    hres = x_ref[0] + att
    h_ref[0] = hres
    ms = jnp.mean(hres * hres, axis=-1, keepdims=True)
    n2 = hres * lax.rsqrt(ms + EPS) * g2_ref[...]
    n2_hi = n2.astype(BF16)
    n2_lo = (n2 - n2_hi.astype(F32)).astype(BF16)
    n2_ref[0] = n2_hi
    wrh = wrh_ref[...]
    logits = _nt_dot(wrh, n2_hi) + _nt_dot(wrh, n2_lo) + _nt_dot(wrl_ref[...], n2_hi) + br_ref[...]
    tm = hres.shape[0]
    gl = logits[N_EXPERTS:N_EXPERTS + N_GROUPS]
    gmax = jnp.max(gl, axis=0, keepdims=True)
    giota = lax.broadcasted_iota(jnp.int32, (N_GROUPS, tm), 0)
    g_idx = jnp.min(jnp.where(gl == gmax, giota, N_GROUPS), axis=0, keepdims=True)
    g_w = 1.0 / jnp.sum(jnp.exp(gl - gmax), axis=0, keepdims=True)
    sel = jnp.zeros((EXPERTS_PER_GROUP, tm), F32)
    for g in range(N_GROUPS):
        sel = jnp.where(g_idx == g, logits[g * EXPERTS_PER_GROUP:(g + 1) * EXPERTS_PER_GROUP], sel)
    eiota = lax.broadcasted_iota(jnp.int32, (EXPERTS_PER_GROUP, tm), 0)
    e1 = jnp.max(sel, axis=0, keepdims=True)
    i1 = jnp.min(jnp.where(sel == e1, eiota, EXPERTS_PER_GROUP), axis=0, keepdims=True)
    sel2 = jnp.where(eiota == i1, -jnp.inf, sel)
    e2 = jnp.max(sel2, axis=0, keepdims=True)
    i2 = jnp.min(jnp.where(sel2 == e2, eiota, EXPERTS_PER_GROUP), axis=0, keepdims=True)
    r = jnp.exp(e2 - e1)
    w1 = g_w / (1.0 + r)
    eid_ref[0] = jnp.concatenate([g_idx * EXPERTS_PER_GROUP + i1, g_idx * EXPERTS_PER_GROUP + i2], axis=0)
    gate_ref[0] = jnp.concatenate([w1, w1 * r], axis=0)


def _outproj(oa, ob, x, wa, wb, g2, wrh, wrl, br, *, tm):
    bsz, n, d = x.shape
    full = lambda shape: pl.BlockSpec(shape, lambda b, i: (0,) * len(shape))
    tok = lambda w: pl.BlockSpec((1, tm, w), lambda b, i: (b, i, 0))
    lane = lambda: pl.BlockSpec((1, TOP_K, tm), lambda b, i: (b, 0, i))
    return pl.pallas_call(
        _outproj_kernel,
        grid=(bsz, n // tm),
        in_specs=[tok(WIDTH_A), tok(WIDTH_B), tok(d), full((WIDTH_A, d)), full((WIDTH_B, d)), full((1, d)),
                  full((ROUTER_ROWS, d)), full((ROUTER_ROWS, d)), full((ROUTER_ROWS, 1))],
        out_specs=(tok(d), tok(d), lane(), lane()),
        out_shape=(
            jax.ShapeDtypeStruct((bsz, n, d), F32),
            jax.ShapeDtypeStruct((bsz, n, d), BF16),
            jax.ShapeDtypeStruct((bsz, TOP_K, n), jnp.int32),
            jax.ShapeDtypeStruct((bsz, TOP_K, n), F32),
        ),
        compiler_params=pltpu.CompilerParams(
            dimension_semantics=("arbitrary", "arbitrary"), vmem_limit_bytes=VMEM_LIMIT),
        name="outproj_router",
    )(oa, ob, x, wa, wb, g2, wrh, wrl, br)


def _expert_kernel(be_ref, nb_ref, x_ref, wg_ref, wu_ref, wd_ref, y_ref):
    i = pl.program_id(0)

    @pl.when(i < nb_ref[0])
    def _():
        x = x_ref[...]
        gt = jnp.dot(x, wg_ref[0], preferred_element_type=F32)
        up = jnp.dot(x, wu_ref[0], preferred_element_type=F32)
        act = gt * (1.0 / (1.0 + jnp.exp(-gt))) * up
        y_ref[...] = jnp.dot(act.astype(BF16), wd_ref[0], preferred_element_type=F32).astype(BF16)

    @pl.when(i >= nb_ref[0])
    def _():
        y_ref[...] = jnp.zeros_like(y_ref)


def _experts(block_e, n_used, xs, wg, wu, wd, *, rows):
    cap, d = xs.shape
    grid_spec = pltpu.PrefetchScalarGridSpec(
        num_scalar_prefetch=2,
        grid=(cap // rows,),
        in_specs=[
            pl.BlockSpec((rows, d), lambda i, be, nb: (i, 0)),
            pl.BlockSpec((1, d, D_EXPERT), lambda i, be, nb: (be[i], 0, 0)),
            pl.BlockSpec((1, d, D_EXPERT), lambda i, be, nb: (be[i], 0, 0)),
            pl.BlockSpec((1, D_EXPERT, d), lambda i, be, nb: (be[i], 0, 0)),
        ],
        out_specs=pl.BlockSpec((rows, d), lambda i, be, nb: (i, 0)),
    )
    return pl.pallas_call(
        _expert_kernel,
        grid_spec=grid_spec,
        out_shape=jax.ShapeDtypeStruct((cap, d), BF16),
        compiler_params=pltpu.CompilerParams(
            dimension_semantics=("arbitrary",), vmem_limit_bytes=VMEM_LIMIT),
        name="experts",
    )(block_e, n_used, xs, wg, wu, wd)


def _final_kernel(h_ref, y0_ref, y1_ref, gate_ref, gf_ref, o_ref):
    gates = gate_ref[...]
    hres = (h_ref[...] + gates[:, 0:1] * y0_ref[...].astype(F32)
            + gates[:, 1:2] * y1_ref[...].astype(F32))
    ms = jnp.mean(hres * hres, axis=-1, keepdims=True)
    o_ref[...] = hres * lax.rsqrt(ms + EPS) * gf_ref[...]


def _final(h, y0, y1, gates, gf, *, tm):
    t, d = h.shape
    tok = lambda w: pl.BlockSpec((tm, w), lambda i: (i, 0))
    return pl.pallas_call(
        _final_kernel,
        grid=(t // tm,),
        in_specs=[tok(d), tok(d), tok(d), tok(TOP_K), pl.BlockSpec((1, d), lambda i: (0, 0))],
        out_specs=tok(d),
        out_shape=jax.ShapeDtypeStruct((t, d), F32),
        compiler_params=pltpu.CompilerParams(
            dimension_semantics=("arbitrary",), vmem_limit_bytes=VMEM_LIMIT),
        name="combine_final_norm",
    )(h, y0, y1, gates, gf)


def _rope_tables(n_tok):
    rows = n_tok // GRID_W
    row = jnp.repeat(jnp.arange(rows, dtype=F32), GRID_W)
    col = jnp.tile(jnp.arange(GRID_W, dtype=F32), rows)
    inv = ROPE_THETA ** (-jnp.arange(0, AXIS_DIM, 2, dtype=F32) / AXIS_DIM)
    ar = row[:, None] * inv
    ac = col[:, None] * inv
    ang = jnp.concatenate([ar, ar, ac, ac], axis=-1)
    q = HEAD_DIM // 4
    sign = jnp.concatenate([-jnp.ones((q,), F32), jnp.ones((q,), F32)] * 2)
    return jnp.cos(ang).T, (jnp.sin(ang) * sign).T


def _t5_bucket(rp):
    nb = N_BUCKETS // 2
    max_exact = nb // 2
    ret = (rp > 0).astype(jnp.int32) * nb
    n = jnp.abs(rp)
    large = max_exact + (jnp.log(jnp.maximum(n, 1).astype(F32) / max_exact)
                         / math.log(MAX_DISTANCE / max_exact) * (nb - max_exact)).astype(jnp.int32)
    large = jnp.minimum(large, nb - 1)
    return ret + jnp.where(n < max_exact, n, large)


def _bias_tables(rel_bias, n_tok, tq):
    rb = rel_bias.astype(F32) * LOG2E
    r = jnp.arange(BAND_ROWS, dtype=jnp.int32)[:, None]
    c = jnp.arange(tq, dtype=jnp.int32)[None, :]
    band = jnp.transpose(rb[_t5_bucket(r - BAND_SHIFT * KEY_CHUNK - c)], (2, 0, 1))
    far = jnp.array([-2 * MAX_DISTANCE, 2 * MAX_DISTANCE], jnp.int32)
    cfar = rb[_t5_bucket(far)].T
    kpos = jnp.arange(N_META, dtype=jnp.int32)[:, None]
    qpos = N_META + jnp.arange(n_tok, dtype=jnp.int32)[None, :]
    mb = jnp.transpose(rb[_t5_bucket(kpos - qpos)], (2, 0, 1))
    pad = jnp.full((N_HEADS_B, META_PAD - N_META, n_tok), MASK_VALUE, F32)
    return band, cfar, jnp.concatenate([mb, pad], axis=1)


def _dispatch_plan(eid, rows):
    t = eid.shape[1]
    n_assign = TOP_K * t
    flat = eid.reshape(-1)
    a_idx = jnp.arange(n_assign, dtype=jnp.int32)
    skey = jnp.sort(flat * n_assign + a_idx)
    s_assign = skey % n_assign
    onehot = (flat[:, None] == jnp.arange(N_EXPERTS, dtype=jnp.int32)[None, :]).astype(jnp.int32)
    csum = jnp.cumsum(onehot, axis=0)
    counts = csum[-1]
    rank = jnp.take_along_axis(csum, flat[:, None], axis=1)[:, 0] - 1
    padded = (counts + rows - 1) // rows * rows
    pend = jnp.cumsum(padded)
    poff = pend - padded
    off = jnp.cumsum(counts) - counts
    pos = poff[flat] + rank
    n_blocks = (n_assign + N_EXPERTS * (rows - 1) + rows - 1) // rows
    cap = n_blocks * rows
    block_start = jnp.arange(n_blocks, dtype=jnp.int32) * rows
    block_e = jnp.minimum(jnp.searchsorted(pend, block_start, side='right'), N_EXPERTS - 1).astype(jnp.int32)
    d = jnp.arange(cap, dtype=jnp.int32)
    e_d = jnp.repeat(block_e, rows)
    j = d - poff[e_d]
    valid = j < counts[e_d]
    src = jnp.clip(off[e_d] + j, 0, n_assign - 1)
    row_tok = jnp.where(valid, s_assign[src] % t, 0)
    n_used = (pend[-1] // rows).astype(jnp.int32).reshape(1)
    return row_tok, block_e, n_used, pos.reshape(TOP_K, t)


def _encode(x, shared):
    bsz, n_tok, d = x.shape
    assert n_tok % Q_TILE == 0 and n_tok % TOKEN_TILE == 0
    tq = Q_TILE
    tm = TOKEN_TILE
    cos_t, sin_t = _rope_tables(n_tok)
    qa_t, ka, va_t, qb_t, kb, vb_t = _inproj(
        x, shared['g1'], shared['wt'], shared['wkb'], cos_t, sin_t, shared['qn'], shared['kn'],
        tm=tm, chunk=KEY_CHUNK)
    oa = _attn_a(qa_t, ka, va_t, shared['km_a'], shared['vm_a'], tq=tq)
    band, cfar, mbias = _bias_tables(shared['rel_bias'], n_tok, tq)
    ob = _attn_b(qb_t, kb, vb_t, shared['km_b'], shared['vm_b'], band, mbias, cfar,
                 shared['lq1'], shared['lk1'], shared['lq2'], shared['lk2'], shared['sn'], tq=tq)
    h, n2, eid, gates = _outproj(oa, ob, x, shared['wa'], shared['wb'], shared['g2'],
                                 shared['wrh'], shared['wrl'], shared['br'], tm=tm)
    t = bsz * n_tok
    eid = jnp.transpose(eid, (1, 0, 2)).reshape(TOP_K, t)
    gates = jnp.transpose(gates, (1, 0, 2)).reshape(TOP_K, t)
    row_tok, block_e, n_used, pos = _dispatch_plan(eid, MOE_ROWS)
    xs = jnp.take(n2.reshape(t, d), row_tok, axis=0)
    y = _experts(block_e, n_used, xs, shared['wg'], shared['wu'], shared['wd'], rows=MOE_ROWS)
    y0 = jnp.take(y, pos[0], axis=0)
    y1 = jnp.take(y, pos[1], axis=0)
    out = _final(h.reshape(t, d), y0, y1, gates.T, shared['gf'], tm=tm)
    return out.reshape(bsz, n_tok, d)


def kernel(x_prompt, x_sample, meta_tokens, rel_bias, norm1, w_in, q_norm, k_norm, lambda_q1, lambda_k1,
           lambda_q2, lambda_k2, sub_norm, w_out, norm2, w_router_group, b_router_group, w_router_expert,
           b_router_expert, w_gate, w_up, w_down, final_norm):
    i = LAYER
    d = D_MODEL
    o0 = WIDTH_A
    o1 = o0 + KV_A
    o2 = o1 + KV_A
    o3 = o2 + WIDTH_B
    o4 = o3 + WIDTH_B
    w = w_in[i]
    wt = jnp.concatenate([w[:, :o2], w[:, o2:o3], w[:, o4:]], axis=1).T.astype(BF16)
    wr = jnp.concatenate([w_router_expert[i], w_router_group[i],
                          jnp.zeros((d, ROUTER_ROWS - N_EXPERTS - N_GROUPS), F32)], axis=1).T
    wrh = wr.astype(BF16)
    br = jnp.concatenate([b_router_expert[i], b_router_group[i],
                          jnp.zeros((ROUTER_ROWS - N_EXPERTS - N_GROUPS,), F32)]).reshape(ROUTER_ROWS, 1)
    shared = {
        'g1': norm1[i].reshape(1, d), 'wt': wt, 'wkb': w[:, o3:o4].astype(BF16),
        'qn': q_norm[i].reshape(HEAD_DIM, 1), 'kn': k_norm[i].reshape(HEAD_DIM, 1),
        'rel_bias': rel_bias,
        'lq1': lambda_q1[i].reshape(1, HEAD_DIM), 'lk1': lambda_k1[i].reshape(1, HEAD_DIM),
        'lq2': lambda_q2[i].reshape(1, HEAD_DIM), 'lk2': lambda_k2[i].reshape(1, HEAD_DIM),
        'sn': sub_norm[i].reshape(2 * HEAD_DIM, 1),
        'wa': w_out[i][:WIDTH_A].astype(BF16), 'wb': w_out[i][WIDTH_A:].astype(BF16),
        'g2': norm2[i].reshape(1, d), 'wrh': wrh, 'wrl': (wr - wrh.astype(F32)).astype(BF16), 'br': br,
        'wg': w_gate[i].astype(BF16), 'wu': w_up[i].astype(BF16), 'wd': w_down[i].astype(BF16),
        'gf': final_norm.reshape(1, d),
    }
    xm = jnp.concatenate([meta_tokens.astype(F32), jnp.zeros((META_PAD - N_META, d), F32)], axis=0)[None]
    ones = jnp.ones((HEAD_DIM, META_PAD), F32)
    _, km_a, vm_a, _, km_b, vm_b = _inproj(
        xm, shared['g1'], shared['wt'], shared['wkb'], ones, 0.0 * ones, shared['qn'], shared['kn'],
        tm=META_PAD, chunk=META_PAD)
    shared['km_a'] = km_a[0]
    shared['vm_a'] = vm_a[0, 0]
    shared['km_b'] = km_b[0]
    shared['vm_b'] = vm_b[0, 0]
    return _encode(x_prompt, shared), _encode(x_sample, shared)
```

```python
import functools
import math

import jax
import jax.numpy as jnp
from jax import lax
from jax.experimental import pallas as pl
from jax.experimental.pallas import tpu as pltpu

D_MODEL = 1024
HEAD_DIM = 64
N_HEADS_A = 8
N_KV_A = 2
GQA_GROUP = N_HEADS_A // N_KV_A
N_HEADS_B = 4
AXIS_DIM = HEAD_DIM // 2
ROPE_THETA = 10000.0
GRID_W = 64
N_META = 16
N_BUCKETS = 32
MAX_DISTANCE = 128
WIDTH_A = N_HEADS_A * HEAD_DIM
WIDTH_B = N_HEADS_B * 2 * HEAD_DIM
KV_A = N_KV_A * HEAD_DIM
N_GROUPS = 4
EXPERTS_PER_GROUP = 8
N_EXPERTS = N_GROUPS * EXPERTS_PER_GROUP
TOP_K = 2
D_EXPERT = D_MODEL // 2
EPS = 1e-6
LAYER = 0
LAM_INIT = 0.8 - 0.6 * math.exp(-0.3 * LAYER)

LOG2E = 1.4426950408889634
Q_SCALE = (HEAD_DIM ** -0.5) * LOG2E
MASK_VALUE = -1e30

LANE = 128
BF16_ROWS = 16
META_PAD = LANE
KEY_CHUNK = 256
TOKEN_TILE = 512
Q_TILE = 512
MOE_ROWS = 512
ROUTER_ROWS = 40
VMEM_LIMIT = 48 * 1024 * 1024

VA_ROWS = HEAD_DIM + BF16_ROWS
VB_ROWS = 2 * HEAD_DIM + BF16_ROWS
TILE_CHUNKS = Q_TILE // KEY_CHUNK
NEAR_CHUNKS = TILE_CHUNKS + 2
BAND_SHIFT = 2
BAND_ROWS = (TILE_CHUNKS + 5) * KEY_CHUNK

BOUND_MARGIN = 1.01
L_MIN = 2.0 ** -64
NORM_ROWS = 8
PART_LANES = 512
FAST_UNROLL = 4

F32 = jnp.float32
BF16 = jnp.bfloat16


def _nt_dot(a, b):
    return lax.dot_general(a, b, (((1,), (1,)), ((), ())), preferred_element_type=F32)


def _inproj_kernel(x_ref, g1_ref, wt_ref, wkb_ref, cos_ref, sin_ref, qn_ref, kn_ref,
                   qa_ref, ka_ref, va_ref, qb_ref, kb_ref, vb_ref, knorm_ref, *, chunk):
    x = x_ref[0]
    ms = jnp.mean(x * x, axis=-1, keepdims=True)
    n = (x * lax.rsqrt(ms + EPS) * g1_ref[...]).astype(BF16)
    pt = _nt_dot(wt_ref[...], n)
    kb = jnp.dot(n, wkb_ref[...], preferred_element_type=F32).astype(BF16)
    kb_ref[0] = kb
    kbf = kb.astype(F32)
    seg = (lax.broadcasted_iota(jnp.int32, (WIDTH_B, LANE), 0) // HEAD_DIM
           == lax.broadcasted_iota(jnp.int32, (WIDTH_B, LANE), 1)).astype(BF16)
    kb_sq = jnp.dot((kbf * kbf).astype(BF16), seg, preferred_element_type=F32)
    kb_max = jnp.max(kb_sq, axis=0, keepdims=True)
    cos = cos_ref[...]
    sin = sin_ref[...]
    q = HEAD_DIM // 4

    def norm_rope(blk, gain):
        ss = jnp.mean(blk * blk, axis=0, keepdims=True)
        y = blk * lax.rsqrt(ss + EPS) * gain
        swapped = jnp.concatenate([y[q:2 * q], y[0:q], y[3 * q:4 * q], y[2 * q:3 * q]], axis=0)
        return y * cos + swapped * sin

    qn = qn_ref[...]
    kn = kn_ref[...]
    for h in range(N_HEADS_A):
        blk = pt[h * HEAD_DIM:(h + 1) * HEAD_DIM]
        qa_ref[0, h * HEAD_DIM:(h + 1) * HEAD_DIM, :] = (norm_rope(blk, qn) * Q_SCALE).astype(BF16)
    o0 = WIDTH_A
    ka = jnp.concatenate(
        [norm_rope(pt[o0 + j * HEAD_DIM:o0 + (j + 1) * HEAD_DIM], kn) for j in range(N_KV_A)], axis=0)
    ka_ref[0] = ka.T.astype(BF16)
    kaf = ka.astype(BF16).astype(F32)
    ka_sq = kaf * kaf
    ka_max = [jnp.max(jnp.sum(ka_sq[g * HEAD_DIM:(g + 1) * HEAD_DIM], axis=0, keepdims=True),
                      axis=1, keepdims=True) for g in range(N_KV_A)]
    knorm_ref[0, 0] = jnp.concatenate(
        [kb_max] + [jnp.broadcast_to(v, (1, LANE)) for v in ka_max]
        + [jnp.zeros((NORM_ROWS - 1 - N_KV_A, LANE), F32)], axis=0)
    o1 = o0 + KV_A
    o2 = o1 + KV_A
    o3 = o2 + WIDTH_B
    tm = x.shape[0]
    vd = 2 * HEAD_DIM
    ones_rows = (lax.broadcasted_iota(jnp.int32, (BF16_ROWS, chunk), 0) == 0).astype(BF16)
    for j in range(tm // chunk):
        cols = slice(j * chunk, (j + 1) * chunk)
        for g in range(N_KV_A):
            va_ref[0, j, g * VA_ROWS:g * VA_ROWS + HEAD_DIM] = (
                pt[o1 + g * HEAD_DIM:o1 + (g + 1) * HEAD_DIM, cols].astype(BF16))
            va_ref[0, j, g * VA_ROWS + HEAD_DIM:(g + 1) * VA_ROWS] = ones_rows
        for h in range(N_HEADS_B):
            vb_ref[0, j, h * VB_ROWS:h * VB_ROWS + vd] = pt[o3 + h * vd:o3 + (h + 1) * vd, cols].astype(BF16)
            vb_ref[0, j, h * VB_ROWS + vd:(h + 1) * VB_ROWS] = ones_rows
    qb_ref[0] = (pt[o2:o3] * Q_SCALE).astype(BF16)


def _inproj(x, g1, wt, wkb, cos_t, sin_t, qn, kn, *, tm, chunk):
    bsz, n, d = x.shape
    rows_t = wt.shape[0]
    grid = (bsz, n // tm)
    full = lambda shape: pl.BlockSpec(shape, lambda b, i: (0,) * len(shape))
    va_rows = N_KV_A * VA_ROWS
    vb_rows = N_HEADS_B * VB_ROWS
    out_shape = (
        jax.ShapeDtypeStruct((bsz, WIDTH_A, n), BF16),
        jax.ShapeDtypeStruct((bsz, n, KV_A), BF16),
        jax.ShapeDtypeStruct((bsz, n // chunk, va_rows, chunk), BF16),
        jax.ShapeDtypeStruct((bsz, WIDTH_B, n), BF16),
        jax.ShapeDtypeStruct((bsz, n, WIDTH_B), BF16),
        jax.ShapeDtypeStruct((bsz, n // chunk, vb_rows, chunk), BF16),
        jax.ShapeDtypeStruct((bsz, n // tm, NORM_ROWS, LANE), F32),
    )
    return pl.pallas_call(
        functools.partial(_inproj_kernel, chunk=chunk),
        grid=grid,
        in_specs=[
            pl.BlockSpec((1, tm, d), lambda b, i: (b, i, 0)),
            full((1, d)),
            full((rows_t, d)),
            full((d, WIDTH_B)),
            pl.BlockSpec((HEAD_DIM, tm), lambda b, i: (0, i)),
            pl.BlockSpec((HEAD_DIM, tm), lambda b, i: (0, i)),
            full((HEAD_DIM, 1)),
            full((HEAD_DIM, 1)),
        ],
        out_specs=(
            pl.BlockSpec((1, WIDTH_A, tm), lambda b, i: (b, 0, i)),
            pl.BlockSpec((1, tm, KV_A), lambda b, i: (b, i, 0)),
            pl.BlockSpec((1, tm // chunk, va_rows, chunk), lambda b, i: (b, i, 0, 0)),
            pl.BlockSpec((1, WIDTH_B, tm), lambda b, i: (b, 0, i)),
            pl.BlockSpec((1, tm, WIDTH_B), lambda b, i: (b, i, 0)),
            pl.BlockSpec((1, tm // chunk, vb_rows, chunk), lambda b, i: (b, i, 0, 0)),
            pl.BlockSpec((1, 1, NORM_ROWS, LANE), lambda b, i: (b, i, 0, 0)),
        ),
        out_shape=out_shape,
        compiler_params=pltpu.CompilerParams(
            dimension_semantics=("arbitrary", "arbitrary"), vmem_limit_bytes=VMEM_LIMIT),
        name="inproj",
    )(x, g1, wt, wkb, cos_t, sin_t, qn, kn)


def _online_update(s, c_shift, m_ref, acc_ref, vblk, m_idx, acc_idx):
    mc = jnp.max(s, axis=0, keepdims=True)
    if c_shift is not None:
        mc = mc + c_shift
    m_old = m_ref[m_idx]
    m_new = jnp.maximum(m_old, mc)
    alpha = jnp.exp2(m_old - m_new)
    shift = m_new if c_shift is None else m_new - c_shift
    p = jnp.exp2(s - shift).astype(BF16)
    pv = jnp.dot(vblk, p, preferred_element_type=F32)
    acc_ref[acc_idx] = alpha * acc_ref[acc_idx] + pv
    m_ref[m_idx] = m_new


def _init_from_meta(s, m_ref, acc_ref, vmeta, idx):
    m0 = jnp.max(s, axis=0, keepdims=True)
    p = jnp.exp2(s - m0).astype(BF16)
    m_ref[idx] = m0
    acc_ref[idx] = jnp.dot(vmeta, p, preferred_element_type=F32)


def _bounded_update(s, shift, acc_ref, vblk, idx, first=False):
    p = jnp.exp2(s - shift).astype(BF16)
    pv = jnp.dot(vblk, p, preferred_element_type=F32)
    acc_ref[idx] = pv if first else acc_ref[idx] + pv


def _part(p, tq):
    c0 = p * PART_LANES
    return c0 // tq, slice(c0 % tq, c0 % tq + PART_LANES), slice(c0, c0 + PART_LANES)


def _query_norm(q_bf16):
    qf = q_bf16.astype(F32)
    return jnp.sqrt(jnp.sum(qf * qf, axis=0, keepdims=True))


def _pipelined_chunks(count, parts, chunk_of, qk_into, consume, sa_ref, sb_ref, unroll=2):
    assert unroll % 2 == 0 and count >= unroll and count % unroll == 0
    bufs = (sa_ref, sb_ref)

    def step(u, nxt, cur):
        for part in range(parts):
            if nxt is not None:
                qk_into(bufs[(u + 1) % 2], nxt, part)
            consume(bufs[u % 2], cur, part)

    for part in range(parts):
        qk_into(sa_ref, chunk_of(0), part)

    def body(j, carry):
        i = unroll * j
        for u in range(unroll):
            step(u, chunk_of(i + u + 1), chunk_of(i + u))
        return carry

    lax.fori_loop(0, count // unroll - 1, body, 0)
    i = count - unroll
    for u in range(unroll):
        step(u, chunk_of(i + u + 1) if u + 1 < unroll else None, chunk_of(i + u))


def _attn_a_kernel(kmax_ref, q_ref, k_ref, v_ref, km_ref, vm_ref, o_ref, qpad_ref, sa_ref, sb_ref, m_ref,
                   acc_ref, *, tq, nk):
    b = pl.program_id(0)
    g = pl.program_id(1)
    kmax = kmax_ref[b, g] * BOUND_MARGIN
    zero = jnp.zeros((HEAD_DIM, tq), BF16)
    for h in range(GQA_GROUP):
        qh = q_ref[0, h * HEAD_DIM:(h + 1) * HEAD_DIM, :]
        qpad_ref[0:HEAD_DIM, h * tq:(h + 1) * tq] = jnp.where(g == 0, qh, zero)
        qpad_ref[HEAD_DIM:2 * HEAD_DIM, h * tq:(h + 1) * tq] = jnp.where(g == 1, qh, zero)
        m_ref[h] = _query_norm(qh) * kmax

    valid = lax.broadcasted_iota(jnp.int32, (META_PAD, 1), 0) < N_META

    def meta_scores():
        s = jnp.dot(km_ref[...], qpad_ref[...], preferred_element_type=F32)
        return jnp.where(valid, s, MASK_VALUE)

    parts = GQA_GROUP * tq // PART_LANES

    def qk_into(dst, kc, p):
        cols = _part(p, tq)[2]
        dst[:, cols] = jnp.dot(k_ref[0, kc], qpad_ref[:, cols], preferred_element_type=F32)

    s = meta_scores()
    for h in range(GQA_GROUP):
        _bounded_update(s[:, h * tq:(h + 1) * tq], m_ref[h], acc_ref, vm_ref[...], h, first=True)

    def consume(src, kc, p):
        h, lanes, cols = _part(p, tq)
        _bounded_update(src[:, cols], m_ref[h, :, lanes], acc_ref, v_ref[0, kc], (h, slice(None), lanes))

    _pipelined_chunks(nk, parts, lambda i: i, qk_into, consume, sa_ref, sb_ref, unroll=FAST_UNROLL)

    l_min = jnp.min(acc_ref[0, HEAD_DIM:HEAD_DIM + 1, :])
    for h in range(1, GQA_GROUP):
        l_min = jnp.minimum(l_min, jnp.min(acc_ref[h, HEAD_DIM:HEAD_DIM + 1, :]))

    @pl.when(jnp.logical_not(l_min >= L_MIN))
    def _():
        s = meta_scores()
        for h in range(GQA_GROUP):
            _init_from_meta(s[:, h * tq:(h + 1) * tq], m_ref, acc_ref, vm_ref[...], h)

        def consume_exact(src, kc, p):
            h, lanes, cols = _part(p, tq)
            _online_update(src[:, cols], None, m_ref, acc_ref, v_ref[0, kc],
                           (h, slice(None), lanes), (h, slice(None), lanes))

        _pipelined_chunks(nk, parts, lambda i: i, qk_into, consume_exact, sa_ref, sb_ref)

    outs = []
    for h in range(GQA_GROUP):
        outs.append(acc_ref[h, 0:HEAD_DIM, :] / acc_ref[h, HEAD_DIM:HEAD_DIM + 1, :])
    o = jnp.concatenate(outs, axis=0)
    o_ref[0] = o.T.astype(BF16)


def _attn_a(kmax, qa_t, ka, va_t, km, vm_t, *, tq):
    bsz, _, n = qa_t.shape
    nk = n // KEY_CHUNK
    ka4 = ka.reshape(bsz, nk, KEY_CHUNK, KV_A)
    gw = GQA_GROUP * HEAD_DIM
    return pl.pallas_call(
        functools.partial(_attn_a_kernel, tq=tq, nk=nk),
        grid=(bsz, N_KV_A, n // tq),
        in_specs=[
            pl.BlockSpec(memory_space=pltpu.SMEM),
            pl.BlockSpec((1, gw, tq), lambda b, g, i: (b, g, i)),
            pl.BlockSpec((1, nk, KEY_CHUNK, KV_A), lambda b, g, i: (b, 0, 0, 0)),
            pl.BlockSpec((1, nk, VA_ROWS, KEY_CHUNK), lambda b, g, i: (b, 0, g, 0)),
            pl.BlockSpec((META_PAD, KV_A), lambda b, g, i: (0, 0)),
            pl.BlockSpec((VA_ROWS, META_PAD), lambda b, g, i: (g, 0)),
        ],
        out_specs=pl.BlockSpec((1, tq, gw), lambda b, g, i: (b, i, g)),
        out_shape=jax.ShapeDtypeStruct((bsz, n, WIDTH_A), BF16),
        scratch_shapes=[
            pltpu.VMEM((KV_A, GQA_GROUP * tq), BF16),
            pltpu.VMEM((KEY_CHUNK, GQA_GROUP * tq), F32),
            pltpu.VMEM((KEY_CHUNK, GQA_GROUP * tq), F32),
            pltpu.VMEM((GQA_GROUP, 1, tq), F32),
            pltpu.VMEM((GQA_GROUP, VA_ROWS, tq), F32),
        ],
        compiler_params=pltpu.CompilerParams(
            dimension_semantics=("arbitrary", "arbitrary", "arbitrary"), vmem_limit_bytes=VMEM_LIMIT),
        name="attn_a",
    )(kmax, qa_t, ka4, va_t, km, vm_t)


def _attn_b_kernel(cfar_ref, kmax_ref, q_ref, k_ref, v_ref, km_ref, vm_ref, band_ref, mbias_ref,
                   lq1_ref, lk1_ref, lq2_ref, lk2_ref, sn_ref, o_ref,
                   qblk_ref, sa_ref, sb_ref, m_ref, acc_ref, *, tq, nk):
    b = pl.program_id(0)
    h = pl.program_id(1)
    qi = pl.program_id(2)
    vd = 2 * HEAD_DIM
    zero = jnp.zeros((HEAD_DIM, tq), BF16)
    q1 = q_ref[0, 0:HEAD_DIM, :]
    q2 = q_ref[0, HEAD_DIM:vd, :]
    qblk_ref[0:HEAD_DIM, 0:tq] = q1
    qblk_ref[0:HEAD_DIM, tq:2 * tq] = zero
    qblk_ref[HEAD_DIM:vd, 0:tq] = zero
    qblk_ref[HEAD_DIM:vd, tq:2 * tq] = q2
    c_neg = cfar_ref[h, 0]
    c_pos = cfar_ref[h, 1]
    b_max = cfar_ref[h, 2]
    for j, qj in enumerate((q1, q2)):
        m_ref[j] = _query_norm(qj) * (kmax_ref[b, 2 * h + j] * BOUND_MARGIN) + b_max

    def meta_scores():
        return jnp.dot(km_ref[...], qblk_ref[...], preferred_element_type=F32)

    parts = 2 * tq // PART_LANES

    def qk_into(dst, info, p):
        cols = _part(p, tq)[2]
        dst[:, cols] = jnp.dot(k_ref[0, info[0]], qblk_ref[:, cols], preferred_element_type=F32)

    near_lo = jnp.clip(qi * TILE_CHUNKS - 1, 0, nk - NEAR_CHUNKS)

    def far_chunk(i):
        right = i >= near_lo
        return jnp.where(right, i + NEAR_CHUNKS, i), jnp.where(right, c_pos, c_neg)

    def near_chunk(i):
        kc = near_lo + i
        return kc, pl.multiple_of((kc - qi * TILE_CHUNKS + BAND_SHIFT) * KEY_CHUNK, KEY_CHUNK)

    s = meta_scores()
    for j in range(2):
        _bounded_update(s[:, j * tq:(j + 1) * tq] + mbias_ref[0], m_ref[j], acc_ref, vm_ref[...], j, first=True)

    def consume_far(src, info, p):
        kc, c = info
        j, lanes, cols = _part(p, tq)
        _bounded_update(src[:, cols], m_ref[j, :, lanes] - c, acc_ref, v_ref[0, kc],
                        (j, slice(None), lanes))

    def consume_near(src, info, p):
        kc, r0 = info
        j, lanes, cols = _part(p, tq)
        bias = band_ref[0, pl.ds(r0, KEY_CHUNK), lanes]
        _bounded_update(src[:, cols] + bias, m_ref[j, :, lanes], acc_ref, v_ref[0, kc],
                        (j, slice(None), lanes))

    _pipelined_chunks(nk - NEAR_CHUNKS, parts, far_chunk, qk_into, consume_far, sa_ref, sb_ref,
                      unroll=FAST_UNROLL)
    _pipelined_chunks(NEAR_CHUNKS, parts, near_chunk, qk_into, consume_near, sa_ref, sb_ref,
                      unroll=FAST_UNROLL)

    l_min = jnp.minimum(jnp.min(acc_ref[0, vd:vd + 1, :]), jnp.min(acc_ref[1, vd:vd + 1, :]))

    @pl.when(jnp.logical_not(l_min >= L_MIN))
    def _():
        s = meta_scores()
        for j in range(2):
            _init_from_meta(s[:, j * tq:(j + 1) * tq] + mbias_ref[0], m_ref, acc_ref, vm_ref[...], j)

        def exact_far(src, info, p):
            kc, c = info
            j, lanes, cols = _part(p, tq)
            _online_update(src[:, cols], c, m_ref, acc_ref, v_ref[0, kc],
                           (j, slice(None), lanes), (j, slice(None), lanes))

        def exact_near(src, info, p):
            kc, r0 = info
            j, lanes, cols = _part(p, tq)
            bias = band_ref[0, pl.ds(r0, KEY_CHUNK), lanes]
            _online_update(src[:, cols] + bias, None, m_ref, acc_ref, v_ref[0, kc],
                           (j, slice(None), lanes), (j, slice(None), lanes))

        _pipelined_chunks(nk - NEAR_CHUNKS, parts, far_chunk, qk_into, exact_far, sa_ref, sb_ref)
        _pipelined_chunks(NEAR_CHUNKS, parts, near_chunk, qk_into, exact_near, sa_ref, sb_ref)

    lam =(jnp.exp(jnp.sum(lq1_ref[...] * lk1_ref[...], axis=-1, keepdims=True))
           - jnp.exp(jnp.sum(lq2_ref[...] * lk2_ref[...], axis=-1, keepdims=True)) + LAM_INIT)
    o = (acc_ref[0, 0:vd, :] / acc_ref[0, vd:vd + 1, :]
         - lam * (acc_ref[1, 0:vd, :] / acc_ref[1, vd:vd + 1, :]))
    ms = jnp.mean(o * o, axis=0, keepdims=True)
    o = o * lax.rsqrt(ms + EPS) * sn_ref[...] * (1.0 - LAM_INIT)
    o_ref[0] = o.T.astype(BF16)


def _attn_b(cfar, kmax, qb_t, kb, vb_t, kmb, vmb_t, band, mbias, lq1, lk1, lq2, lk2, sn, *, tq):
    bsz, _, n = qb_t.shape
    nk = n // KEY_CHUNK
    assert tq == Q_TILE and NEAR_CHUNKS % 2 == 0 and (nk - NEAR_CHUNKS) % 2 == 0 and nk - NEAR_CHUNKS >= 2
    vd = 2 * HEAD_DIM
    kb4 = kb.reshape(bsz, nk, KEY_CHUNK, WIDTH_B)
    vec = lambda: pl.BlockSpec((1, HEAD_DIM), lambda b, h, i: (0, 0))
    return pl.pallas_call(
        functools.partial(_attn_b_kernel, tq=tq, nk=nk),
        grid=(bsz, N_HEADS_B, n // tq),
        in_specs=[
            pl.BlockSpec(memory_space=pltpu.SMEM),
            pl.BlockSpec(memory_space=pltpu.SMEM),
            pl.BlockSpec((1, vd, tq), lambda b, h, i: (b, h, i)),
            pl.BlockSpec((1, nk, KEY_CHUNK, vd), lambda b, h, i: (b, 0, 0, h)),
            pl.BlockSpec((1, nk, VB_ROWS, KEY_CHUNK), lambda b, h, i: (b, 0, h, 0)),
            pl.BlockSpec((META_PAD, vd), lambda b, h, i: (0, h)),
            pl.BlockSpec((VB_ROWS, META_PAD), lambda b, h, i: (h, 0)),
            pl.BlockSpec((1, BAND_ROWS, tq), lambda b, h, i: (h, 0, 0)),
            pl.BlockSpec((1, META_PAD, tq), lambda b, h, i: (h, 0, i)),
            vec(), vec(), vec(), vec(),
            pl.BlockSpec((vd, 1), lambda b, h, i: (0, 0)),
        ],
        out_specs=pl.BlockSpec((1, tq, vd), lambda b, h, i: (b, i, h)),
        out_shape=jax.ShapeDtypeStruct((bsz, n, WIDTH_B), BF16),
        scratch_shapes=[
            pltpu.VMEM((vd, 2 * tq), BF16),
            pltpu.VMEM((KEY_CHUNK, 2 * tq), F32),
            pltpu.VMEM((KEY_CHUNK, 2 * tq), F32),
            pltpu.VMEM((2, 1, tq), F32),
            pltpu.VMEM((2, VB_ROWS, tq), F32),
        ],
        compiler_params=pltpu.CompilerParams(
            dimension_semantics=("arbitrary", "arbitrary", "arbitrary"), vmem_limit_bytes=VMEM_LIMIT),
        name="attn_b",
    )(cfar, kmax, qb_t, kb4, vb_t, kmb, vmb_t, band, mbias, lq1, lk1, lq2, lk2, sn)


def _outproj_kernel(oa_ref, ob_ref, x_ref, wa_ref, wb_ref, g2_ref, wrh_ref, wrl_ref, br_ref,
                    h_ref, n2_ref, eid_ref, gate_ref):
    att = (jnp.dot(oa_ref[0], wa_ref[...], preferred_element_type=F32)
           + jnp.dot(ob_ref[0], wb_ref[...], preferred_element_type=F32))
    hres = x_ref[0] + att
    h_ref[0] = hres
    ms = jnp.mean(hres * hres, axis=-1, keepdims=True)
    n2 = hres * lax.rsqrt(ms + EPS) * g2_ref[...]
    n2_hi = n2.astype(BF16)
    n2_lo = (n2 - n2_hi.astype(F32)).astype(BF16)
    n2_ref[0] = n2_hi
    wrh = wrh_ref[...]
    logits = _nt_dot(wrh, n2_hi) + _nt_dot(wrh, n2_lo) + _nt_dot(wrl_ref[...], n2_hi) + br_ref[...]
    tm = hres.shape[0]
    gl = logits[N_EXPERTS:N_EXPERTS + N_GROUPS]
    gmax = jnp.max(gl, axis=0, keepdims=True)
    giota = lax.broadcasted_iota(jnp.int32, (N_GROUPS, tm), 0)
    g_idx = jnp.min(jnp.where(gl == gmax, giota, N_GROUPS), axis=0, keepdims=True)
    g_w = 1.0 / jnp.sum(jnp.exp(gl - gmax), axis=0, keepdims=True)
    sel = jnp.zeros((EXPERTS_PER_GROUP, tm), F32)
    for g in range(N_GROUPS):
        sel = jnp.where(g_idx == g, logits[g * EXPERTS_PER_GROUP:(g + 1) * EXPERTS_PER_GROUP], sel)
    eiota = lax.broadcasted_iota(jnp.int32, (EXPERTS_PER_GROUP, tm), 0)
    e1 = jnp.max(sel, axis=0, keepdims=True)
    i1 = jnp.min(jnp.where(sel == e1, eiota, EXPERTS_PER_GROUP), axis=0, keepdims=True)
    sel2 = jnp.where(eiota == i1, -jnp.inf, sel)
    e2 = jnp.max(sel2, axis=0, keepdims=True)
    i2 = jnp.min(jnp.where(sel2 == e2, eiota, EXPERTS_PER_GROUP), axis=0, keepdims=True)
    r = jnp.exp(e2 - e1)
    w1 = g_w / (1.0 + r)
    eid_ref[0] = jnp.concatenate([g_idx * EXPERTS_PER_GROUP + i1, g_idx * EXPERTS_PER_GROUP + i2], axis=0)
    gate_ref[0] = jnp.concatenate([w1, w1 * r], axis=0)


def _outproj(oa, ob, x, wa, wb, g2, wrh, wrl, br, *, tm):
    bsz, n, d = x.shape
    full = lambda shape: pl.BlockSpec(shape, lambda b, i: (0,) * len(shape))
    tok = lambda w: pl.BlockSpec((1, tm, w), lambda b, i: (b, i, 0))
    lane = lambda: pl.BlockSpec((1, TOP_K, tm), lambda b, i: (b, 0, i))
    return pl.pallas_call(
        _outproj_kernel,
        grid=(bsz, n // tm),
        in_specs=[tok(WIDTH_A), tok(WIDTH_B), tok(d), full((WIDTH_A, d)), full((WIDTH_B, d)), full((1, d)),
                  full((ROUTER_ROWS, d)), full((ROUTER_ROWS, d)), full((ROUTER_ROWS, 1))],
        out_specs=(tok(d), tok(d), lane(), lane()),
        out_shape=(
            jax.ShapeDtypeStruct((bsz, n, d), F32),
            jax.ShapeDtypeStruct((bsz, n, d), BF16),
            jax.ShapeDtypeStruct((bsz, TOP_K, n), jnp.int32),
            jax.ShapeDtypeStruct((bsz, TOP_K, n), F32),
        ),
        compiler_params=pltpu.CompilerParams(
            dimension_semantics=("arbitrary", "arbitrary"), vmem_limit_bytes=VMEM_LIMIT),
        name="outproj_router",
    )(oa, ob, x, wa, wb, g2, wrh, wrl, br)


def _expert_kernel(be_ref, nb_ref, x_ref, wg_ref, wu_ref, wd_ref, y_ref):
    i = pl.program_id(0)

    @pl.when(i < nb_ref[0])
    def _():
        x = x_ref[...]
        gt = jnp.dot(x, wg_ref[0], preferred_element_type=F32)
        up = jnp.dot(x, wu_ref[0], preferred_element_type=F32)
        act = gt * (1.0 / (1.0 + jnp.exp(-gt))) * up
        y_ref[...] = jnp.dot(act.astype(BF16), wd_ref[0], preferred_element_type=F32).astype(BF16)

    @pl.when(i >= nb_ref[0])
    def _():
        y_ref[...] = jnp.zeros_like(y_ref)


def _experts(block_e, n_used, xs, wg, wu, wd, *, rows):
    cap, d = xs.shape
    grid_spec = pltpu.PrefetchScalarGridSpec(
        num_scalar_prefetch=2,
        grid=(cap // rows,),
        in_specs=[
            pl.BlockSpec((rows, d), lambda i, be, nb: (i, 0)),
            pl.BlockSpec((1, d, D_EXPERT), lambda i, be, nb: (be[i], 0, 0)),
            pl.BlockSpec((1, d, D_EXPERT), lambda i, be, nb: (be[i], 0, 0)),
            pl.BlockSpec((1, D_EXPERT, d), lambda i, be, nb: (be[i], 0, 0)),
        ],
        out_specs=pl.BlockSpec((rows, d), lambda i, be, nb: (i, 0)),
    )
    return pl.pallas_call(
        _expert_kernel,
        grid_spec=grid_spec,
        out_shape=jax.ShapeDtypeStruct((cap, d), BF16),
        compiler_params=pltpu.CompilerParams(
            dimension_semantics=("arbitrary",), vmem_limit_bytes=VMEM_LIMIT),
        name="experts",
    )(block_e, n_used, xs, wg, wu, wd)


def _final_kernel(h_ref, y0_ref, y1_ref, gate_ref, gf_ref, o_ref):
    gates = gate_ref[...]
    hres = (h_ref[...] + gates[:, 0:1] * y0_ref[...].astype(F32)
            + gates[:, 1:2] * y1_ref[...].astype(F32))
    ms = jnp.mean(hres * hres, axis=-1, keepdims=True)
    o_ref[...] = hres * lax.rsqrt(ms + EPS) * gf_ref[...]


def _final(h, y0, y1, gates, gf, *, tm):
    t, d = h.shape
    tok = lambda w: pl.BlockSpec((tm, w), lambda i: (i, 0))
    return pl.pallas_call(
        _final_kernel,
        grid=(t // tm,),
        in_specs=[tok(d), tok(d), tok(d), tok(TOP_K), pl.BlockSpec((1, d), lambda i: (0, 0))],
        out_specs=tok(d),
        out_shape=jax.ShapeDtypeStruct((t, d), F32),
        compiler_params=pltpu.CompilerParams(
            dimension_semantics=("arbitrary",), vmem_limit_bytes=VMEM_LIMIT),
        name="combine_final_norm",
    )(h, y0, y1, gates, gf)


def _rope_tables(n_tok):
    rows = n_tok // GRID_W
    row = jnp.repeat(jnp.arange(rows, dtype=F32), GRID_W)
    col = jnp.tile(jnp.arange(GRID_W, dtype=F32), rows)
    inv = ROPE_THETA ** (-jnp.arange(0, AXIS_DIM, 2, dtype=F32) / AXIS_DIM)
    ar = row[:, None] * inv
    ac = col[:, None] * inv
    ang = jnp.concatenate([ar, ar, ac, ac], axis=-1)
    q = HEAD_DIM // 4
    sign = jnp.concatenate([-jnp.ones((q,), F32), jnp.ones((q,), F32)] * 2)
    return jnp.cos(ang).T, (jnp.sin(ang) * sign).T


def _t5_bucket(rp):
    nb = N_BUCKETS // 2
    max_exact = nb // 2
    ret = (rp > 0).astype(jnp.int32) * nb
    n = jnp.abs(rp)
    large = max_exact + (jnp.log(jnp.maximum(n, 1).astype(F32) / max_exact)
                         / math.log(MAX_DISTANCE / max_exact) * (nb - max_exact)).astype(jnp.int32)
    large = jnp.minimum(large, nb - 1)
    return ret + jnp.where(n < max_exact, n, large)


def _bias_tables(rel_bias, n_tok, tq):
    rb = rel_bias.astype(F32) * LOG2E
    lookup = lambda d: rb[_t5_bucket(d)].T
    shift = BAND_SHIFT * KEY_CHUNK
    n_wrap = BAND_ROWS + tq
    u = jnp.concatenate([lookup(jnp.arange(BAND_ROWS, dtype=jnp.int32) - shift),
                         jnp.broadcast_to(lookup(jnp.array([-shift], jnp.int32)), (N_HEADS_B, tq))], axis=1)
    skew = jnp.tile(u, (1, tq))[:, :tq * (n_wrap - 1)].reshape(N_HEADS_B, tq, n_wrap - 1)
    band = jnp.transpose(skew[:, :, :BAND_ROWS], (0, 2, 1))
    far = jnp.array([-2 * MAX_DISTANCE, 2 * MAX_DISTANCE], jnp.int32)
    cfar = jnp.concatenate([lookup(far), jnp.max(rb, axis=0)[:, None]], axis=1)
    v = lookup(-jnp.arange(n_tok + N_META + 1, dtype=jnp.int32))
    mb = jnp.stack([v[:, N_META - j:N_META - j + n_tok] for j in range(N_META)], axis=1)
    pad = jnp.full((N_HEADS_B, META_PAD - N_META, n_tok), MASK_VALUE, F32)
    return band, cfar, jnp.concatenate([mb, pad], axis=1)


def _dispatch_plan(eid, rows):
    t = eid.shape[1]
    n_assign = TOP_K * t
    flat = eid.reshape(-1)
    a_idx = jnp.arange(n_assign, dtype=jnp.int32)
    skey = jnp.sort(flat * n_assign + a_idx)
    s_assign = skey % n_assign
    onehot = (flat[:, None] == jnp.arange(N_EXPERTS, dtype=jnp.int32)[None, :]).astype(jnp.int32)
    csum = jnp.cumsum(onehot, axis=0)
    counts = csum[-1]
    rank = jnp.take_along_axis(csum, flat[:, None], axis=1)[:, 0] - 1
    padded = (counts + rows - 1) // rows * rows
    pend = jnp.cumsum(padded)
    poff = pend - padded
    off = jnp.cumsum(counts) - counts
    pos = poff[flat] + rank
    n_blocks = (n_assign + N_EXPERTS * (rows - 1) + rows - 1) // rows
    cap = n_blocks * rows
    block_start = jnp.arange(n_blocks, dtype=jnp.int32) * rows
    block_e = jnp.minimum(jnp.searchsorted(pend, block_start, side='right'), N_EXPERTS - 1).astype(jnp.int32)
    d = jnp.arange(cap, dtype=jnp.int32)
    e_d = jnp.repeat(block_e, rows)
    j = d - poff[e_d]
    valid = j < counts[e_d]
    src = jnp.clip(off[e_d] + j, 0, n_assign - 1)
    row_tok = jnp.where(valid, s_assign[src] % t, 0)
    n_used = (pend[-1] // rows).astype(jnp.int32).reshape(1)
    return row_tok, block_e, n_used, pos.reshape(TOP_K, t)


def _key_bounds(knorm, knorm_meta):
    both = jnp.maximum(jnp.max(knorm, axis=1), knorm_meta[0, 0][None])
    return jnp.sqrt(both[:, 1:1 + N_KV_A, 0]), jnp.sqrt(both[:, 0, 0:2 * N_HEADS_B])


def _mixer(x, shared):
    bsz, n_tok, d = x.shape
    assert n_tok % Q_TILE == 0 and n_tok % TOKEN_TILE == 0
    tq = Q_TILE
    tm = TOKEN_TILE
    cos_t, sin_t = _rope_tables(n_tok)
    qa_t, ka, va_t, qb_t, kb, vb_t, knorm = _inproj(
        x, shared['g1'], shared['wt'], shared['wkb'], cos_t, sin_t, shared['qn'], shared['kn'],
        tm=tm, chunk=KEY_CHUNK)
    kmax_a, kmax_b = _key_bounds(knorm, shared['knorm_meta'])
    oa = _attn_a(kmax_a, qa_t, ka, va_t, shared['km_a'], shared['vm_a'], tq=tq)
    band, cfar, mbias = _bias_tables(shared['rel_bias'], n_tok, tq)
    ob = _attn_b(cfar, kmax_b, qb_t, kb, vb_t, shared['km_b'], shared['vm_b'], band, mbias,
                 shared['lq1'], shared['lk1'], shared['lq2'], shared['lk2'], shared['sn'], tq=tq)
    h, n2, eid, gates = _outproj(oa, ob, x, shared['wa'], shared['wb'], shared['g2'],
                                 shared['wrh'], shared['wrl'], shared['br'], tm=tm)
    t = bsz * n_tok
    eid = jnp.transpose(eid, (1, 0, 2)).reshape(TOP_K, t)
    gates = jnp.transpose(gates, (1, 0, 2)).reshape(TOP_K, t)
    row_tok, block_e, n_used, pos = _dispatch_plan(eid, MOE_ROWS)
    xs = jnp.take(n2.reshape(t, d), row_tok, axis=0)
    return h.reshape(t, d), xs, block_e, n_used, pos, gates.T


def _moe_and_norm(mixed, shape, shared):
    h, xs, block_e, n_used, pos, gates = mixed
    y = _experts(block_e, n_used, xs, shared['wg'], shared['wu'], shared['wd'], rows=MOE_ROWS)
    y0 = jnp.take(y, pos[0], axis=0)
    y1 = jnp.take(y, pos[1], axis=0)
    return _final(h, y0, y1, gates, shared['gf'], tm=TOKEN_TILE).reshape(shape)


def kernel(x_prompt, x_sample, meta_tokens, rel_bias, norm1, w_in, q_norm, k_norm, lambda_q1, lambda_k1,
           lambda_q2, lambda_k2, sub_norm, w_out, norm2, w_router_group, b_router_group, w_router_expert,
           b_router_expert, w_gate, w_up, w_down, final_norm):
    i = LAYER
    d = D_MODEL
    o0 = WIDTH_A
    o1 = o0 + KV_A
    o2 = o1 + KV_A
    o3 = o2 + WIDTH_B
    o4 = o3 + WIDTH_B
    w = w_in[i]
    wt = jnp.concatenate([w[:, :o2], w[:, o2:o3], w[:, o4:]], axis=1).T.astype(BF16)
    wr = jnp.concatenate([w_router_expert[i], w_router_group[i],
                          jnp.zeros((d, ROUTER_ROWS - N_EXPERTS - N_GROUPS), F32)], axis=1).T
    wrh = wr.astype(BF16)
    br = jnp.concatenate([b_router_expert[i], b_router_group[i],
                          jnp.zeros((ROUTER_ROWS - N_EXPERTS - N_GROUPS,), F32)]).reshape(ROUTER_ROWS, 1)
    shared = {
        'g1': norm1[i].reshape(1, d), 'wt': wt, 'wkb': w[:, o3:o4].astype(BF16),
        'qn': q_norm[i].reshape(HEAD_DIM, 1), 'kn': k_norm[i].reshape(HEAD_DIM, 1),
        'rel_bias': rel_bias,
        'lq1': lambda_q1[i].reshape(1, HEAD_DIM), 'lk1': lambda_k1[i].reshape(1, HEAD_DIM),
        'lq2': lambda_q2[i].reshape(1, HEAD_DIM), 'lk2': lambda_k2[i].reshape(1, HEAD_DIM),
        'sn': sub_norm[i].reshape(2 * HEAD_DIM, 1),
        'wa': w_out[i][:WIDTH_A].astype(BF16), 'wb': w_out[i][WIDTH_A:].astype(BF16),
        'g2': norm2[i].reshape(1, d), 'wrh': wrh, 'wrl': (wr - wrh.astype(F32)).astype(BF16), 'br': br,
        'wg': w_gate[i].astype(BF16), 'wu': w_up[i].astype(BF16), 'wd': w_down[i].astype(BF16),
        'gf': final_norm.reshape(1, d),
    }
    xm = jnp.concatenate([meta_tokens.astype(F32), jnp.zeros((META_PAD - N_META, d), F32)], axis=0)[None]
    ones = jnp.ones((HEAD_DIM, META_PAD), F32)
    _, km_a, vm_a, _, km_b, vm_b, knorm_meta = _inproj(
        xm, shared['g1'], shared['wt'], shared['wkb'], ones, 0.0 * ones, shared['qn'], shared['kn'],
        tm=META_PAD, chunk=META_PAD)
    shared['km_a'] = km_a[0]
    shared['vm_a'] = vm_a[0, 0]
    shared['km_b'] = km_b[0]
    shared['vm_b'] = vm_b[0, 0]
    shared['knorm_meta'] = knorm_meta
    mixed_prompt = _mixer(x_prompt, shared)
    mixed_sample = _mixer(x_sample, shared)
    return (_moe_and_norm(mixed_prompt, x_prompt.shape, shared),
            _moe_and_norm(mixed_sample, x_sample.shape, shared))
```

```python
import functools
import math

import jax
import jax.numpy as jnp
from jax import lax
from jax.experimental import pallas as pl
from jax.experimental.pallas import tpu as pltpu

D_MODEL = 1024
HEAD_DIM = 64
N_HEADS_A = 8
N_KV_A = 2
GQA_GROUP = N_HEADS_A // N_KV_A
N_HEADS_B = 4
AXIS_DIM = HEAD_DIM // 2
ROPE_THETA = 10000.0
GRID_W = 64
N_META = 16
N_BUCKETS = 32
MAX_DISTANCE = 128
WIDTH_A = N_HEADS_A * HEAD_DIM
WIDTH_B = N_HEADS_B * 2 * HEAD_DIM
KV_A = N_KV_A * HEAD_DIM
N_GROUPS = 4
EXPERTS_PER_GROUP = 8
N_EXPERTS = N_GROUPS * EXPERTS_PER_GROUP
TOP_K = 2
D_EXPERT = D_MODEL // 2
EPS = 1e-6
LAYER = 0
LAM_INIT = 0.8 - 0.6 * math.exp(-0.3 * LAYER)

LOG2E = 1.4426950408889634
Q_SCALE = (HEAD_DIM ** -0.5) * LOG2E
MASK_VALUE = -1e30

LANE = 128
BF16_ROWS = 16
META_PAD = LANE
KEY_CHUNK = 256
TOKEN_TILE = 512
Q_TILE = 512
MOE_ROWS = 512
ROUTER_ROWS = 40
VMEM_LIMIT = 48 * 1024 * 1024

VA_ROWS = HEAD_DIM + BF16_ROWS
VB_ROWS = 2 * HEAD_DIM + BF16_ROWS
TILE_CHUNKS = Q_TILE // KEY_CHUNK
NEAR_CHUNKS = TILE_CHUNKS + 2
BAND_SHIFT = 2
BAND_ROWS = (TILE_CHUNKS + 5) * KEY_CHUNK

BOUND_MARGIN = 1.01
L_MIN = 2.0 ** -64
NORM_ROWS = 8
PART_LANES = 512
FAST_UNROLL = 4

F32 = jnp.float32
BF16 = jnp.bfloat16


def _nt_dot(a, b):
    return lax.dot_general(a, b, (((1,), (1,)), ((), ())), preferred_element_type=F32)


def _inproj_kernel(x_ref, g1_ref, wt_ref, wkb_ref, cos_ref, sin_ref, qn_ref, kn_ref,
                   qa_ref, ka_ref, va_ref, qb_ref, kb_ref, vb_ref, knorm_ref, *, chunk):
    x = x_ref[0]
    ms = jnp.mean(x * x, axis=-1, keepdims=True)
    n = (x * lax.rsqrt(ms + EPS) * g1_ref[...]).astype(BF16)
    pt = _nt_dot(wt_ref[...], n)
    kb = jnp.dot(n, wkb_ref[...], preferred_element_type=F32).astype(BF16)
    kb_ref[0] = kb
    kbf = kb.astype(F32)
    seg = (lax.broadcasted_iota(jnp.int32, (WIDTH_B, LANE), 0) // HEAD_DIM
           == lax.broadcasted_iota(jnp.int32, (WIDTH_B, LANE), 1)).astype(BF16)
    kb_sq = jnp.dot((kbf * kbf).astype(BF16), seg, preferred_element_type=F32)
    kb_max = jnp.max(kb_sq, axis=0, keepdims=True)
    cos = cos_ref[...]
    sin = sin_ref[...]
    q = HEAD_DIM // 4

    def norm_rope(blk, gain):
        ss = jnp.mean(blk * blk, axis=0, keepdims=True)
        y = blk * lax.rsqrt(ss + EPS) * gain
        swapped = jnp.concatenate([y[q:2 * q], y[0:q], y[3 * q:4 * q], y[2 * q:3 * q]], axis=0)
        return y * cos + swapped * sin

    qn = qn_ref[...]
    kn = kn_ref[...]
    for h in range(N_HEADS_A):
        blk = pt[h * HEAD_DIM:(h + 1) * HEAD_DIM]
        qa_ref[0, h * HEAD_DIM:(h + 1) * HEAD_DIM, :] = (norm_rope(blk, qn) * Q_SCALE).astype(BF16)
    o0 = WIDTH_A
    ka = jnp.concatenate(
        [norm_rope(pt[o0 + j * HEAD_DIM:o0 + (j + 1) * HEAD_DIM], kn) for j in range(N_KV_A)], axis=0)
    ka_ref[0] = ka.T.astype(BF16)
    kaf = ka.astype(BF16).astype(F32)
    ka_sq = kaf * kaf
    ka_max = [jnp.max(jnp.sum(ka_sq[g * HEAD_DIM:(g + 1) * HEAD_DIM], axis=0, keepdims=True),
                      axis=1, keepdims=True) for g in range(N_KV_A)]
    knorm_ref[0, 0] = jnp.concatenate(
        [kb_max] + [jnp.broadcast_to(v, (1, LANE)) for v in ka_max]
        + [jnp.zeros((NORM_ROWS - 1 - N_KV_A, LANE), F32)], axis=0)
    o1 = o0 + KV_A
    o2 = o1 + KV_A
    o3 = o2 + WIDTH_B
    tm = x.shape[0]
    vd = 2 * HEAD_DIM
    ones_rows = (lax.broadcasted_iota(jnp.int32, (BF16_ROWS, chunk), 0) == 0).astype(BF16)
    for j in range(tm // chunk):
        cols = slice(j * chunk, (j + 1) * chunk)
        for g in range(N_KV_A):
            va_ref[0, j, g * VA_ROWS:g * VA_ROWS + HEAD_DIM] = (
                pt[o1 + g * HEAD_DIM:o1 + (g + 1) * HEAD_DIM, cols].astype(BF16))
            va_ref[0, j, g * VA_ROWS + HEAD_DIM:(g + 1) * VA_ROWS] = ones_rows
        for h in range(N_HEADS_B):
            vb_ref[0, j, h * VB_ROWS:h * VB_ROWS + vd] = pt[o3 + h * vd:o3 + (h + 1) * vd, cols].astype(BF16)
            vb_ref[0, j, h * VB_ROWS + vd:(h + 1) * VB_ROWS] = ones_rows
    qb_ref[0] = (pt[o2:o3] * Q_SCALE).astype(BF16)


def _inproj(x, g1, wt, wkb, cos_t, sin_t, qn, kn, *, tm, chunk):
    bsz, n, d = x.shape
    rows_t = wt.shape[0]
    grid = (bsz, n // tm)
    full = lambda shape: pl.BlockSpec(shape, lambda b, i: (0,) * len(shape))
    va_rows = N_KV_A * VA_ROWS
    vb_rows = N_HEADS_B * VB_ROWS
    out_shape = (
        jax.ShapeDtypeStruct((bsz, WIDTH_A, n), BF16),
        jax.ShapeDtypeStruct((bsz, n, KV_A), BF16),
        jax.ShapeDtypeStruct((bsz, n // chunk, va_rows, chunk), BF16),
        jax.ShapeDtypeStruct((bsz, WIDTH_B, n), BF16),
        jax.ShapeDtypeStruct((bsz, n, WIDTH_B), BF16),
        jax.ShapeDtypeStruct((bsz, n // chunk, vb_rows, chunk), BF16),
        jax.ShapeDtypeStruct((bsz, n // tm, NORM_ROWS, LANE), F32),
    )
    return pl.pallas_call(
        functools.partial(_inproj_kernel, chunk=chunk),
        grid=grid,
        in_specs=[
            pl.BlockSpec((1, tm, d), lambda b, i: (b, i, 0)),
            full((1, d)),
            full((rows_t, d)),
            full((d, WIDTH_B)),
            pl.BlockSpec((HEAD_DIM, tm), lambda b, i: (0, i)),
            pl.BlockSpec((HEAD_DIM, tm), lambda b, i: (0, i)),
            full((HEAD_DIM, 1)),
            full((HEAD_DIM, 1)),
        ],
        out_specs=(
            pl.BlockSpec((1, WIDTH_A, tm), lambda b, i: (b, 0, i)),
            pl.BlockSpec((1, tm, KV_A), lambda b, i: (b, i, 0)),
            pl.BlockSpec((1, tm // chunk, va_rows, chunk), lambda b, i: (b, i, 0, 0)),
            pl.BlockSpec((1, WIDTH_B, tm), lambda b, i: (b, 0, i)),
            pl.BlockSpec((1, tm, WIDTH_B), lambda b, i: (b, i, 0)),
            pl.BlockSpec((1, tm // chunk, vb_rows, chunk), lambda b, i: (b, i, 0, 0)),
            pl.BlockSpec((1, 1, NORM_ROWS, LANE), lambda b, i: (b, i, 0, 0)),
        ),
        out_shape=out_shape,
        compiler_params=pltpu.CompilerParams(
            dimension_semantics=("arbitrary", "arbitrary"), vmem_limit_bytes=VMEM_LIMIT),
        name="inproj",
    )(x, g1, wt, wkb, cos_t, sin_t, qn, kn)


def _online_update(s, c_shift, m_ref, acc_ref, vblk, m_idx, acc_idx):
    mc = jnp.max(s, axis=0, keepdims=True)
    if c_shift is not None:
        mc = mc + c_shift
    m_old = m_ref[m_idx]
    m_new = jnp.maximum(m_old, mc)
    alpha = jnp.exp2(m_old - m_new)
    shift = m_new if c_shift is None else m_new - c_shift
    p = jnp.exp2(s - shift).astype(BF16)
    pv = jnp.dot(vblk, p, preferred_element_type=F32)
    acc_ref[acc_idx] = alpha * acc_ref[acc_idx] + pv
    m_ref[m_idx] = m_new


def _init_from_meta(s, m_ref, acc_ref, vmeta, idx):
    m0 = jnp.max(s, axis=0, keepdims=True)
    p = jnp.exp2(s - m0).astype(BF16)
    m_ref[idx] = m0
    acc_ref[idx] = jnp.dot(vmeta, p, preferred_element_type=F32)


def _bounded_update(s, shift, acc_ref, vblk, idx, first=False):
    p = jnp.exp2(s - shift).astype(BF16)
    pv = jnp.dot(vblk, p, preferred_element_type=F32)
    acc_ref[idx] = pv if first else acc_ref[idx] + pv


def _part(p, tq):
    c0 = p * PART_LANES
    return c0 // tq, slice(c0 % tq, c0 % tq + PART_LANES), slice(c0, c0 + PART_LANES)


def _query_norm(q_bf16):
    qf = q_bf16.astype(F32)
    return jnp.sqrt(jnp.sum(qf * qf, axis=0, keepdims=True))


def _pipelined_chunks(count, parts, chunk_of, qk_into, consume, sa_ref, sb_ref, unroll=2):
    assert unroll % 2 == 0 and count >= unroll and count % unroll == 0
    bufs = (sa_ref, sb_ref)

    def step(u, nxt, cur):
        for part in range(parts):
            if nxt is not None:
                qk_into(bufs[(u + 1) % 2], nxt, part)
            consume(bufs[u % 2], cur, part)

    for part in range(parts):
        qk_into(sa_ref, chunk_of(0), part)

    def body(j, carry):
        i = unroll * j
        for u in range(unroll):
            step(u, chunk_of(i + u + 1), chunk_of(i + u))
        return carry

    lax.fori_loop(0, count // unroll - 1, body, 0)
    i = count - unroll
    for u in range(unroll):
        step(u, chunk_of(i + u + 1) if u + 1 < unroll else None, chunk_of(i + u))


def _attn_a_kernel(kmax_ref, q_ref, k_ref, v_ref, km_ref, vm_ref, o_ref, qpad_ref, sa_ref, sb_ref, m_ref,
                   acc_ref, *, tq, nk):
    b = pl.program_id(0)
    g = pl.program_id(1)
    kmax = kmax_ref[b, g] * BOUND_MARGIN
    zero = jnp.zeros((HEAD_DIM, tq), BF16)
    for h in range(GQA_GROUP):
        qh = q_ref[0, h * HEAD_DIM:(h + 1) * HEAD_DIM, :]
        qpad_ref[0:HEAD_DIM, h * tq:(h + 1) * tq] = jnp.where(g == 0, qh, zero)
        qpad_ref[HEAD_DIM:2 * HEAD_DIM, h * tq:(h + 1) * tq] = jnp.where(g == 1, qh, zero)
        m_ref[h] = _query_norm(qh) * kmax

    valid = lax.broadcasted_iota(jnp.int32, (META_PAD, 1), 0) < N_META

    def meta_scores():
        s = jnp.dot(km_ref[...], qpad_ref[...], preferred_element_type=F32)
        return jnp.where(valid, s, MASK_VALUE)

    parts = GQA_GROUP * tq // PART_LANES

    def qk_into(dst, kc, p):
        cols = _part(p, tq)[2]
        dst[:, cols] = jnp.dot(k_ref[0, kc], qpad_ref[:, cols], preferred_element_type=F32)

    s = meta_scores()
    for h in range(GQA_GROUP):
        _bounded_update(s[:, h * tq:(h + 1) * tq], m_ref[h], acc_ref, vm_ref[...], h, first=True)

    def consume(src, kc, p):
        h, lanes, cols = _part(p, tq)
        _bounded_update(src[:, cols], m_ref[h, :, lanes], acc_ref, v_ref[0, kc], (h, slice(None), lanes))

    _pipelined_chunks(nk, parts, lambda i: i, qk_into, consume, sa_ref, sb_ref, unroll=FAST_UNROLL)

    l_min = jnp.min(acc_ref[0, HEAD_DIM:HEAD_DIM + 1, :])
    for h in range(1, GQA_GROUP):
        l_min = jnp.minimum(l_min, jnp.min(acc_ref[h, HEAD_DIM:HEAD_DIM + 1, :]))

    @pl.when(jnp.logical_not(l_min >= L_MIN))
    def _():
        s = meta_scores()
        for h in range(GQA_GROUP):
            _init_from_meta(s[:, h * tq:(h + 1) * tq], m_ref, acc_ref, vm_ref[...], h)

        def consume_exact(src, kc, p):
            h, lanes, cols = _part(p, tq)
            _online_update(src[:, cols], None, m_ref, acc_ref, v_ref[0, kc],
                           (h, slice(None), lanes), (h, slice(None), lanes))

        _pipelined_chunks(nk, parts, lambda i: i, qk_into, consume_exact, sa_ref, sb_ref)

    outs = []
    for h in range(GQA_GROUP):
        outs.append(acc_ref[h, 0:HEAD_DIM, :] / acc_ref[h, HEAD_DIM:HEAD_DIM + 1, :])
    o = jnp.concatenate(outs, axis=0)
    o_ref[0] = o.T.astype(BF16)


def _attn_a(kmax, qa_t, ka, va_t, km, vm_t, *, tq):
    bsz, _, n = qa_t.shape
    nk = n // KEY_CHUNK
    ka4 = ka.reshape(bsz, nk, KEY_CHUNK, KV_A)
    gw = GQA_GROUP * HEAD_DIM
    return pl.pallas_call(
        functools.partial(_attn_a_kernel, tq=tq, nk=nk),
        grid=(bsz, N_KV_A, n // tq),
        in_specs=[
            pl.BlockSpec(memory_space=pltpu.SMEM),
            pl.BlockSpec((1, gw, tq), lambda b, g, i: (b, g, i)),
            pl.BlockSpec((1, nk, KEY_CHUNK, KV_A), lambda b, g, i: (b, 0, 0, 0)),
            pl.BlockSpec((1, nk, VA_ROWS, KEY_CHUNK), lambda b, g, i: (b, 0, g, 0)),
            pl.BlockSpec((META_PAD, KV_A), lambda b, g, i: (0, 0)),
            pl.BlockSpec((VA_ROWS, META_PAD), lambda b, g, i: (g, 0)),
        ],
        out_specs=pl.BlockSpec((1, tq, gw), lambda b, g, i: (b, i, g)),
        out_shape=jax.ShapeDtypeStruct((bsz, n, WIDTH_A), BF16),
        scratch_shapes=[
            pltpu.VMEM((KV_A, GQA_GROUP * tq), BF16),
            pltpu.VMEM((KEY_CHUNK, GQA_GROUP * tq), F32),
            pltpu.VMEM((KEY_CHUNK, GQA_GROUP * tq), F32),
            pltpu.VMEM((GQA_GROUP, 1, tq), F32),
            pltpu.VMEM((GQA_GROUP, VA_ROWS, tq), F32),
        ],
        compiler_params=pltpu.CompilerParams(
            dimension_semantics=("arbitrary", "arbitrary", "arbitrary"), vmem_limit_bytes=VMEM_LIMIT),
        name="attn_a",
    )(kmax, qa_t, ka4, va_t, km, vm_t)


def _attn_b_kernel(cfar_ref, kmax_ref, q_ref, k_ref, v_ref, km_ref, vm_ref, band_ref, mbias_ref,
                   lq1_ref, lk1_ref, lq2_ref, lk2_ref, sn_ref, o_ref,
                   qblk_ref, sa_ref, sb_ref, m_ref, acc_ref, *, tq, nk):
    b = pl.program_id(0)
    h = pl.program_id(1)
    qi = pl.program_id(2)
    vd = 2 * HEAD_DIM
    zero = jnp.zeros((HEAD_DIM, tq), BF16)
    q1 = q_ref[0, 0:HEAD_DIM, :]
    q2 = q_ref[0, HEAD_DIM:vd, :]
    qblk_ref[0:HEAD_DIM, 0:tq] = q1
    qblk_ref[0:HEAD_DIM, tq:2 * tq] = zero
    qblk_ref[HEAD_DIM:vd, 0:tq] = zero
    qblk_ref[HEAD_DIM:vd, tq:2 * tq] = q2
    c_neg = cfar_ref[h, 0]
    c_pos = cfar_ref[h, 1]
    b_max = cfar_ref[h, 2]
    for j, qj in enumerate((q1, q2)):
        m_ref[j] = _query_norm(qj) * (kmax_ref[b, 2 * h + j] * BOUND_MARGIN) + b_max

    def meta_scores():
        return jnp.dot(km_ref[...], qblk_ref[...], preferred_element_type=F32)

    parts = 2 * tq // PART_LANES

    def qk_into(dst, info, p):
        cols = _part(p, tq)[2]
        dst[:, cols] = jnp.dot(k_ref[0, info[0]], qblk_ref[:, cols], preferred_element_type=F32)

    near_lo = jnp.clip(qi * TILE_CHUNKS - 1, 0, nk - NEAR_CHUNKS)

    def far_chunk(i):
        right = i >= near_lo
        return jnp.where(right, i + NEAR_CHUNKS, i), jnp.where(right, c_pos, c_neg)

    def near_chunk(i):
        kc = near_lo + i
        return kc, pl.multiple_of((kc - qi * TILE_CHUNKS + BAND_SHIFT) * KEY_CHUNK, KEY_CHUNK)

    s = meta_scores()
    for j in range(2):
        _bounded_update(s[:, j * tq:(j + 1) * tq] + mbias_ref[0], m_ref[j], acc_ref, vm_ref[...], j, first=True)

    def consume_far(src, info, p):
        kc, c = info
        j, lanes, cols = _part(p, tq)
        _bounded_update(src[:, cols], m_ref[j, :, lanes] - c, acc_ref, v_ref[0, kc],
                        (j, slice(None), lanes))

    def consume_near(src, info, p):
        kc, r0 = info
        j, lanes, cols = _part(p, tq)
        bias = band_ref[0, pl.ds(r0, KEY_CHUNK), lanes]
        _bounded_update(src[:, cols] + bias, m_ref[j, :, lanes], acc_ref, v_ref[0, kc],
                        (j, slice(None), lanes))

    _pipelined_chunks(nk - NEAR_CHUNKS, parts, far_chunk, qk_into, consume_far, sa_ref, sb_ref,
                      unroll=FAST_UNROLL)
    _pipelined_chunks(NEAR_CHUNKS, parts, near_chunk, qk_into, consume_near, sa_ref, sb_ref,
                      unroll=FAST_UNROLL)

    l_min = jnp.minimum(jnp.min(acc_ref[0, vd:vd + 1, :]), jnp.min(acc_ref[1, vd:vd + 1, :]))

    @pl.when(jnp.logical_not(l_min >= L_MIN))
    def _():
        s = meta_scores()
        for j in range(2):
            _init_from_meta(s[:, j * tq:(j + 1) * tq] + mbias_ref[0], m_ref, acc_ref, vm_ref[...], j)

        def exact_far(src, info, p):
            kc, c = info
            j, lanes, cols = _part(p, tq)
            _online_update(src[:, cols], c, m_ref, acc_ref, v_ref[0, kc],
                           (j, slice(None), lanes), (j, slice(None), lanes))

        def exact_near(src, info, p):
            kc, r0 = info
            j, lanes, cols = _part(p, tq)
            bias = band_ref[0, pl.ds(r0, KEY_CHUNK), lanes]
            _online_update(src[:, cols] + bias, None, m_ref, acc_ref, v_ref[0, kc],
                           (j, slice(None), lanes), (j, slice(None), lanes))

        _pipelined_chunks(nk - NEAR_CHUNKS, parts, far_chunk, qk_into, exact_far, sa_ref, sb_ref)
        _pipelined_chunks(NEAR_CHUNKS, parts, near_chunk, qk_into, exact_near, sa_ref, sb_ref)

    lam =(jnp.exp(jnp.sum(lq1_ref[...] * lk1_ref[...], axis=-1, keepdims=True))
           - jnp.exp(jnp.sum(lq2_ref[...] * lk2_ref[...], axis=-1, keepdims=True)) + LAM_INIT)
    o = (acc_ref[0, 0:vd, :] / acc_ref[0, vd:vd + 1, :]
         - lam * (acc_ref[1, 0:vd, :] / acc_ref[1, vd:vd + 1, :]))
    ms = jnp.mean(o * o, axis=0, keepdims=True)
    o = o * lax.rsqrt(ms + EPS) * sn_ref[...] * (1.0 - LAM_INIT)
    o_ref[0] = o.T.astype(BF16)


def _attn_b(cfar, kmax, qb_t, kb, vb_t, kmb, vmb_t, band, mbias, lq1, lk1, lq2, lk2, sn, *, tq):
    bsz, _, n = qb_t.shape
    nk = n // KEY_CHUNK
    assert tq == Q_TILE and NEAR_CHUNKS % 2 == 0 and (nk - NEAR_CHUNKS) % 2 == 0 and nk - NEAR_CHUNKS >= 2
    vd = 2 * HEAD_DIM
    kb4 = kb.reshape(bsz, nk, KEY_CHUNK, WIDTH_B)
    vec = lambda: pl.BlockSpec((1, HEAD_DIM), lambda b, h, i: (0, 0))
    return pl.pallas_call(
        functools.partial(_attn_b_kernel, tq=tq, nk=nk),
        grid=(bsz, N_HEADS_B, n // tq),
        in_specs=[
            pl.BlockSpec(memory_space=pltpu.SMEM),
            pl.BlockSpec(memory_space=pltpu.SMEM),
            pl.BlockSpec((1, vd, tq), lambda b, h, i: (b, h, i)),
            pl.BlockSpec((1, nk, KEY_CHUNK, vd), lambda b, h, i: (b, 0, 0, h)),
            pl.BlockSpec((1, nk, VB_ROWS, KEY_CHUNK), lambda b, h, i: (b, 0, h, 0)),
            pl.BlockSpec((META_PAD, vd), lambda b, h, i: (0, h)),
            pl.BlockSpec((VB_ROWS, META_PAD), lambda b, h, i: (h, 0)),
            pl.BlockSpec((1, BAND_ROWS, tq), lambda b, h, i: (h, 0, 0)),
            pl.BlockSpec((1, META_PAD, tq), lambda b, h, i: (h, 0, i)),
            vec(), vec(), vec(), vec(),
            pl.BlockSpec((vd, 1), lambda b, h, i: (0, 0)),
        ],
        out_specs=pl.BlockSpec((1, tq, vd), lambda b, h, i: (b, i, h)),
        out_shape=jax.ShapeDtypeStruct((bsz, n, WIDTH_B), BF16),
        scratch_shapes=[
            pltpu.VMEM((vd, 2 * tq), BF16),
            pltpu.VMEM((KEY_CHUNK, 2 * tq), F32),
            pltpu.VMEM((KEY_CHUNK, 2 * tq), F32),
            pltpu.VMEM((2, 1, tq), F32),
            pltpu.VMEM((2, VB_ROWS, tq), F32),
        ],
        compiler_params=pltpu.CompilerParams(
            dimension_semantics=("arbitrary", "arbitrary", "arbitrary"), vmem_limit_bytes=VMEM_LIMIT),
        name="attn_b",
    )(cfar, kmax, qb_t, kb4, vb_t, kmb, vmb_t, band, mbias, lq1, lk1, lq2, lk2, sn)


def _outproj_kernel(oa_ref, ob_ref, x_ref, wa_ref, wb_ref, g2_ref, wrh_ref, wrl_ref, br_ref,
                    h_ref, n2_ref, eid_ref, gate_ref):
    att = (jnp.dot(oa_ref[0], wa_ref[...], preferred_element_type=F32)
           + jnp.dot(ob_ref[0], wb_ref[...], preferred_element_type=F32))
    hres = x_ref[0] + att
    h_ref[0] = hres
    ms = jnp.mean(hres * hres, axis=-1, keepdims=True)
    n2 = hres * lax.rsqrt(ms + EPS) * g2_ref[...]
    n2_hi = n2.astype(BF16)
    n2_lo = (n2 - n2_hi.astype(F32)).astype(BF16)
    n2_ref[0] = n2
    wrh = wrh_ref[...]
    logits = _nt_dot(wrh, n2_hi) + _nt_dot(wrh, n2_lo) + _nt_dot(wrl_ref[...], n2_hi) + br_ref[...]
    tm = hres.shape[0]
    gl = logits[N_EXPERTS:N_EXPERTS + N_GROUPS]
    gmax = jnp.max(gl, axis=0, keepdims=True)
    giota = lax.broadcasted_iota(jnp.int32, (N_GROUPS, tm), 0)
    g_idx = jnp.min(jnp.where(gl == gmax, giota, N_GROUPS), axis=0, keepdims=True)
    g_w = 1.0 / jnp.sum(jnp.exp(gl - gmax), axis=0, keepdims=True)
    sel = jnp.zeros((EXPERTS_PER_GROUP, tm), F32)
    for g in range(N_GROUPS):
        sel = jnp.where(g_idx == g, logits[g * EXPERTS_PER_GROUP:(g + 1) * EXPERTS_PER_GROUP], sel)
    eiota = lax.broadcasted_iota(jnp.int32, (EXPERTS_PER_GROUP, tm), 0)
    e1 = jnp.max(sel, axis=0, keepdims=True)
    i1 = jnp.min(jnp.where(sel == e1, eiota, EXPERTS_PER_GROUP), axis=0, keepdims=True)
    sel2 = jnp.where(eiota == i1, -jnp.inf, sel)
    e2 = jnp.max(sel2, axis=0, keepdims=True)
    i2 = jnp.min(jnp.where(sel2 == e2, eiota, EXPERTS_PER_GROUP), axis=0, keepdims=True)
    r = jnp.exp(e2 - e1)
    w1 = g_w / (1.0 + r)
    eid_ref[0] = jnp.concatenate([g_idx * EXPERTS_PER_GROUP + i1, g_idx * EXPERTS_PER_GROUP + i2], axis=0)
    gate_ref[0] = jnp.concatenate([w1, w1 * r], axis=0)


def _outproj(oa, ob, x, wa, wb, g2, wrh, wrl, br, *, tm):
    bsz, n, d = x.shape
    full = lambda shape: pl.BlockSpec(shape, lambda b, i: (0,) * len(shape))
    tok = lambda w: pl.BlockSpec((1, tm, w), lambda b, i: (b, i, 0))
    lane = lambda: pl.BlockSpec((1, TOP_K, tm), lambda b, i: (b, 0, i))
    return pl.pallas_call(
        _outproj_kernel,
        grid=(bsz, n // tm),
        in_specs=[tok(WIDTH_A), tok(WIDTH_B), tok(d), full((WIDTH_A, d)), full((WIDTH_B, d)), full((1, d)),
                  full((ROUTER_ROWS, d)), full((ROUTER_ROWS, d)), full((ROUTER_ROWS, 1))],
        out_specs=(tok(d), tok(d), lane(), lane()),
        out_shape=(
            jax.ShapeDtypeStruct((bsz, n, d), F32),
            jax.ShapeDtypeStruct((bsz, n, d), F32),
            jax.ShapeDtypeStruct((bsz, TOP_K, n), jnp.int32),
            jax.ShapeDtypeStruct((bsz, TOP_K, n), F32),
        ),
        compiler_params=pltpu.CompilerParams(
            dimension_semantics=("arbitrary", "arbitrary"), vmem_limit_bytes=VMEM_LIMIT),
        name="outproj_router",
    )(oa, ob, x, wa, wb, g2, wrh, wrl, br)


def _expert_kernel(be_ref, cur_ref, nxt_ref, x_hbm, wg_ref, wu_ref, wd_ref, y_ref, xbuf, sems, *, rows):
    i = pl.program_id(0)
    n_steps = pl.num_programs(0)
    slot = lax.rem(i, 2)

    def row_copy(tok, slot_, r):
        return pltpu.make_async_copy(x_hbm.at[pl.ds(tok, 1), :], xbuf.at[slot_, pl.ds(r, 1), :], sems.at[slot_])

    @pl.when(i == 0)
    def _():
        for r in range(rows):
            row_copy(cur_ref[0, 0, r], 0, r).start()

    for r in range(rows):
        row_copy(0, slot, r).wait()
    x = xbuf[slot].astype(BF16)
    gt = jnp.dot(x, wg_ref[0], preferred_element_type=F32)
    up = jnp.dot(x, wu_ref[0], preferred_element_type=F32)
    act = gt * (1.0 / (1.0 + jnp.exp(-gt))) * up
    y = jnp.dot(act.astype(BF16), wd_ref[0], preferred_element_type=F32).astype(BF16)

    for r in range(rows):
        row_copy(nxt_ref[0, 0, r], 1 - slot, r).start()
    y_ref[...] = y

    @pl.when(i == n_steps - 1)
    def _():
        for r in range(rows):
            row_copy(0, 1 - slot, r).wait()


def _experts(block_e, row_tok, x, wg, wu, wd, *, rows):
    d = x.shape[1]
    n_blocks = block_e.shape[0]
    idx = row_tok.reshape(n_blocks, 1, rows)
    smem_rows = lambda shift: pl.BlockSpec(
        (1, 1, rows), lambda i, be: (jnp.minimum(i + shift, n_blocks - 1), 0, 0), memory_space=pltpu.SMEM)
    grid_spec = pltpu.PrefetchScalarGridSpec(
        num_scalar_prefetch=1,
        grid=(n_blocks,),
        in_specs=[
            smem_rows(0),
            smem_rows(1),
            pl.BlockSpec(memory_space=pl.ANY),
            pl.BlockSpec((1, d, D_EXPERT), lambda i, be: (be[i], 0, 0)),
            pl.BlockSpec((1, d, D_EXPERT), lambda i, be: (be[i], 0, 0)),
            pl.BlockSpec((1, D_EXPERT, d), lambda i, be: (be[i], 0, 0)),
        ],
        out_specs=pl.BlockSpec((rows, d), lambda i, be: (i, 0)),
        scratch_shapes=[pltpu.VMEM((2, rows, d), F32), pltpu.SemaphoreType.DMA((2,))],
    )
    return pl.pallas_call(
        functools.partial(_expert_kernel, rows=rows),
        grid_spec=grid_spec,
        out_shape=jax.ShapeDtypeStruct((n_blocks * rows, d), BF16),
        compiler_params=pltpu.CompilerParams(
            dimension_semantics=("arbitrary",), vmem_limit_bytes=VMEM_LIMIT),
        name="experts",
    )(block_e, idx, idx, x, wg, wu, wd)


def _final_kernel(h_ref, y0_ref, y1_ref, gate_ref, gf_ref, o_ref):
    gates = gate_ref[...]
    hres = (h_ref[...] + gates[:, 0:1] * y0_ref[...].astype(F32)
            + gates[:, 1:2] * y1_ref[...].astype(F32))
    ms = jnp.mean(hres * hres, axis=-1, keepdims=True)
    o_ref[...] = hres * lax.rsqrt(ms + EPS) * gf_ref[...]


def _final(h, y0, y1, gates, gf, *, tm):
    t, d = h.shape
    tok = lambda w: pl.BlockSpec((tm, w), lambda i: (i, 0))
    return pl.pallas_call(
        _final_kernel,
        grid=(t // tm,),
        in_specs=[tok(d), tok(d), tok(d), tok(TOP_K), pl.BlockSpec((1, d), lambda i: (0, 0))],
        out_specs=tok(d),
        out_shape=jax.ShapeDtypeStruct((t, d), F32),
        compiler_params=pltpu.CompilerParams(
            dimension_semantics=("arbitrary",), vmem_limit_bytes=VMEM_LIMIT),
        name="combine_final_norm",
    )(h, y0, y1, gates, gf)


def _rope_tables(n_tok):
    rows = n_tok // GRID_W
    row = jnp.repeat(jnp.arange(rows, dtype=F32), GRID_W)
    col = jnp.tile(jnp.arange(GRID_W, dtype=F32), rows)
    inv = ROPE_THETA ** (-jnp.arange(0, AXIS_DIM, 2, dtype=F32) / AXIS_DIM)
    ar = row[:, None] * inv
    ac = col[:, None] * inv
    ang = jnp.concatenate([ar, ar, ac, ac], axis=-1)
    q = HEAD_DIM // 4
    sign = jnp.concatenate([-jnp.ones((q,), F32), jnp.ones((q,), F32)] * 2)
    return jnp.cos(ang).T, (jnp.sin(ang) * sign).T


def _t5_bucket(rp):
    nb = N_BUCKETS // 2
    max_exact = nb // 2
    ret = (rp > 0).astype(jnp.int32) * nb
    n = jnp.abs(rp)
    large = max_exact + (jnp.log(jnp.maximum(n, 1).astype(F32) / max_exact)
                         / math.log(MAX_DISTANCE / max_exact) * (nb - max_exact)).astype(jnp.int32)
    large = jnp.minimum(large, nb - 1)
    return ret + jnp.where(n < max_exact, n, large)


def _bias_lookup(rel_bias, d):
    return (rel_bias.astype(F32) * LOG2E)[_t5_bucket(d)].T


def _band_tables(rel_bias, tq):
    shift = BAND_SHIFT * KEY_CHUNK
    n_wrap = BAND_ROWS + tq
    u = jnp.concatenate(
        [_bias_lookup(rel_bias, jnp.arange(BAND_ROWS, dtype=jnp.int32) - shift),
         jnp.broadcast_to(_bias_lookup(rel_bias, jnp.array([-shift], jnp.int32)), (N_HEADS_B, tq))], axis=1)
    skew = jnp.broadcast_to(u[:, None, :], (N_HEADS_B, tq, n_wrap)).reshape(N_HEADS_B, tq * n_wrap)
    skew = skew[:, :tq * (n_wrap - 1)].reshape(N_HEADS_B, tq, n_wrap - 1)
    band = jnp.transpose(skew[:, :, :BAND_ROWS], (0, 2, 1))
    far = jnp.array([-2 * MAX_DISTANCE, 2 * MAX_DISTANCE], jnp.int32)
    b_max = jnp.max(rel_bias.astype(F32) * LOG2E, axis=0)[:, None]
    return band, jnp.concatenate([_bias_lookup(rel_bias, far), b_max], axis=1)


def _meta_bias(rel_bias, n_tok):
    v = _bias_lookup(rel_bias, -jnp.arange(n_tok + N_META + 1, dtype=jnp.int32))
    mb = jnp.stack([v[:, N_META - j:N_META - j + n_tok] for j in range(N_META)], axis=1)
    pad = jnp.full((N_HEADS_B, META_PAD - N_META, n_tok), MASK_VALUE, F32)
    return jnp.concatenate([mb, pad], axis=1)


def _dispatch_plan(eid, rows):
    t = eid.shape[1]
    n_assign = TOP_K * t
    flat = eid.reshape(-1)
    a_idx = jnp.arange(n_assign, dtype=jnp.int32)
    skey = jnp.sort(flat * n_assign + a_idx)
    s_assign = skey % n_assign
    experts = jnp.arange(N_EXPERTS, dtype=jnp.int32)
    bounds = jnp.arange(N_EXPERTS + 1, dtype=jnp.int32) * n_assign
    off_all = jnp.sum((skey[:, None] < bounds[None, :]).astype(jnp.int32), axis=0)
    off = off_all[:-1]
    counts = off_all[1:] - off
    padded = (counts + rows - 1) // rows * rows
    pend = jnp.cumsum(padded)
    poff = pend - padded
    step = jnp.diff(poff - off, prepend=0)
    dest = a_idx + jnp.sum(jnp.where(a_idx[:, None] >= off[None, :], step[None, :], 0), axis=1)
    _, pos = lax.sort((s_assign, dest), num_keys=1)
    n_blocks = (n_assign + N_EXPERTS * (rows - 1) + rows - 1) // rows
    cap = n_blocks * rows
    block_start = jnp.arange(n_blocks, dtype=jnp.int32) * rows
    block_e = jnp.minimum(jnp.sum((block_start[:, None] >= pend[None, :]).astype(jnp.int32), axis=1),
                          N_EXPERTS - 1)
    d = jnp.arange(cap, dtype=jnp.int32)
    e_d = jnp.repeat(block_e, rows)
    onehot_d = e_d[:, None] == experts[None, :]
    pick = lambda v: jnp.sum(jnp.where(onehot_d, v[None, :], 0), axis=1)
    j = d - pick(poff)
    valid = j < pick(counts)
    src = jnp.clip(pick(off) + j, 0, n_assign - 1)
    row_tok = jnp.where(valid, s_assign[src] % t, 0)
    return row_tok, block_e, pos.reshape(TOP_K, t)


def _key_bounds(knorm, knorm_meta):
    both = jnp.maximum(jnp.max(knorm, axis=1), knorm_meta[0, 0][None])
    return jnp.sqrt(both[:, 1:1 + N_KV_A, 0]), jnp.sqrt(both[:, 0, 0:2 * N_HEADS_B])


def _mixer(x, shared):
    bsz, n_tok, d = x.shape
    assert n_tok % Q_TILE == 0 and n_tok % TOKEN_TILE == 0
    tq = Q_TILE
    tm = TOKEN_TILE
    cos_t, sin_t = _rope_tables(n_tok)
    qa_t, ka, va_t, qb_t, kb, vb_t, knorm = _inproj(
        x, shared['g1'], shared['wt'], shared['wkb'], cos_t, sin_t, shared['qn'], shared['kn'],
        tm=tm, chunk=KEY_CHUNK)
    kmax_a, kmax_b = _key_bounds(knorm, shared['knorm_meta'])
    oa = _attn_a(kmax_a, qa_t, ka, va_t, shared['km_a'], shared['vm_a'], tq=tq)
    mbias = _meta_bias(shared['rel_bias'], n_tok)
    ob = _attn_b(shared['cfar'], kmax_b, qb_t, kb, vb_t, shared['km_b'], shared['vm_b'], shared['band'], mbias,
                 shared['lq1'], shared['lk1'], shared['lq2'], shared['lk2'], shared['sn'], tq=tq)
    h, n2, eid, gates = _outproj(oa, ob, x, shared['wa'], shared['wb'], shared['g2'],
                                 shared['wrh'], shared['wrl'], shared['br'], tm=tm)
    t = bsz * n_tok
    eid = jnp.transpose(eid, (1, 0, 2)).reshape(TOP_K, t)
    gates = jnp.transpose(gates, (1, 0, 2)).reshape(TOP_K, t)
    row_tok, block_e, pos = _dispatch_plan(eid, MOE_ROWS)
    return h.reshape(t, d), n2.reshape(t, d), row_tok, block_e, pos, gates.T


def _moe_and_norm(mixed, shape, shared):
    h, n2, row_tok, block_e, pos, gates = mixed
    y = _experts(block_e, row_tok, n2, shared['wg'], shared['wu'], shared['wd'], rows=MOE_ROWS)
    y0 = jnp.take(y, pos[0], axis=0)
    y1 = jnp.take(y, pos[1], axis=0)
    return _final(h, y0, y1, gates, shared['gf'], tm=TOKEN_TILE).reshape(shape)


def kernel(x_prompt, x_sample, meta_tokens, rel_bias, norm1, w_in, q_norm, k_norm, lambda_q1, lambda_k1,
           lambda_q2, lambda_k2, sub_norm, w_out, norm2, w_router_group, b_router_group, w_router_expert,
           b_router_expert, w_gate, w_up, w_down, final_norm):
    i = LAYER
    d = D_MODEL
    o0 = WIDTH_A
    o1 = o0 + KV_A
    o2 = o1 + KV_A
    o3 = o2 + WIDTH_B
    o4 = o3 + WIDTH_B
    w = w_in[i]
    wt = jnp.concatenate([w[:, :o2], w[:, o2:o3], w[:, o4:]], axis=1).T.astype(BF16)
    wr = jnp.concatenate([w_router_expert[i], w_router_group[i],
                          jnp.zeros((d, ROUTER_ROWS - N_EXPERTS - N_GROUPS), F32)], axis=1).T
    wrh = wr.astype(BF16)
    br = jnp.concatenate([b_router_expert[i], b_router_group[i],
                          jnp.zeros((ROUTER_ROWS - N_EXPERTS - N_GROUPS,), F32)]).reshape(ROUTER_ROWS, 1)
    shared = {
        'g1': norm1[i].reshape(1, d), 'wt': wt, 'wkb': w[:, o3:o4].astype(BF16),
        'qn': q_norm[i].reshape(HEAD_DIM, 1), 'kn': k_norm[i].reshape(HEAD_DIM, 1),
        'rel_bias': rel_bias,
        'lq1': lambda_q1[i].reshape(1, HEAD_DIM), 'lk1': lambda_k1[i].reshape(1, HEAD_DIM),
        'lq2': lambda_q2[i].reshape(1, HEAD_DIM), 'lk2': lambda_k2[i].reshape(1, HEAD_DIM),
        'sn': sub_norm[i].reshape(2 * HEAD_DIM, 1),
        'wa': w_out[i][:WIDTH_A].astype(BF16), 'wb': w_out[i][WIDTH_A:].astype(BF16),
        'g2': norm2[i].reshape(1, d), 'wrh': wrh, 'wrl': (wr - wrh.astype(F32)).astype(BF16), 'br': br,
        'wg': w_gate[i].astype(BF16), 'wu': w_up[i].astype(BF16), 'wd': w_down[i].astype(BF16),
        'gf': final_norm.reshape(1, d),
    }
    xm = jnp.concatenate([meta_tokens.astype(F32), jnp.zeros((META_PAD - N_META, d), F32)], axis=0)[None]
    ones = jnp.ones((HEAD_DIM, META_PAD), F32)
    _, km_a, vm_a, _, km_b, vm_b, knorm_meta = _inproj(
        xm, shared['g1'], shared['wt'], shared['wkb'], ones, 0.0 * ones, shared['qn'], shared['kn'],
        tm=META_PAD, chunk=META_PAD)
    shared['km_a'] = km_a[0]
    shared['vm_a'] = vm_a[0, 0]
    shared['km_b'] = km_b[0]
    shared['vm_b'] = vm_b[0, 0]
    shared['knorm_meta'] = knorm_meta
    shared['band'], shared['cfar'] = _band_tables(rel_bias, Q_TILE)
    mixed_prompt = _mixer(x_prompt, shared)
    mixed_sample = _mixer(x_sample, shared)
    return (_moe_and_norm(mixed_prompt, x_prompt.shape, shared),
            _moe_and_norm(mixed_sample, x_sample.shape, shared))
```

```python
import functools
import math

import jax
import jax.numpy as jnp
from jax import lax
from jax.experimental import pallas as pl
from jax.experimental.pallas import tpu as pltpu

D_MODEL = 1024
HEAD_DIM = 64
N_HEADS_A = 8
N_KV_A = 2
GQA_GROUP = N_HEADS_A // N_KV_A
N_HEADS_B = 4
AXIS_DIM = HEAD_DIM // 2
ROPE_THETA = 10000.0
GRID_W = 64
N_META = 16
N_BUCKETS = 32
MAX_DISTANCE = 128
WIDTH_A = N_HEADS_A * HEAD_DIM
WIDTH_B = N_HEADS_B * 2 * HEAD_DIM
KV_A = N_KV_A * HEAD_DIM
N_GROUPS = 4
EXPERTS_PER_GROUP = 8
N_EXPERTS = N_GROUPS * EXPERTS_PER_GROUP
TOP_K = 2
D_EXPERT = D_MODEL // 2
EPS = 1e-6
LAYER = 0
LAM_INIT = 0.8 - 0.6 * math.exp(-0.3 * LAYER)

LOG2E = 1.4426950408889634
Q_SCALE = (HEAD_DIM ** -0.5) * LOG2E
MASK_VALUE = -1e30

LANE = 128
BF16_ROWS = 16
META_PAD = LANE
KEY_CHUNK = 256
TOKEN_TILE = 512
Q_TILE = 512
MOE_ROWS = 512
ROUTER_ROWS = 40
VMEM_LIMIT = 48 * 1024 * 1024

VA_ROWS = HEAD_DIM + BF16_ROWS
VB_ROWS = 2 * HEAD_DIM + BF16_ROWS
TILE_CHUNKS = Q_TILE // KEY_CHUNK
NEAR_CHUNKS = TILE_CHUNKS + 2
BAND_SHIFT = 2
BAND_ROWS = (TILE_CHUNKS + 5) * KEY_CHUNK

BOUND_MARGIN = 1.01
L_MIN = 2.0 ** -64
D_TILES = D_MODEL // LANE
NORM_ROWS = 8
PART_LANES = 512
FAST_UNROLL = 4

F32 = jnp.float32
BF16 = jnp.bfloat16


def _nt_dot(a, b):
    return lax.dot_general(a, b, (((1,), (1,)), ((), ())), preferred_element_type=F32)


def _inproj_kernel(x_ref, g1_ref, wt_ref, wkb_ref, cos_ref, sin_ref, qn_ref, kn_ref,
                   qa_ref, ka_ref, va_ref, qb_ref, kb_ref, vb_ref, knorm_ref, *, chunk):
    x = x_ref[0]
    ms = jnp.mean(x * x, axis=-1, keepdims=True)
    n = (x * lax.rsqrt(ms + EPS) * g1_ref[...]).astype(BF16)
    pt = _nt_dot(wt_ref[...], n)
    kb = jnp.dot(n, wkb_ref[...], preferred_element_type=F32).astype(BF16)
    kb_ref[0] = kb
    kbf = kb.astype(F32)
    seg = (lax.broadcasted_iota(jnp.int32, (WIDTH_B, LANE), 0) // HEAD_DIM
           == lax.broadcasted_iota(jnp.int32, (WIDTH_B, LANE), 1)).astype(BF16)
    kb_sq = jnp.dot((kbf * kbf).astype(BF16), seg, preferred_element_type=F32)
    kb_max = jnp.max(kb_sq, axis=0, keepdims=True)
    cos = cos_ref[...]
    sin = sin_ref[...]
    q = HEAD_DIM // 4

    def norm_rope(blk, gain):
        ss = jnp.mean(blk * blk, axis=0, keepdims=True)
        y = blk * lax.rsqrt(ss + EPS) * gain
        swapped = jnp.concatenate([y[q:2 * q], y[0:q], y[3 * q:4 * q], y[2 * q:3 * q]], axis=0)
        return y * cos + swapped * sin

    qn = qn_ref[...]
    kn = kn_ref[...]
    for h in range(N_HEADS_A):
        blk = pt[h * HEAD_DIM:(h + 1) * HEAD_DIM]
        qa_ref[0, h * HEAD_DIM:(h + 1) * HEAD_DIM, :] = (norm_rope(blk, qn) * Q_SCALE).astype(BF16)
    o0 = WIDTH_A
    ka = jnp.concatenate(
        [norm_rope(pt[o0 + j * HEAD_DIM:o0 + (j + 1) * HEAD_DIM], kn) for j in range(N_KV_A)], axis=0)
    ka_ref[0] = ka.T.astype(BF16)
    kaf = ka.astype(BF16).astype(F32)
    ka_sq = kaf * kaf
    ka_max = [jnp.max(jnp.sum(ka_sq[g * HEAD_DIM:(g + 1) * HEAD_DIM], axis=0, keepdims=True),
                      axis=1, keepdims=True) for g in range(N_KV_A)]
    knorm_ref[0, 0] = jnp.concatenate(
        [kb_max] + [jnp.broadcast_to(v, (1, LANE)) for v in ka_max]
        + [jnp.zeros((NORM_ROWS - 1 - N_KV_A, LANE), F32)], axis=0)
    o1 = o0 + KV_A
    o2 = o1 + KV_A
    o3 = o2 + WIDTH_B
    tm = x.shape[0]
    vd = 2 * HEAD_DIM
    ones_rows = (lax.broadcasted_iota(jnp.int32, (BF16_ROWS, chunk), 0) == 0).astype(BF16)
    for j in range(tm // chunk):
        cols = slice(j * chunk, (j + 1) * chunk)
        for g in range(N_KV_A):
            va_ref[0, j, g * VA_ROWS:g * VA_ROWS + HEAD_DIM] = (
                pt[o1 + g * HEAD_DIM:o1 + (g + 1) * HEAD_DIM, cols].astype(BF16))
            va_ref[0, j, g * VA_ROWS + HEAD_DIM:(g + 1) * VA_ROWS] = ones_rows
        for h in range(N_HEADS_B):
            vb_ref[0, j, h * VB_ROWS:h * VB_ROWS + vd] = pt[o3 + h * vd:o3 + (h + 1) * vd, cols].astype(BF16)
            vb_ref[0, j, h * VB_ROWS + vd:(h + 1) * VB_ROWS] = ones_rows
    qb_ref[0] = (pt[o2:o3] * Q_SCALE).astype(BF16)


def _inproj(x, g1, wt, wkb, cos_t, sin_t, qn, kn, *, tm, chunk):
    bsz, n, d = x.shape
    rows_t = wt.shape[0]
    grid = (bsz, n // tm)
    full = lambda shape: pl.BlockSpec(shape, lambda b, i: (0,) * len(shape))
    va_rows = N_KV_A * VA_ROWS
    vb_rows = N_HEADS_B * VB_ROWS
    out_shape = (
        jax.ShapeDtypeStruct((bsz, WIDTH_A, n), BF16),
        jax.ShapeDtypeStruct((bsz, n, KV_A), BF16),
        jax.ShapeDtypeStruct((bsz, n // chunk, va_rows, chunk), BF16),
        jax.ShapeDtypeStruct((bsz, WIDTH_B, n), BF16),
        jax.ShapeDtypeStruct((bsz, n, WIDTH_B), BF16),
        jax.ShapeDtypeStruct((bsz, n // chunk, vb_rows, chunk), BF16),
        jax.ShapeDtypeStruct((bsz, n // tm, NORM_ROWS, LANE), F32),
    )
    return pl.pallas_call(
        functools.partial(_inproj_kernel, chunk=chunk),
        grid=grid,
        in_specs=[
            pl.BlockSpec((1, tm, d), lambda b, i: (b, i, 0)),
            full((1, d)),
            full((rows_t, d)),
            full((d, WIDTH_B)),
            pl.BlockSpec((HEAD_DIM, tm), lambda b, i: (0, i)),
            pl.BlockSpec((HEAD_DIM, tm), lambda b, i: (0, i)),
            full((HEAD_DIM, 1)),
            full((HEAD_DIM, 1)),
        ],
        out_specs=(
            pl.BlockSpec((1, WIDTH_A, tm), lambda b, i: (b, 0, i)),
            pl.BlockSpec((1, tm, KV_A), lambda b, i: (b, i, 0)),
            pl.BlockSpec((1, tm // chunk, va_rows, chunk), lambda b, i: (b, i, 0, 0)),
            pl.BlockSpec((1, WIDTH_B, tm), lambda b, i: (b, 0, i)),
            pl.BlockSpec((1, tm, WIDTH_B), lambda b, i: (b, i, 0)),
            pl.BlockSpec((1, tm // chunk, vb_rows, chunk), lambda b, i: (b, i, 0, 0)),
            pl.BlockSpec((1, 1, NORM_ROWS, LANE), lambda b, i: (b, i, 0, 0)),
        ),
        out_shape=out_shape,
        compiler_params=pltpu.CompilerParams(
            dimension_semantics=("arbitrary", "arbitrary"), vmem_limit_bytes=VMEM_LIMIT),
        name="inproj",
    )(x, g1, wt, wkb, cos_t, sin_t, qn, kn)


def _online_update(s, c_shift, m_ref, acc_ref, vblk, m_idx, acc_idx):
    mc = jnp.max(s, axis=0, keepdims=True)
    if c_shift is not None:
        mc = mc + c_shift
    m_old = m_ref[m_idx]
    m_new = jnp.maximum(m_old, mc)
    alpha = jnp.exp2(m_old - m_new)
    shift = m_new if c_shift is None else m_new - c_shift
    p = jnp.exp2(s - shift).astype(BF16)
    pv = jnp.dot(vblk, p, preferred_element_type=F32)
    acc_ref[acc_idx] = alpha * acc_ref[acc_idx] + pv
    m_ref[m_idx] = m_new


def _init_from_meta(s, m_ref, acc_ref, vmeta, idx):
    m0 = jnp.max(s, axis=0, keepdims=True)
    p = jnp.exp2(s - m0).astype(BF16)
    m_ref[idx] = m0
    acc_ref[idx] = jnp.dot(vmeta, p, preferred_element_type=F32)


def _bounded_update(s, shift, acc_ref, vblk, idx, first=False):
    p = jnp.exp2(s - shift).astype(BF16)
    pv = jnp.dot(vblk, p, preferred_element_type=F32)
    acc_ref[idx] = pv if first else acc_ref[idx] + pv


def _part(p, tq):
    c0 = p * PART_LANES
    return c0 // tq, slice(c0 % tq, c0 % tq + PART_LANES), slice(c0, c0 + PART_LANES)


def _query_norm(q_bf16):
    qf = q_bf16.astype(F32)
    return jnp.sqrt(jnp.sum(qf * qf, axis=0, keepdims=True))


def _pipelined_chunks(count, parts, chunk_of, qk_into, consume, sa_ref, sb_ref, unroll=2):
    assert unroll % 2 == 0 and count >= unroll and count % unroll == 0
    bufs = (sa_ref, sb_ref)

    def step(u, nxt, cur):
        for part in range(parts):
            if nxt is not None:
                qk_into(bufs[(u + 1) % 2], nxt, part)
            consume(bufs[u % 2], cur, part)

    for part in range(parts):
        qk_into(sa_ref, chunk_of(0), part)

    def body(j, carry):
        i = unroll * j
        for u in range(unroll):
            step(u, chunk_of(i + u + 1), chunk_of(i + u))
        return carry

    lax.fori_loop(0, count // unroll - 1, body, 0)
    i = count - unroll
    for u in range(unroll):
        step(u, chunk_of(i + u + 1) if u + 1 < unroll else None, chunk_of(i + u))


def _attn_a_kernel(kmax_ref, q_ref, k_ref, v_ref, km_ref, vm_ref, o_ref, qpad_ref, sa_ref, sb_ref, m_ref,
                   acc_ref, *, tq, nk):
    b = pl.program_id(0)
    g = pl.program_id(1)
    kmax = kmax_ref[b, g] * BOUND_MARGIN
    zero = jnp.zeros((HEAD_DIM, tq), BF16)
    for h in range(GQA_GROUP):
        qh = q_ref[0, h * HEAD_DIM:(h + 1) * HEAD_DIM, :]
        qpad_ref[0:HEAD_DIM, h * tq:(h + 1) * tq] = jnp.where(g == 0, qh, zero)
        qpad_ref[HEAD_DIM:2 * HEAD_DIM, h * tq:(h + 1) * tq] = jnp.where(g == 1, qh, zero)
        m_ref[h] = _query_norm(qh) * kmax

    valid = lax.broadcasted_iota(jnp.int32, (META_PAD, 1), 0) < N_META

    def meta_scores():
        s = jnp.dot(km_ref[...], qpad_ref[...], preferred_element_type=F32)
        return jnp.where(valid, s, MASK_VALUE)

    parts = GQA_GROUP * tq // PART_LANES

    def qk_into(dst, kc, p):
        cols = _part(p, tq)[2]
        dst[:, cols] = jnp.dot(k_ref[0, kc], qpad_ref[:, cols], preferred_element_type=F32)

    s = meta_scores()
    for h in range(GQA_GROUP):
        _bounded_update(s[:, h * tq:(h + 1) * tq], m_ref[h], acc_ref, vm_ref[...], h, first=True)

    def consume(src, kc, p):
        h, lanes, cols = _part(p, tq)
        _bounded_update(src[:, cols], m_ref[h, :, lanes], acc_ref, v_ref[0, kc], (h, slice(None), lanes))

    _pipelined_chunks(nk, parts, lambda i: i, qk_into, consume, sa_ref, sb_ref, unroll=FAST_UNROLL)

    l_min = jnp.min(acc_ref[0, HEAD_DIM:HEAD_DIM + 1, :])
    for h in range(1, GQA_GROUP):
        l_min = jnp.minimum(l_min, jnp.min(acc_ref[h, HEAD_DIM:HEAD_DIM + 1, :]))

    @pl.when(jnp.logical_not(l_min >= L_MIN))
    def _():
        s = meta_scores()
        for h in range(GQA_GROUP):
            _init_from_meta(s[:, h * tq:(h + 1) * tq], m_ref, acc_ref, vm_ref[...], h)

        def consume_exact(src, kc, p):
            h, lanes, cols = _part(p, tq)
            _online_update(src[:, cols], None, m_ref, acc_ref, v_ref[0, kc],
                           (h, slice(None), lanes), (h, slice(None), lanes))

        _pipelined_chunks(nk, parts, lambda i: i, qk_into, consume_exact, sa_ref, sb_ref)

    outs = []
    for h in range(GQA_GROUP):
        outs.append(acc_ref[h, 0:HEAD_DIM, :] / acc_ref[h, HEAD_DIM:HEAD_DIM + 1, :])
    o = jnp.concatenate(outs, axis=0)
    o_ref[0] = o.T.astype(BF16)


def _attn_a(kmax, qa_t, ka, va_t, km, vm_t, *, tq):
    bsz, _, n = qa_t.shape
    nk = n // KEY_CHUNK
    ka4 = ka.reshape(bsz, nk, KEY_CHUNK, KV_A)
    gw = GQA_GROUP * HEAD_DIM
    return pl.pallas_call(
        functools.partial(_attn_a_kernel, tq=tq, nk=nk),
        grid=(bsz, N_KV_A, n // tq),
        in_specs=[
            pl.BlockSpec(memory_space=pltpu.SMEM),
            pl.BlockSpec((1, gw, tq), lambda b, g, i: (b, g, i)),
            pl.BlockSpec((1, nk, KEY_CHUNK, KV_A), lambda b, g, i: (b, 0, 0, 0)),
            pl.BlockSpec((1, nk, VA_ROWS, KEY_CHUNK), lambda b, g, i: (b, 0, g, 0)),
            pl.BlockSpec((META_PAD, KV_A), lambda b, g, i: (0, 0)),
            pl.BlockSpec((VA_ROWS, META_PAD), lambda b, g, i: (g, 0)),
        ],
        out_specs=pl.BlockSpec((1, tq, gw), lambda b, g, i: (b, i, g)),
        out_shape=jax.ShapeDtypeStruct((bsz, n, WIDTH_A), BF16),
        scratch_shapes=[
            pltpu.VMEM((KV_A, GQA_GROUP * tq), BF16),
            pltpu.VMEM((KEY_CHUNK, GQA_GROUP * tq), F32),
            pltpu.VMEM((KEY_CHUNK, GQA_GROUP * tq), F32),
            pltpu.VMEM((GQA_GROUP, 1, tq), F32),
            pltpu.VMEM((GQA_GROUP, VA_ROWS, tq), F32),
        ],
        compiler_params=pltpu.CompilerParams(
            dimension_semantics=("arbitrary", "arbitrary", "arbitrary"), vmem_limit_bytes=VMEM_LIMIT),
        name="attn_a",
    )(kmax, qa_t, ka4, va_t, km, vm_t)


def _attn_b_kernel(cfar_ref, kmax_ref, q_ref, k_ref, v_ref, km_ref, vm_ref, band_ref, mbias_ref,
                   lq1_ref, lk1_ref, lq2_ref, lk2_ref, sn_ref, o_ref,
                   qblk_ref, sa_ref, sb_ref, m_ref, acc_ref, *, tq, nk):
    b = pl.program_id(0)
    h = pl.program_id(1)
    qi = pl.program_id(2)
    vd = 2 * HEAD_DIM
    zero = jnp.zeros((HEAD_DIM, tq), BF16)
    q1 = q_ref[0, 0:HEAD_DIM, :]
    q2 = q_ref[0, HEAD_DIM:vd, :]
    qblk_ref[0:HEAD_DIM, 0:tq] = q1
    qblk_ref[0:HEAD_DIM, tq:2 * tq] = zero
    qblk_ref[HEAD_DIM:vd, 0:tq] = zero
    qblk_ref[HEAD_DIM:vd, tq:2 * tq] = q2
    c_neg = cfar_ref[h, 0]
    c_pos = cfar_ref[h, 1]
    b_max = cfar_ref[h, 2]
    for j, qj in enumerate((q1, q2)):
        m_ref[j] = _query_norm(qj) * (kmax_ref[b, 2 * h + j] * BOUND_MARGIN) + b_max

    def meta_scores():
        return jnp.dot(km_ref[...], qblk_ref[...], preferred_element_type=F32)

    parts = 2 * tq // PART_LANES

    def qk_into(dst, info, p):
        cols = _part(p, tq)[2]
        dst[:, cols] = jnp.dot(k_ref[0, info[0]], qblk_ref[:, cols], preferred_element_type=F32)

    near_lo = jnp.clip(qi * TILE_CHUNKS - 1, 0, nk - NEAR_CHUNKS)

    def far_chunk(i):
        right = i >= near_lo
        return jnp.where(right, i + NEAR_CHUNKS, i), jnp.where(right, c_pos, c_neg)

    def near_chunk(i):
        kc = near_lo + i
        return kc, pl.multiple_of((kc - qi * TILE_CHUNKS + BAND_SHIFT) * KEY_CHUNK, KEY_CHUNK)

    s = meta_scores()
    for j in range(2):
        _bounded_update(s[:, j * tq:(j + 1) * tq] + mbias_ref[0], m_ref[j], acc_ref, vm_ref[...], j, first=True)

    def consume_far(src, info, p):
        kc, c = info
        j, lanes, cols = _part(p, tq)
        _bounded_update(src[:, cols], m_ref[j, :, lanes] - c, acc_ref, v_ref[0, kc],
                        (j, slice(None), lanes))

    def consume_near(src, info, p):
        kc, r0 = info
        j, lanes, cols = _part(p, tq)
        bias = band_ref[0, pl.ds(r0, KEY_CHUNK), lanes]
        _bounded_update(src[:, cols] + bias, m_ref[j, :, lanes], acc_ref, v_ref[0, kc],
                        (j, slice(None), lanes))

    _pipelined_chunks(nk - NEAR_CHUNKS, parts, far_chunk, qk_into, consume_far, sa_ref, sb_ref,
                      unroll=FAST_UNROLL)
    _pipelined_chunks(NEAR_CHUNKS, parts, near_chunk, qk_into, consume_near, sa_ref, sb_ref,
                      unroll=FAST_UNROLL)

    l_min = jnp.minimum(jnp.min(acc_ref[0, vd:vd + 1, :]), jnp.min(acc_ref[1, vd:vd + 1, :]))

    @pl.when(jnp.logical_not(l_min >= L_MIN))
    def _():
        s = meta_scores()
        for j in range(2):
            _init_from_meta(s[:, j * tq:(j + 1) * tq] + mbias_ref[0], m_ref, acc_ref, vm_ref[...], j)

        def exact_far(src, info, p):
            kc, c = info
            j, lanes, cols = _part(p, tq)
            _online_update(src[:, cols], c, m_ref, acc_ref, v_ref[0, kc],
                           (j, slice(None), lanes), (j, slice(None), lanes))

        def exact_near(src, info, p):
            kc, r0 = info
            j, lanes, cols = _part(p, tq)
            bias = band_ref[0, pl.ds(r0, KEY_CHUNK), lanes]
            _online_update(src[:, cols] + bias, None, m_ref, acc_ref, v_ref[0, kc],
                           (j, slice(None), lanes), (j, slice(None), lanes))

        _pipelined_chunks(nk - NEAR_CHUNKS, parts, far_chunk, qk_into, exact_far, sa_ref, sb_ref)
        _pipelined_chunks(NEAR_CHUNKS, parts, near_chunk, qk_into, exact_near, sa_ref, sb_ref)

    lam =(jnp.exp(jnp.sum(lq1_ref[...] * lk1_ref[...], axis=-1, keepdims=True))
           - jnp.exp(jnp.sum(lq2_ref[...] * lk2_ref[...], axis=-1, keepdims=True)) + LAM_INIT)
    o = (acc_ref[0, 0:vd, :] / acc_ref[0, vd:vd + 1, :]
         - lam * (acc_ref[1, 0:vd, :] / acc_ref[1, vd:vd + 1, :]))
    ms = jnp.mean(o * o, axis=0, keepdims=True)
    o = o * lax.rsqrt(ms + EPS) * sn_ref[...] * (1.0 - LAM_INIT)
    o_ref[0] = o.T.astype(BF16)


def _attn_b(cfar, kmax, qb_t, kb, vb_t, kmb, vmb_t, band, mbias, lq1, lk1, lq2, lk2, sn, *, tq):
    bsz, _, n = qb_t.shape
    nk = n // KEY_CHUNK
    assert tq == Q_TILE and NEAR_CHUNKS % 2 == 0 and (nk - NEAR_CHUNKS) % 2 == 0 and nk - NEAR_CHUNKS >= 2
    vd = 2 * HEAD_DIM
    kb4 = kb.reshape(bsz, nk, KEY_CHUNK, WIDTH_B)
    vec = lambda: pl.BlockSpec((1, HEAD_DIM), lambda b, h, i: (0, 0))
    return pl.pallas_call(
        functools.partial(_attn_b_kernel, tq=tq, nk=nk),
        grid=(bsz, N_HEADS_B, n // tq),
        in_specs=[
            pl.BlockSpec(memory_space=pltpu.SMEM),
            pl.BlockSpec(memory_space=pltpu.SMEM),
            pl.BlockSpec((1, vd, tq), lambda b, h, i: (b, h, i)),
            pl.BlockSpec((1, nk, KEY_CHUNK, vd), lambda b, h, i: (b, 0, 0, h)),
            pl.BlockSpec((1, nk, VB_ROWS, KEY_CHUNK), lambda b, h, i: (b, 0, h, 0)),
            pl.BlockSpec((META_PAD, vd), lambda b, h, i: (0, h)),
            pl.BlockSpec((VB_ROWS, META_PAD), lambda b, h, i: (h, 0)),
            pl.BlockSpec((1, BAND_ROWS, tq), lambda b, h, i: (h, 0, 0)),
            pl.BlockSpec((1, META_PAD, tq), lambda b, h, i: (h, 0, i)),
            vec(), vec(), vec(), vec(),
            pl.BlockSpec((vd, 1), lambda b, h, i: (0, 0)),
        ],
        out_specs=pl.BlockSpec((1, tq, vd), lambda b, h, i: (b, i, h)),
        out_shape=jax.ShapeDtypeStruct((bsz, n, WIDTH_B), BF16),
        scratch_shapes=[
            pltpu.VMEM((vd, 2 * tq), BF16),
            pltpu.VMEM((KEY_CHUNK, 2 * tq), F32),
            pltpu.VMEM((KEY_CHUNK, 2 * tq), F32),
            pltpu.VMEM((2, 1, tq), F32),
            pltpu.VMEM((2, VB_ROWS, tq), F32),
        ],
        compiler_params=pltpu.CompilerParams(
            dimension_semantics=("arbitrary", "arbitrary", "arbitrary"), vmem_limit_bytes=VMEM_LIMIT),
        name="attn_b",
    )(cfar, kmax, qb_t, kb4, vb_t, kmb, vmb_t, band, mbias, lq1, lk1, lq2, lk2, sn)


def _outproj_kernel(oa_ref, ob_ref, x_ref, wa_ref, wb_ref, g2_ref, wrh_ref, wrl_ref, br_ref,
                    h_ref, n2_ref, eid_ref, gate_ref):
    att = (jnp.dot(oa_ref[0], wa_ref[...], preferred_element_type=F32)
           + jnp.dot(ob_ref[0], wb_ref[...], preferred_element_type=F32))
    hres = x_ref[0] + att
    h_ref[0] = hres
    ms = jnp.mean(hres * hres, axis=-1, keepdims=True)
    n2 = hres * lax.rsqrt(ms + EPS) * g2_ref[...]
    tm = hres.shape[0]
    n2_hi = n2.astype(BF16)
    n2_lo = (n2 - n2_hi.astype(F32)).astype(BF16)
    for s in range(D_TILES):
        n2_ref[0, pl.ds(s, tm, stride=D_TILES), :] = n2[:, s * LANE:(s + 1) * LANE]
    wrh = wrh_ref[...]
    logits = _nt_dot(wrh, n2_hi) + _nt_dot(wrh, n2_lo) + _nt_dot(wrl_ref[...], n2_hi) + br_ref[...]
    gl = logits[N_EXPERTS:N_EXPERTS + N_GROUPS]
    gmax = jnp.max(gl, axis=0, keepdims=True)
    giota = lax.broadcasted_iota(jnp.int32, (N_GROUPS, tm), 0)
    g_idx = jnp.min(jnp.where(gl == gmax, giota, N_GROUPS), axis=0, keepdims=True)
    g_w = 1.0 / jnp.sum(jnp.exp(gl - gmax), axis=0, keepdims=True)
    sel = jnp.zeros((EXPERTS_PER_GROUP, tm), F32)
    for g in range(N_GROUPS):
        sel = jnp.where(g_idx == g, logits[g * EXPERTS_PER_GROUP:(g + 1) * EXPERTS_PER_GROUP], sel)
    eiota = lax.broadcasted_iota(jnp.int32, (EXPERTS_PER_GROUP, tm), 0)
    e1 = jnp.max(sel, axis=0, keepdims=True)
    i1 = jnp.min(jnp.where(sel == e1, eiota, EXPERTS_PER_GROUP), axis=0, keepdims=True)
    sel2 = jnp.where(eiota == i1, -jnp.inf, sel)
    e2 = jnp.max(sel2, axis=0, keepdims=True)
    i2 = jnp.min(jnp.where(sel2 == e2, eiota, EXPERTS_PER_GROUP), axis=0, keepdims=True)
    r = jnp.exp(e2 - e1)
    w1 = g_w / (1.0 + r)
    eid_ref[0] = jnp.concatenate([g_idx * EXPERTS_PER_GROUP + i1, g_idx * EXPERTS_PER_GROUP + i2], axis=0)
    gate_ref[0] = jnp.concatenate([w1, w1 * r], axis=0)


def _outproj(oa, ob, x, wa, wb, g2, wrh, wrl, br, *, tm):
    bsz, n, d = x.shape
    full = lambda shape: pl.BlockSpec(shape, lambda b, i: (0,) * len(shape))
    tok = lambda w: pl.BlockSpec((1, tm, w), lambda b, i: (b, i, 0))
    lane = lambda: pl.BlockSpec((1, TOP_K, tm), lambda b, i: (b, 0, i))
    return pl.pallas_call(
        _outproj_kernel,
        grid=(bsz, n // tm),
        in_specs=[tok(WIDTH_A), tok(WIDTH_B), tok(d), full((WIDTH_A, d)), full((WIDTH_B, d)), full((1, d)),
                  full((ROUTER_ROWS, d)), full((ROUTER_ROWS, d)), full((ROUTER_ROWS, 1))],
        out_specs=(tok(d), pl.BlockSpec((1, tm * D_TILES, LANE), lambda b, i: (b, i, 0)), lane(), lane()),
        out_shape=(
            jax.ShapeDtypeStruct((bsz, n, d), F32),
            jax.ShapeDtypeStruct((bsz, n * D_TILES, LANE), F32),
            jax.ShapeDtypeStruct((bsz, TOP_K, n), jnp.int32),
            jax.ShapeDtypeStruct((bsz, TOP_K, n), F32),
        ),
        compiler_params=pltpu.CompilerParams(
            dimension_semantics=("arbitrary", "arbitrary"), vmem_limit_bytes=VMEM_LIMIT),
        name="outproj_router",
    )(oa, ob, x, wa, wb, g2, wrh, wrl, br)


def _expert_kernel(be_ref, cur_ref, nxt_ref, x_hbm, wg_ref, wu_ref, wd_ref, y_ref, xbuf, sems, *, rows):
    i = pl.program_id(0)
    n_steps = pl.num_programs(0)
    slot = lax.rem(i, 2)

    def row_copy(tile_row, slot_, r):
        src = x_hbm.at[pl.ds(pl.multiple_of(tile_row, D_TILES), D_TILES), :]
        return pltpu.make_async_copy(src, xbuf.at[slot_, pl.ds(r * D_TILES, D_TILES), :], sems.at[slot_])

    @pl.when(i == 0)
    def _():
        for r in range(rows):
            row_copy(cur_ref[0, 0, r], 0, r).start()

    for r in range(rows):
        row_copy(0, slot, r).wait()
    x = jnp.concatenate([xbuf[slot, pl.ds(s, rows, stride=D_TILES), :] for s in range(D_TILES)],
                        axis=1).astype(BF16)
    gt = jnp.dot(x, wg_ref[0], preferred_element_type=F32)
    up = jnp.dot(x, wu_ref[0], preferred_element_type=F32)
    act = gt * (1.0 / (1.0 + jnp.exp(-gt))) * up
    y = jnp.dot(act.astype(BF16), wd_ref[0], preferred_element_type=F32).astype(BF16)

    for r in range(rows):
        row_copy(nxt_ref[0, 0, r], 1 - slot, r).start()
    y_ref[...] = y

    @pl.when(i == n_steps - 1)
    def _():
        for r in range(rows):
            row_copy(0, 1 - slot, r).wait()


def _experts(block_e, row_tok, x, wg, wu, wd, *, rows):
    d = D_MODEL
    n_blocks = block_e.shape[0]
    idx = (row_tok * D_TILES).reshape(n_blocks, 1, rows)
    smem_rows = lambda shift: pl.BlockSpec(
        (1, 1, rows), lambda i, be: (jnp.minimum(i + shift, n_blocks - 1), 0, 0), memory_space=pltpu.SMEM)
    grid_spec = pltpu.PrefetchScalarGridSpec(
        num_scalar_prefetch=1,
        grid=(n_blocks,),
        in_specs=[
            smem_rows(0),
            smem_rows(1),
            pl.BlockSpec(memory_space=pl.ANY),
            pl.BlockSpec((1, d, D_EXPERT), lambda i, be: (be[i], 0, 0)),
            pl.BlockSpec((1, d, D_EXPERT), lambda i, be: (be[i], 0, 0)),
            pl.BlockSpec((1, D_EXPERT, d), lambda i, be: (be[i], 0, 0)),
        ],
        out_specs=pl.BlockSpec((rows, d), lambda i, be: (i, 0)),
        scratch_shapes=[pltpu.VMEM((2, rows * D_TILES, LANE), F32), pltpu.SemaphoreType.DMA((2,))],
    )
    return pl.pallas_call(
        functools.partial(_expert_kernel, rows=rows),
        grid_spec=grid_spec,
        out_shape=jax.ShapeDtypeStruct((n_blocks * rows, d), BF16),
        compiler_params=pltpu.CompilerParams(
            dimension_semantics=("arbitrary",), vmem_limit_bytes=VMEM_LIMIT),
        name="experts",
    )(block_e, idx, idx, x, wg, wu, wd)


def _final_kernel(h_ref, y0_ref, y1_ref, gate_ref, gf_ref, o_ref):
    gates = gate_ref[...]
    hres = (h_ref[...] + gates[:, 0:1] * y0_ref[...].astype(F32)
            + gates[:, 1:2] * y1_ref[...].astype(F32))
    ms = jnp.mean(hres * hres, axis=-1, keepdims=True)
    o_ref[...] = hres * lax.rsqrt(ms + EPS) * gf_ref[...]


def _final(h, y0, y1, gates, gf, *, tm):
    t, d = h.shape
    tok = lambda w: pl.BlockSpec((tm, w), lambda i: (i, 0))
    return pl.pallas_call(
        _final_kernel,
        grid=(t // tm,),
        in_specs=[tok(d), tok(d), tok(d), tok(TOP_K), pl.BlockSpec((1, d), lambda i: (0, 0))],
        out_specs=tok(d),
        out_shape=jax.ShapeDtypeStruct((t, d), F32),
        compiler_params=pltpu.CompilerParams(
            dimension_semantics=("arbitrary",), vmem_limit_bytes=VMEM_LIMIT),
        name="combine_final_norm",
    )(h, y0, y1, gates, gf)


def _rope_tables(n_tok):
    rows = n_tok // GRID_W
    row = jnp.repeat(jnp.arange(rows, dtype=F32), GRID_W)
    col = jnp.tile(jnp.arange(GRID_W, dtype=F32), rows)
    inv = ROPE_THETA ** (-jnp.arange(0, AXIS_DIM, 2, dtype=F32) / AXIS_DIM)
    ar = row[:, None] * inv
    ac = col[:, None] * inv
    ang = jnp.concatenate([ar, ar, ac, ac], axis=-1)
    q = HEAD_DIM // 4
    sign = jnp.concatenate([-jnp.ones((q,), F32), jnp.ones((q,), F32)] * 2)
    return jnp.cos(ang).T, (jnp.sin(ang) * sign).T


def _t5_bucket(rp):
    nb = N_BUCKETS // 2
    max_exact = nb // 2
    ret = (rp > 0).astype(jnp.int32) * nb
    n = jnp.abs(rp)
    large = max_exact + (jnp.log(jnp.maximum(n, 1).astype(F32) / max_exact)
                         / math.log(MAX_DISTANCE / max_exact) * (nb - max_exact)).astype(jnp.int32)
    large = jnp.minimum(large, nb - 1)
    return ret + jnp.where(n < max_exact, n, large)


def _bias_lookup(rel_bias, d):
    return (rel_bias.astype(F32) * LOG2E)[_t5_bucket(d)].T


def _band_tables(rel_bias, tq):
    assert MAX_DISTANCE <= LANE
    far = jnp.array([-2 * MAX_DISTANCE, 2 * MAX_DISTANCE], jnp.int32)
    c_far = _bias_lookup(rel_bias, far)
    rev = _bias_lookup(rel_bias, 2 * LANE - 1 - jnp.arange(4 * LANE - 1, dtype=jnp.int32))
    strip = jnp.stack([rev[:, LANE - 1 - rho:4 * LANE - 1 - rho] for rho in range(LANE)], axis=1)
    const = lambda col: jnp.broadcast_to(c_far[:, col, None, None], (N_HEADS_B, LANE, LANE))

    def block(k):
        if k <= -2 or k >= 2:
            return const(0 if k < 0 else 1)
        return strip[:, :, (1 - k) * LANE:(2 - k) * LANE]

    shift_blocks = BAND_SHIFT * KEY_CHUNK // LANE
    band = jnp.concatenate(
        [jnp.concatenate([block(a - b - shift_blocks) for b in range(tq // LANE)], axis=2)
         for a in range(BAND_ROWS // LANE)], axis=1)
    b_max = jnp.max(rel_bias.astype(F32) * LOG2E, axis=0)[:, None]
    return band, jnp.concatenate([c_far, b_max], axis=1)


def _meta_bias(rel_bias, n_tok):
    v = _bias_lookup(rel_bias, -jnp.arange(n_tok + N_META + 1, dtype=jnp.int32))
    mb = jnp.stack([v[:, N_META - j:N_META - j + n_tok] for j in range(N_META)], axis=1)
    pad = jnp.full((N_HEADS_B, META_PAD - N_META, n_tok), MASK_VALUE, F32)
    return jnp.concatenate([mb, pad], axis=1)


def _dispatch_plan(eid, rows):
    t = eid.shape[1]
    n_assign = TOP_K * t
    flat = eid.reshape(-1)
    a_idx = jnp.arange(n_assign, dtype=jnp.int32)
    skey = jnp.sort(flat * n_assign + a_idx)
    s_assign = skey % n_assign
    experts = jnp.arange(N_EXPERTS, dtype=jnp.int32)
    bounds = jnp.arange(N_EXPERTS + 1, dtype=jnp.int32) * n_assign
    off_all = jnp.sum((skey[:, None] < bounds[None, :]).astype(jnp.int32), axis=0)
    off = off_all[:-1]
    counts = off_all[1:] - off
    padded = (counts + rows - 1) // rows * rows
    pend = jnp.cumsum(padded)
    poff = pend - padded
    step = jnp.diff(poff - off, prepend=0)
    dest = a_idx + jnp.sum(jnp.where(a_idx[:, None] >= off[None, :], step[None, :], 0), axis=1)
    _, pos = lax.sort((s_assign, dest), num_keys=1)
    n_blocks = (n_assign + N_EXPERTS * (rows - 1) + rows - 1) // rows
    cap = n_blocks * rows
    block_start = jnp.arange(n_blocks, dtype=jnp.int32) * rows
    block_e = jnp.minimum(jnp.sum((block_start[:, None] >= pend[None, :]).astype(jnp.int32), axis=1),
                          N_EXPERTS - 1)
    d = jnp.arange(cap, dtype=jnp.int32)
    e_d = jnp.repeat(block_e, rows)
    onehot_d = e_d[:, None] == experts[None, :]
    pick = lambda v: jnp.sum(jnp.where(onehot_d, v[None, :], 0), axis=1)
    j = d - pick(poff)
    valid = j < pick(counts)
    src = jnp.clip(pick(off) + j, 0, n_assign - 1)
    row_tok = jnp.where(valid, s_assign[src] % t, 0)
    return row_tok, block_e, pos.reshape(TOP_K, t)


def _key_bounds(knorm, knorm_meta):
    both = jnp.maximum(jnp.max(knorm, axis=1), knorm_meta[0, 0][None])
    return jnp.sqrt(both[:, 1:1 + N_KV_A, 0]), jnp.sqrt(both[:, 0, 0:2 * N_HEADS_B])


def _mixer(x, shared):
    bsz, n_tok, d = x.shape
    assert n_tok % Q_TILE == 0 and n_tok % TOKEN_TILE == 0
    tq = Q_TILE
    tm = TOKEN_TILE
    cos_t, sin_t = _rope_tables(n_tok)
    qa_t, ka, va_t, qb_t, kb, vb_t, knorm = _inproj(
        x, shared['g1'], shared['wt'], shared['wkb'], cos_t, sin_t, shared['qn'], shared['kn'],
        tm=tm, chunk=KEY_CHUNK)
    kmax_a, kmax_b = _key_bounds(knorm, shared['knorm_meta'])
    oa = _attn_a(kmax_a, qa_t, ka, va_t, shared['km_a'], shared['vm_a'], tq=tq)
    mbias = _meta_bias(shared['rel_bias'], n_tok)
    ob = _attn_b(shared['cfar'], kmax_b, qb_t, kb, vb_t, shared['km_b'], shared['vm_b'], shared['band'], mbias,
                 shared['lq1'], shared['lk1'], shared['lq2'], shared['lk2'], shared['sn'], tq=tq)
    h, n2, eid, gates = _outproj(oa, ob, x, shared['wa'], shared['wb'], shared['g2'],
                                 shared['wrh'], shared['wrl'], shared['br'], tm=tm)
    t = bsz * n_tok
    eid = jnp.transpose(eid, (1, 0, 2)).reshape(TOP_K, t)
    gates = jnp.transpose(gates, (1, 0, 2)).reshape(TOP_K, t)
    row_tok, block_e, pos = _dispatch_plan(eid, MOE_ROWS)
    return h.reshape(t, d), n2.reshape(t * D_TILES, LANE), row_tok, block_e, pos, gates.T


def _moe_and_norm(mixed, shape, shared):
    h, n2, row_tok, block_e, pos, gates = mixed
    y = _experts(block_e, row_tok, n2, shared['wg'], shared['wu'], shared['wd'], rows=MOE_ROWS)
    y0 = jnp.take(y, pos[0], axis=0)
    y1 = jnp.take(y, pos[1], axis=0)
    return _final(h, y0, y1, gates, shared['gf'], tm=TOKEN_TILE).reshape(shape)


def kernel(x_prompt, x_sample, meta_tokens, rel_bias, norm1, w_in, q_norm, k_norm, lambda_q1, lambda_k1,
           lambda_q2, lambda_k2, sub_norm, w_out, norm2, w_router_group, b_router_group, w_router_expert,
           b_router_expert, w_gate, w_up, w_down, final_norm):
    i = LAYER
    d = D_MODEL
    o0 = WIDTH_A
    o1 = o0 + KV_A
    o2 = o1 + KV_A
    o3 = o2 + WIDTH_B
    o4 = o3 + WIDTH_B
    w = w_in[i]
    wt = jnp.concatenate([w[:, :o2], w[:, o2:o3], w[:, o4:]], axis=1).T.astype(BF16)
    wr = jnp.concatenate([w_router_expert[i], w_router_group[i],
                          jnp.zeros((d, ROUTER_ROWS - N_EXPERTS - N_GROUPS), F32)], axis=1).T
    wrh = wr.astype(BF16)
    br = jnp.concatenate([b_router_expert[i], b_router_group[i],
                          jnp.zeros((ROUTER_ROWS - N_EXPERTS - N_GROUPS,), F32)]).reshape(ROUTER_ROWS, 1)
    shared = {
        'g1': norm1[i].reshape(1, d), 'wt': wt, 'wkb': w[:, o3:o4].astype(BF16),
        'qn': q_norm[i].reshape(HEAD_DIM, 1), 'kn': k_norm[i].reshape(HEAD_DIM, 1),
        'rel_bias': rel_bias,
        'lq1': lambda_q1[i].reshape(1, HEAD_DIM), 'lk1': lambda_k1[i].reshape(1, HEAD_DIM),
        'lq2': lambda_q2[i].reshape(1, HEAD_DIM), 'lk2': lambda_k2[i].reshape(1, HEAD_DIM),
        'sn': sub_norm[i].reshape(2 * HEAD_DIM, 1),
        'wa': w_out[i][:WIDTH_A].astype(BF16), 'wb': w_out[i][WIDTH_A:].astype(BF16),
        'g2': norm2[i].reshape(1, d), 'wrh': wrh, 'wrl': (wr - wrh.astype(F32)).astype(BF16), 'br': br,
        'wg': w_gate[i].astype(BF16), 'wu': w_up[i].astype(BF16), 'wd': w_down[i].astype(BF16),
        'gf': final_norm.reshape(1, d),
    }
    xm = jnp.concatenate([meta_tokens.astype(F32), jnp.zeros((META_PAD - N_META, d), F32)], axis=0)[None]
    ones = jnp.ones((HEAD_DIM, META_PAD), F32)
    _, km_a, vm_a, _, km_b, vm_b, knorm_meta = _inproj(
        xm, shared['g1'], shared['wt'], shared['wkb'], ones, 0.0 * ones, shared['qn'], shared['kn'],
        tm=META_PAD, chunk=META_PAD)
    shared['km_a'] = km_a[0]
    shared['vm_a'] = vm_a[0, 0]
    shared['km_b'] = km_b[0]
    shared['vm_b'] = vm_b[0, 0]
    shared['knorm_meta'] = knorm_meta
    shared['band'], shared['cfar'] = _band_tables(rel_bias, Q_TILE)
    mixed_prompt = _mixer(x_prompt, shared)
    mixed_sample = _mixer(x_sample, shared)
    return (_moe_and_norm(mixed_prompt, x_prompt.shape, shared),
            _moe_and_norm(mixed_sample, x_sample.shape, shared))
```

```python
import functools
import math

import jax
import jax.numpy as jnp
from jax import lax
from jax.experimental import pallas as pl
from jax.experimental.pallas import tpu as pltpu

D_MODEL = 1024
HEAD_DIM = 64
N_HEADS_A = 8
N_KV_A = 2
GQA_GROUP = N_HEADS_A // N_KV_A
N_HEADS_B = 4
AXIS_DIM = HEAD_DIM // 2
ROPE_THETA = 10000.0
GRID_W = 64
N_META = 16
N_BUCKETS = 32
MAX_DISTANCE = 128
WIDTH_A = N_HEADS_A * HEAD_DIM
WIDTH_B = N_HEADS_B * 2 * HEAD_DIM
KV_A = N_KV_A * HEAD_DIM
N_GROUPS = 4
EXPERTS_PER_GROUP = 8
N_EXPERTS = N_GROUPS * EXPERTS_PER_GROUP
TOP_K = 2
D_EXPERT = D_MODEL // 2
EPS = 1e-6
LAYER = 0
LAM_INIT = 0.8 - 0.6 * math.exp(-0.3 * LAYER)

LOG2E = 1.4426950408889634
Q_SCALE = (HEAD_DIM ** -0.5) * LOG2E
MASK_VALUE = -1e30

LANE = 128
BF16_ROWS = 16
META_PAD = LANE
KEY_CHUNK = 256
TOKEN_TILE = 512
Q_TILE = 512
MOE_ROWS = 512
ROUTER_ROWS = 40
VMEM_LIMIT = 48 * 1024 * 1024

VA_ROWS = HEAD_DIM + BF16_ROWS
VB_ROWS = 2 * HEAD_DIM + BF16_ROWS
TILE_CHUNKS = Q_TILE // KEY_CHUNK
NEAR_CHUNKS = TILE_CHUNKS + 2
BAND_SHIFT = 2
BAND_ROWS = (TILE_CHUNKS + 5) * KEY_CHUNK

BOUND_MARGIN = 1.01
L_MIN = 2.0 ** -64
D_TILES = D_MODEL // LANE
NORM_ROWS = 8
PART_LANES = 512
FAST_UNROLL = 4

F32 = jnp.float32
BF16 = jnp.bfloat16


def _nt_dot(a, b):
    return lax.dot_general(a, b, (((1,), (1,)), ((), ())), preferred_element_type=F32)


def _inproj_kernel(x_ref, g1_ref, wt_ref, wkb_ref, cos_ref, sin_ref, qn_ref, kn_ref,
                   qa_ref, ka_ref, va_ref, qb_ref, kb_ref, vb_ref, knorm_ref, *, chunk):
    x = x_ref[0]
    ms = jnp.mean(x * x, axis=-1, keepdims=True)
    n = (x * lax.rsqrt(ms + EPS) * g1_ref[...]).astype(BF16)
    pt = _nt_dot(wt_ref[...], n)
    kb = jnp.dot(n, wkb_ref[...], preferred_element_type=F32).astype(BF16)
    kb_ref[0] = kb
    kbf = kb.astype(F32)
    seg = (lax.broadcasted_iota(jnp.int32, (WIDTH_B, LANE), 0) // HEAD_DIM
           == lax.broadcasted_iota(jnp.int32, (WIDTH_B, LANE), 1)).astype(BF16)
    kb_sq = jnp.dot((kbf * kbf).astype(BF16), seg, preferred_element_type=F32)
    kb_max = jnp.max(kb_sq, axis=0, keepdims=True)
    cos = cos_ref[...]
    sin = sin_ref[...]
    q = HEAD_DIM // 4

    def norm_rope(blk, gain):
        ss = jnp.mean(blk * blk, axis=0, keepdims=True)
        y = blk * lax.rsqrt(ss + EPS) * gain
        swapped = jnp.concatenate([y[q:2 * q], y[0:q], y[3 * q:4 * q], y[2 * q:3 * q]], axis=0)
        return y * cos + swapped * sin

    qn = qn_ref[...]
    kn = kn_ref[...]
    for h in range(N_HEADS_A):
        blk = pt[h * HEAD_DIM:(h + 1) * HEAD_DIM]
        qa_ref[0, h * HEAD_DIM:(h + 1) * HEAD_DIM, :] = (norm_rope(blk, qn) * Q_SCALE).astype(BF16)
    o0 = WIDTH_A
    ka = jnp.concatenate(
        [norm_rope(pt[o0 + j * HEAD_DIM:o0 + (j + 1) * HEAD_DIM], kn) for j in range(N_KV_A)], axis=0)
    ka_ref[0] = ka.T.astype(BF16)
    kaf = ka.astype(BF16).astype(F32)
    ka_sq = kaf * kaf
    ka_max = [jnp.max(jnp.sum(ka_sq[g * HEAD_DIM:(g + 1) * HEAD_DIM], axis=0, keepdims=True),
                      axis=1, keepdims=True) for g in range(N_KV_A)]
    knorm_ref[0, 0] = jnp.concatenate(
        [kb_max] + [jnp.broadcast_to(v, (1, LANE)) for v in ka_max]
        + [jnp.zeros((NORM_ROWS - 1 - N_KV_A, LANE), F32)], axis=0)
    o1 = o0 + KV_A
    o2 = o1 + KV_A
    o3 = o2 + WIDTH_B
    tm = x.shape[0]
    vd = 2 * HEAD_DIM
    ones_rows = (lax.broadcasted_iota(jnp.int32, (BF16_ROWS, chunk), 0) == 0).astype(BF16)
    for j in range(tm // chunk):
        cols = slice(j * chunk, (j + 1) * chunk)
        for g in range(N_KV_A):
            va_ref[0, j, g * VA_ROWS:g * VA_ROWS + HEAD_DIM] = (
                pt[o1 + g * HEAD_DIM:o1 + (g + 1) * HEAD_DIM, cols].astype(BF16))
            va_ref[0, j, g * VA_ROWS + HEAD_DIM:(g + 1) * VA_ROWS] = ones_rows
        for h in range(N_HEADS_B):
            vb_ref[0, j, h * VB_ROWS:h * VB_ROWS + vd] = pt[o3 + h * vd:o3 + (h + 1) * vd, cols].astype(BF16)
            vb_ref[0, j, h * VB_ROWS + vd:(h + 1) * VB_ROWS] = ones_rows
    qb_ref[0] = (pt[o2:o3] * Q_SCALE).astype(BF16)


def _inproj(x, g1, wt, wkb, cos_t, sin_t, qn, kn, *, tm, chunk):
    bsz, n, d = x.shape
    rows_t = wt.shape[0]
    grid = (bsz, n // tm)
    full = lambda shape: pl.BlockSpec(shape, lambda b, i: (0,) * len(shape))
    va_rows = N_KV_A * VA_ROWS
    vb_rows = N_HEADS_B * VB_ROWS
    out_shape = (
        jax.ShapeDtypeStruct((bsz, WIDTH_A, n), BF16),
        jax.ShapeDtypeStruct((bsz, n, KV_A), BF16),
        jax.ShapeDtypeStruct((bsz, n // chunk, va_rows, chunk), BF16),
        jax.ShapeDtypeStruct((bsz, WIDTH_B, n), BF16),
        jax.ShapeDtypeStruct((bsz, n, WIDTH_B), BF16),
        jax.ShapeDtypeStruct((bsz, n // chunk, vb_rows, chunk), BF16),
        jax.ShapeDtypeStruct((bsz, n // tm, NORM_ROWS, LANE), F32),
    )
    return pl.pallas_call(
        functools.partial(_inproj_kernel, chunk=chunk),
        grid=grid,
        in_specs=[
            pl.BlockSpec((1, tm, d), lambda b, i: (b, i, 0)),
            full((1, d)),
            full((rows_t, d)),
            full((d, WIDTH_B)),
            pl.BlockSpec((HEAD_DIM, tm), lambda b, i: (0, i)),
            pl.BlockSpec((HEAD_DIM, tm), lambda b, i: (0, i)),
            full((HEAD_DIM, 1)),
            full((HEAD_DIM, 1)),
        ],
        out_specs=(
            pl.BlockSpec((1, WIDTH_A, tm), lambda b, i: (b, 0, i)),
            pl.BlockSpec((1, tm, KV_A), lambda b, i: (b, i, 0)),
            pl.BlockSpec((1, tm // chunk, va_rows, chunk), lambda b, i: (b, i, 0, 0)),
            pl.BlockSpec((1, WIDTH_B, tm), lambda b, i: (b, 0, i)),
            pl.BlockSpec((1, tm, WIDTH_B), lambda b, i: (b, i, 0)),
            pl.BlockSpec((1, tm // chunk, vb_rows, chunk), lambda b, i: (b, i, 0, 0)),
            pl.BlockSpec((1, 1, NORM_ROWS, LANE), lambda b, i: (b, i, 0, 0)),
        ),
        out_shape=out_shape,
        compiler_params=pltpu.CompilerParams(
            dimension_semantics=("arbitrary", "arbitrary"), vmem_limit_bytes=VMEM_LIMIT),
        name="inproj",
    )(x, g1, wt, wkb, cos_t, sin_t, qn, kn)


def _online_update(s, c_shift, m_ref, acc_ref, vblk, m_idx, acc_idx):
    mc = jnp.max(s, axis=0, keepdims=True)
    if c_shift is not None:
        mc = mc + c_shift
    m_old = m_ref[m_idx]
    m_new = jnp.maximum(m_old, mc)
    alpha = jnp.exp2(m_old - m_new)
    shift = m_new if c_shift is None else m_new - c_shift
    p = jnp.exp2(s - shift).astype(BF16)
    pv = jnp.dot(vblk, p, preferred_element_type=F32)
    acc_ref[acc_idx] = alpha * acc_ref[acc_idx] + pv
    m_ref[m_idx] = m_new


def _init_from_meta(s, m_ref, acc_ref, vmeta, idx):
    m0 = jnp.max(s, axis=0, keepdims=True)
    p = jnp.exp2(s - m0).astype(BF16)
    m_ref[idx] = m0
    acc_ref[idx] = jnp.dot(vmeta, p, preferred_element_type=F32)


def _bounded_update(s, shift, acc_ref, vblk, idx, first=False):
    p = jnp.exp2(s - shift).astype(BF16)
    pv = jnp.dot(vblk, p, preferred_element_type=F32)
    acc_ref[idx] = pv if first else acc_ref[idx] + pv


def _part(p, tq):
    c0 = p * PART_LANES
    return c0 // tq, slice(c0 % tq, c0 % tq + PART_LANES), slice(c0, c0 + PART_LANES)


def _query_norm(q_bf16):
    qf = q_bf16.astype(F32)
    return jnp.sqrt(jnp.sum(qf * qf, axis=0, keepdims=True))


def _pipelined_chunks(count, parts, chunk_of, qk_into, consume, sa_ref, sb_ref, unroll=2):
    assert unroll % 2 == 0 and count >= unroll and count % unroll == 0
    bufs = (sa_ref, sb_ref)

    def step(u, nxt, cur):
        for part in range(parts):
            if nxt is not None:
                qk_into(bufs[(u + 1) % 2], nxt, part)
            consume(bufs[u % 2], cur, part)

    for part in range(parts):
        qk_into(sa_ref, chunk_of(0), part)

    def body(j, carry):
        i = unroll * j
        for u in range(unroll):
            step(u, chunk_of(i + u + 1), chunk_of(i + u))
        return carry

    lax.fori_loop(0, count // unroll - 1, body, 0)
    i = count - unroll
    for u in range(unroll):
        step(u, chunk_of(i + u + 1) if u + 1 < unroll else None, chunk_of(i + u))


def _attn_a_kernel(kmax_ref, q_ref, k_ref, v_ref, km_ref, vm_ref, o_ref, qpad_ref, sa_ref, sb_ref, m_ref,
                   acc_ref, *, tq, nk):
    b = pl.program_id(0)
    g = pl.program_id(1)
    kmax = kmax_ref[b, g] * BOUND_MARGIN
    zero = jnp.zeros((HEAD_DIM, tq), BF16)
    for h in range(GQA_GROUP):
        qh = q_ref[0, h * HEAD_DIM:(h + 1) * HEAD_DIM, :]
        qpad_ref[0:HEAD_DIM, h * tq:(h + 1) * tq] = jnp.where(g == 0, qh, zero)
        qpad_ref[HEAD_DIM:2 * HEAD_DIM, h * tq:(h + 1) * tq] = jnp.where(g == 1, qh, zero)
        m_ref[h] = _query_norm(qh) * kmax

    valid = lax.broadcasted_iota(jnp.int32, (META_PAD, 1), 0) < N_META

    def meta_scores():
        s = jnp.dot(km_ref[...], qpad_ref[...], preferred_element_type=F32)
        return jnp.where(valid, s, MASK_VALUE)

    parts = GQA_GROUP * tq // PART_LANES

    def qk_into(dst, kc, p):
        cols = _part(p, tq)[2]
        dst[:, cols] = jnp.dot(k_ref[0, kc], qpad_ref[:, cols], preferred_element_type=F32)

    s = meta_scores()
    for h in range(GQA_GROUP):
        _bounded_update(s[:, h * tq:(h + 1) * tq], m_ref[h], acc_ref, vm_ref[...], h, first=True)

    def consume(src, kc, p):
        h, lanes, cols = _part(p, tq)
        _bounded_update(src[:, cols], m_ref[h, :, lanes], acc_ref, v_ref[0, kc], (h, slice(None), lanes))

    _pipelined_chunks(nk, parts, lambda i: i, qk_into, consume, sa_ref, sb_ref, unroll=FAST_UNROLL)

    l_min = jnp.min(acc_ref[0, HEAD_DIM:HEAD_DIM + 1, :])
    for h in range(1, GQA_GROUP):
        l_min = jnp.minimum(l_min, jnp.min(acc_ref[h, HEAD_DIM:HEAD_DIM + 1, :]))

    @pl.when(jnp.logical_not(l_min >= L_MIN))
    def _():
        s = meta_scores()
        for h in range(GQA_GROUP):
            _init_from_meta(s[:, h * tq:(h + 1) * tq], m_ref, acc_ref, vm_ref[...], h)

        def consume_exact(src, kc, p):
            h, lanes, cols = _part(p, tq)
            _online_update(src[:, cols], None, m_ref, acc_ref, v_ref[0, kc],
                           (h, slice(None), lanes), (h, slice(None), lanes))

        _pipelined_chunks(nk, parts, lambda i: i, qk_into, consume_exact, sa_ref, sb_ref)

    outs = []
    for h in range(GQA_GROUP):
        outs.append(acc_ref[h, 0:HEAD_DIM, :] / acc_ref[h, HEAD_DIM:HEAD_DIM + 1, :])
    o = jnp.concatenate(outs, axis=0)
    o_ref[0] = o.T.astype(BF16)


def _attn_a(kmax, qa_t, ka, va_t, km, vm_t, *, tq):
    bsz, _, n = qa_t.shape
    nk = n // KEY_CHUNK
    ka4 = ka.reshape(bsz, nk, KEY_CHUNK, KV_A)
    gw = GQA_GROUP * HEAD_DIM
    return pl.pallas_call(
        functools.partial(_attn_a_kernel, tq=tq, nk=nk),
        grid=(bsz, N_KV_A, n // tq),
        in_specs=[
            pl.BlockSpec(memory_space=pltpu.SMEM),
            pl.BlockSpec((1, gw, tq), lambda b, g, i: (b, g, i)),
            pl.BlockSpec((1, nk, KEY_CHUNK, KV_A), lambda b, g, i: (b, 0, 0, 0)),
            pl.BlockSpec((1, nk, VA_ROWS, KEY_CHUNK), lambda b, g, i: (b, 0, g, 0)),
            pl.BlockSpec((META_PAD, KV_A), lambda b, g, i: (0, 0)),
            pl.BlockSpec((VA_ROWS, META_PAD), lambda b, g, i: (g, 0)),
        ],
        out_specs=pl.BlockSpec((1, tq, gw), lambda b, g, i: (b, i, g)),
        out_shape=jax.ShapeDtypeStruct((bsz, n, WIDTH_A), BF16),
        scratch_shapes=[
            pltpu.VMEM((KV_A, GQA_GROUP * tq), BF16),
            pltpu.VMEM((KEY_CHUNK, GQA_GROUP * tq), F32),
            pltpu.VMEM((KEY_CHUNK, GQA_GROUP * tq), F32),
            pltpu.VMEM((GQA_GROUP, 1, tq), F32),
            pltpu.VMEM((GQA_GROUP, VA_ROWS, tq), F32),
        ],
        compiler_params=pltpu.CompilerParams(
            dimension_semantics=("arbitrary", "arbitrary", "arbitrary"), vmem_limit_bytes=VMEM_LIMIT),
        name="attn_a",
    )(kmax, qa_t, ka4, va_t, km, vm_t)


def _attn_b_kernel(cfar_ref, kmax_ref, q_ref, k_ref, v_ref, km_ref, vm_ref, band_ref, mbias_ref,
                   lq1_ref, lk1_ref, lq2_ref, lk2_ref, sn_ref, o_ref,
                   qblk_ref, sa_ref, sb_ref, m_ref, acc_ref, *, tq, nk):
    b = pl.program_id(0)
    h = pl.program_id(1)
    qi = pl.program_id(2)
    vd = 2 * HEAD_DIM
    zero = jnp.zeros((HEAD_DIM, tq), BF16)
    q1 = q_ref[0, 0:HEAD_DIM, :]
    q2 = q_ref[0, HEAD_DIM:vd, :]
    qblk_ref[0:HEAD_DIM, 0:tq] = q1
    qblk_ref[0:HEAD_DIM, tq:2 * tq] = zero
    qblk_ref[HEAD_DIM:vd, 0:tq] = zero
    qblk_ref[HEAD_DIM:vd, tq:2 * tq] = q2
    c_neg = cfar_ref[h, 0]
    c_pos = cfar_ref[h, 1]
    b_max = cfar_ref[h, 2]
    for j, qj in enumerate((q1, q2)):
        m_ref[j] = _query_norm(qj) * (kmax_ref[b, 2 * h + j] * BOUND_MARGIN) + b_max

    def meta_scores():
        return jnp.dot(km_ref[...], qblk_ref[...], preferred_element_type=F32)

    parts = 2 * tq // PART_LANES

    def qk_into(dst, info, p):
        cols = _part(p, tq)[2]
        dst[:, cols] = jnp.dot(k_ref[0, info[0]], qblk_ref[:, cols], preferred_element_type=F32)

    near_lo = jnp.clip(qi * TILE_CHUNKS - 1, 0, nk - NEAR_CHUNKS)

    def far_chunk(i):
        right = i >= near_lo
        return jnp.where(right, i + NEAR_CHUNKS, i), jnp.where(right, c_pos, c_neg)

    def near_chunk(i):
        kc = near_lo + i
        return kc, pl.multiple_of((kc - qi * TILE_CHUNKS + BAND_SHIFT) * KEY_CHUNK, KEY_CHUNK)

    s = meta_scores()
    for j in range(2):
        _bounded_update(s[:, j * tq:(j + 1) * tq] + mbias_ref[0], m_ref[j], acc_ref, vm_ref[...], j, first=True)

    def consume_far(src, info, p):
        kc, c = info
        j, lanes, cols = _part(p, tq)
        _bounded_update(src[:, cols], m_ref[j, :, lanes] - c, acc_ref, v_ref[0, kc],
                        (j, slice(None), lanes))

    def consume_near(src, info, p):
        kc, r0 = info
        j, lanes, cols = _part(p, tq)
        bias = band_ref[0, pl.ds(r0, KEY_CHUNK), lanes]
        _bounded_update(src[:, cols] + bias, m_ref[j, :, lanes], acc_ref, v_ref[0, kc],
                        (j, slice(None), lanes))

    _pipelined_chunks(nk - NEAR_CHUNKS, parts, far_chunk, qk_into, consume_far, sa_ref, sb_ref,
                      unroll=FAST_UNROLL)
    _pipelined_chunks(NEAR_CHUNKS, parts, near_chunk, qk_into, consume_near, sa_ref, sb_ref,
                      unroll=FAST_UNROLL)

    l_min = jnp.minimum(jnp.min(acc_ref[0, vd:vd + 1, :]), jnp.min(acc_ref[1, vd:vd + 1, :]))

    @pl.when(jnp.logical_not(l_min >= L_MIN))
    def _():
        s = meta_scores()
        for j in range(2):
            _init_from_meta(s[:, j * tq:(j + 1) * tq] + mbias_ref[0], m_ref, acc_ref, vm_ref[...], j)

        def exact_far(src, info, p):
            kc, c = info
            j, lanes, cols = _part(p, tq)
            _online_update(src[:, cols], c, m_ref, acc_ref, v_ref[0, kc],
                           (j, slice(None), lanes), (j, slice(None), lanes))

        def exact_near(src, info, p):
            kc, r0 = info
            j, lanes, cols = _part(p, tq)
            bias = band_ref[0, pl.ds(r0, KEY_CHUNK), lanes]
            _online_update(src[:, cols] + bias, None, m_ref, acc_ref, v_ref[0, kc],
                           (j, slice(None), lanes), (j, slice(None), lanes))

        _pipelined_chunks(nk - NEAR_CHUNKS, parts, far_chunk, qk_into, exact_far, sa_ref, sb_ref)
        _pipelined_chunks(NEAR_CHUNKS, parts, near_chunk, qk_into, exact_near, sa_ref, sb_ref)

    lam =(jnp.exp(jnp.sum(lq1_ref[...] * lk1_ref[...], axis=-1, keepdims=True))
           - jnp.exp(jnp.sum(lq2_ref[...] * lk2_ref[...], axis=-1, keepdims=True)) + LAM_INIT)
    o = (acc_ref[0, 0:vd, :] / acc_ref[0, vd:vd + 1, :]
         - lam * (acc_ref[1, 0:vd, :] / acc_ref[1, vd:vd + 1, :]))
    ms = jnp.mean(o * o, axis=0, keepdims=True)
    o = o * lax.rsqrt(ms + EPS) * sn_ref[...] * (1.0 - LAM_INIT)
    o_ref[0] = o.T.astype(BF16)


def _attn_b(cfar, kmax, qb_t, kb, vb_t, kmb, vmb_t, band, mbias, lq1, lk1, lq2, lk2, sn, *, tq):
    bsz, _, n = qb_t.shape
    nk = n // KEY_CHUNK
    assert tq == Q_TILE and NEAR_CHUNKS % 2 == 0 and (nk - NEAR_CHUNKS) % 2 == 0 and nk - NEAR_CHUNKS >= 2
    vd = 2 * HEAD_DIM
    kb4 = kb.reshape(bsz, nk, KEY_CHUNK, WIDTH_B)
    vec = lambda: pl.BlockSpec((1, HEAD_DIM), lambda b, h, i: (0, 0))
    return pl.pallas_call(
        functools.partial(_attn_b_kernel, tq=tq, nk=nk),
        grid=(bsz, N_HEADS_B, n // tq),
        in_specs=[
            pl.BlockSpec(memory_space=pltpu.SMEM),
            pl.BlockSpec(memory_space=pltpu.SMEM),
            pl.BlockSpec((1, vd, tq), lambda b, h, i: (b, h, i)),
            pl.BlockSpec((1, nk, KEY_CHUNK, vd), lambda b, h, i: (b, 0, 0, h)),
            pl.BlockSpec((1, nk, VB_ROWS, KEY_CHUNK), lambda b, h, i: (b, 0, h, 0)),
            pl.BlockSpec((META_PAD, vd), lambda b, h, i: (0, h)),
            pl.BlockSpec((VB_ROWS, META_PAD), lambda b, h, i: (h, 0)),
            pl.BlockSpec((1, BAND_ROWS, tq), lambda b, h, i: (h, 0, 0)),
            pl.BlockSpec((1, META_PAD, tq), lambda b, h, i: (h, 0, i)),
            vec(), vec(), vec(), vec(),
            pl.BlockSpec((vd, 1), lambda b, h, i: (0, 0)),
        ],
        out_specs=pl.BlockSpec((1, tq, vd), lambda b, h, i: (b, i, h)),
        out_shape=jax.ShapeDtypeStruct((bsz, n, WIDTH_B), BF16),
        scratch_shapes=[
            pltpu.VMEM((vd, 2 * tq), BF16),
            pltpu.VMEM((KEY_CHUNK, 2 * tq), F32),
            pltpu.VMEM((KEY_CHUNK, 2 * tq), F32),
            pltpu.VMEM((2, 1, tq), F32),
            pltpu.VMEM((2, VB_ROWS, tq), F32),
        ],
        compiler_params=pltpu.CompilerParams(
            dimension_semantics=("arbitrary", "arbitrary", "arbitrary"), vmem_limit_bytes=VMEM_LIMIT),
        name="attn_b",
    )(cfar, kmax, qb_t, kb4, vb_t, kmb, vmb_t, band, mbias, lq1, lk1, lq2, lk2, sn)


def _outproj_kernel(oa_ref, ob_ref, x_ref, wa_ref, wb_ref, g2_ref, wrh_ref, wrl_ref, br_ref,
                    h_ref, n2_ref, eid_ref, gate_ref):
    att = (jnp.dot(oa_ref[0], wa_ref[...], preferred_element_type=F32)
           + jnp.dot(ob_ref[0], wb_ref[...], preferred_element_type=F32))
    hres = x_ref[0] + att
    h_ref[0] = hres
    ms = jnp.mean(hres * hres, axis=-1, keepdims=True)
    n2 = hres * lax.rsqrt(ms + EPS) * g2_ref[...]
    tm = hres.shape[0]
    n2_hi = n2.astype(BF16)
    n2_lo = (n2 - n2_hi.astype(F32)).astype(BF16)
    for s in range(D_TILES):
        n2_ref[0, pl.ds(s, tm, stride=D_TILES), :] = n2[:, s * LANE:(s + 1) * LANE]
    wrh = wrh_ref[...]
    logits = _nt_dot(wrh, n2_hi) + _nt_dot(wrh, n2_lo) + _nt_dot(wrl_ref[...], n2_hi) + br_ref[...]
    gl = logits[N_EXPERTS:N_EXPERTS + N_GROUPS]
    gmax = jnp.max(gl, axis=0, keepdims=True)
    giota = lax.broadcasted_iota(jnp.int32, (N_GROUPS, tm), 0)
    g_idx = jnp.min(jnp.where(gl == gmax, giota, N_GROUPS), axis=0, keepdims=True)
    g_w = 1.0 / jnp.sum(jnp.exp(gl - gmax), axis=0, keepdims=True)
    sel = jnp.zeros((EXPERTS_PER_GROUP, tm), F32)
    for g in range(N_GROUPS):
        sel = jnp.where(g_idx == g, logits[g * EXPERTS_PER_GROUP:(g + 1) * EXPERTS_PER_GROUP], sel)
    eiota = lax.broadcasted_iota(jnp.int32, (EXPERTS_PER_GROUP, tm), 0)
    e1 = jnp.max(sel, axis=0, keepdims=True)
    i1 = jnp.min(jnp.where(sel == e1, eiota, EXPERTS_PER_GROUP), axis=0, keepdims=True)
    sel2 = jnp.where(eiota == i1, -jnp.inf, sel)
    e2 = jnp.max(sel2, axis=0, keepdims=True)
    i2 = jnp.min(jnp.where(sel2 == e2, eiota, EXPERTS_PER_GROUP), axis=0, keepdims=True)
    r = jnp.exp(e2 - e1)
    w1 = g_w / (1.0 + r)
    eid_ref[0] = jnp.concatenate([g_idx * EXPERTS_PER_GROUP + i1, g_idx * EXPERTS_PER_GROUP + i2], axis=0)
    gate_ref[0] = jnp.concatenate([w1, w1 * r], axis=0)


def _outproj(oa, ob, x, wa, wb, g2, wrh, wrl, br, *, tm):
    bsz, n, d = x.shape
    full = lambda shape: pl.BlockSpec(shape, lambda b, i: (0,) * len(shape))
    tok = lambda w: pl.BlockSpec((1, tm, w), lambda b, i: (b, i, 0))
    lane = lambda: pl.BlockSpec((1, TOP_K, tm), lambda b, i: (b, 0, i))
    return pl.pallas_call(
        _outproj_kernel,
        grid=(bsz, n // tm),
        in_specs=[tok(WIDTH_A), tok(WIDTH_B), tok(d), full((WIDTH_A, d)), full((WIDTH_B, d)), full((1, d)),
                  full((ROUTER_ROWS, d)), full((ROUTER_ROWS, d)), full((ROUTER_ROWS, 1))],
        out_specs=(tok(d), pl.BlockSpec((1, tm * D_TILES, LANE), lambda b, i: (b, i, 0)), lane(), lane()),
        out_shape=(
            jax.ShapeDtypeStruct((bsz, n, d), F32),
            jax.ShapeDtypeStruct((bsz, n * D_TILES, LANE), F32),
            jax.ShapeDtypeStruct((bsz, TOP_K, n), jnp.int32),
            jax.ShapeDtypeStruct((bsz, TOP_K, n), F32),
        ),
        compiler_params=pltpu.CompilerParams(
            dimension_semantics=("arbitrary", "arbitrary"), vmem_limit_bytes=VMEM_LIMIT),
        name="outproj_router",
    )(oa, ob, x, wa, wb, g2, wrh, wrl, br)


def _expert_kernel(be_ref, nb_ref, cur_ref, nxt_ref, x_hbm, wg_ref, wu_ref, wd_ref, y_ref, xbuf, sems, *, rows):
    i = pl.program_id(0)
    n_steps = pl.num_programs(0)
    n_used = nb_ref[0]
    slot = lax.rem(i, 2)

    def row_copy(tile_row, slot_, r):
        src = x_hbm.at[pl.ds(pl.multiple_of(tile_row, D_TILES), D_TILES), :]
        return pltpu.make_async_copy(src, xbuf.at[slot_, pl.ds(r * D_TILES, D_TILES), :], sems.at[slot_])

    def start_rows(idx_ref, slot_):
        for r in range(rows):
            row_copy(idx_ref[0, 0, r], slot_, r).start(priority=r % 2)

    def wait_rows(slot_):
        for r in range(rows):
            row_copy(0, slot_, r).wait()

    @pl.when(i == 0)
    def _():
        start_rows(cur_ref, 0)

    @pl.when(i < n_used)
    def _():
        wait_rows(slot)
        x = jnp.concatenate([xbuf[slot, pl.ds(s, rows, stride=D_TILES), :] for s in range(D_TILES)],
                            axis=1).astype(BF16)
        gt = jnp.dot(x, wg_ref[0], preferred_element_type=F32)
        up = jnp.dot(x, wu_ref[0], preferred_element_type=F32)
        act = gt * (1.0 / (1.0 + jnp.exp(-gt))) * up
        y = jnp.dot(act.astype(BF16), wd_ref[0], preferred_element_type=F32).astype(BF16)
        start_rows(nxt_ref, 1 - slot)
        y_ref[...] = y

    @pl.when(i >= n_used)
    def _():
        y_ref[...] = jnp.zeros_like(y_ref)

    @pl.when(i == n_used)
    def _():
        wait_rows(slot)

    @pl.when(jnp.logical_and(i == n_steps - 1, i < n_used))
    def _():
        wait_rows(1 - slot)


def _experts(block_e, n_used, row_tok, x, wg, wu, wd, *, rows):
    d = D_MODEL
    n_blocks = block_e.shape[0]
    idx = (row_tok * D_TILES).reshape(n_blocks, 1, rows)
    smem_rows = lambda shift: pl.BlockSpec(
        (1, 1, rows), lambda i, be, nb: (jnp.minimum(i + shift, n_blocks - 1), 0, 0), memory_space=pltpu.SMEM)
    grid_spec = pltpu.PrefetchScalarGridSpec(
        num_scalar_prefetch=2,
        grid=(n_blocks,),
        in_specs=[
            smem_rows(0),
            smem_rows(1),
            pl.BlockSpec(memory_space=pl.ANY),
            pl.BlockSpec((1, d, D_EXPERT), lambda i, be, nb: (be[i], 0, 0)),
            pl.BlockSpec((1, d, D_EXPERT), lambda i, be, nb: (be[i], 0, 0)),
            pl.BlockSpec((1, D_EXPERT, d), lambda i, be, nb: (be[i], 0, 0)),
        ],
        out_specs=pl.BlockSpec((rows, d), lambda i, be, nb: (i, 0)),
        scratch_shapes=[pltpu.VMEM((2, rows * D_TILES, LANE), F32), pltpu.SemaphoreType.DMA((2,))],
    )
    return pl.pallas_call(
        functools.partial(_expert_kernel, rows=rows),
        grid_spec=grid_spec,
        out_shape=jax.ShapeDtypeStruct((n_blocks * rows, d), BF16),
        compiler_params=pltpu.CompilerParams(
            dimension_semantics=("arbitrary",), vmem_limit_bytes=VMEM_LIMIT),
        name="experts",
    )(block_e, n_used, idx, idx, x, wg, wu, wd)


def _final_kernel(h_ref, y0_ref, y1_ref, gate_ref, gf_ref, o_ref):
    gates = gate_ref[...]
    hres = (h_ref[...] + gates[:, 0:1] * y0_ref[...].astype(F32)
            + gates[:, 1:2] * y1_ref[...].astype(F32))
    ms = jnp.mean(hres * hres, axis=-1, keepdims=True)
    o_ref[...] = hres * lax.rsqrt(ms + EPS) * gf_ref[...]


def _final(h, y0, y1, gates, gf, *, tm):
    t, d = h.shape
    tok = lambda w: pl.BlockSpec((tm, w), lambda i: (i, 0))
    return pl.pallas_call(
        _final_kernel,
        grid=(t // tm,),
        in_specs=[tok(d), tok(d), tok(d), tok(TOP_K), pl.BlockSpec((1, d), lambda i: (0, 0))],
        out_specs=tok(d),
        out_shape=jax.ShapeDtypeStruct((t, d), F32),
        compiler_params=pltpu.CompilerParams(
            dimension_semantics=("arbitrary",), vmem_limit_bytes=VMEM_LIMIT),
        name="combine_final_norm",
    )(h, y0, y1, gates, gf)


def _rope_tables(n_tok):
    rows = n_tok // GRID_W
    row = jnp.repeat(jnp.arange(rows, dtype=F32), GRID_W)
    col = jnp.tile(jnp.arange(GRID_W, dtype=F32), rows)
    inv = ROPE_THETA ** (-jnp.arange(0, AXIS_DIM, 2, dtype=F32) / AXIS_DIM)
    ar = row[:, None] * inv
    ac = col[:, None] * inv
    ang = jnp.concatenate([ar, ar, ac, ac], axis=-1)
    q = HEAD_DIM // 4
    sign = jnp.concatenate([-jnp.ones((q,), F32), jnp.ones((q,), F32)] * 2)
    return jnp.cos(ang).T, (jnp.sin(ang) * sign).T


def _t5_bucket(rp):
    nb = N_BUCKETS // 2
    max_exact = nb // 2
    ret = (rp > 0).astype(jnp.int32) * nb
    n = jnp.abs(rp)
    large = max_exact + (jnp.log(jnp.maximum(n, 1).astype(F32) / max_exact)
                         / math.log(MAX_DISTANCE / max_exact) * (nb - max_exact)).astype(jnp.int32)
    large = jnp.minimum(large, nb - 1)
    return ret + jnp.where(n < max_exact, n, large)


def _bias_lookup(rel_bias, d):
    return (rel_bias.astype(F32) * LOG2E)[_t5_bucket(d)].T


def _band_tables(rel_bias, tq):
    assert MAX_DISTANCE <= LANE
    far = jnp.array([-2 * MAX_DISTANCE, 2 * MAX_DISTANCE], jnp.int32)
    c_far = _bias_lookup(rel_bias, far)
    rev = _bias_lookup(rel_bias, 2 * LANE - 1 - jnp.arange(4 * LANE - 1, dtype=jnp.int32))
    strip = jnp.stack([rev[:, LANE - 1 - rho:4 * LANE - 1 - rho] for rho in range(LANE)], axis=1)
    const = lambda col: jnp.broadcast_to(c_far[:, col, None, None], (N_HEADS_B, LANE, LANE))

    def block(k):
        if k <= -2 or k >= 2:
            return const(0 if k < 0 else 1)
        return strip[:, :, (1 - k) * LANE:(2 - k) * LANE]

    shift_blocks = BAND_SHIFT * KEY_CHUNK // LANE
    band = jnp.concatenate(
        [jnp.concatenate([block(a - b - shift_blocks) for b in range(tq // LANE)], axis=2)
         for a in range(BAND_ROWS // LANE)], axis=1)
    b_max = jnp.max(rel_bias.astype(F32) * LOG2E, axis=0)[:, None]
    return band, jnp.concatenate([c_far, b_max], axis=1)


def _meta_bias(rel_bias, n_tok):
    v = _bias_lookup(rel_bias, -jnp.arange(n_tok + N_META + 1, dtype=jnp.int32))
    mb = jnp.stack([v[:, N_META - j:N_META - j + n_tok] for j in range(N_META)], axis=1)
    pad = jnp.full((N_HEADS_B, META_PAD - N_META, n_tok), MASK_VALUE, F32)
    return jnp.concatenate([mb, pad], axis=1)


def _dispatch_plan(eid, rows):
    t = eid.shape[1]
    n_assign = TOP_K * t
    flat = eid.reshape(-1)
    a_idx = jnp.arange(n_assign, dtype=jnp.int32)
    skey = jnp.sort(flat * n_assign + a_idx)
    s_assign = skey % n_assign
    experts = jnp.arange(N_EXPERTS, dtype=jnp.int32)
    bounds = jnp.arange(N_EXPERTS + 1, dtype=jnp.int32) * n_assign
    off_all = jnp.sum((skey[:, None] < bounds[None, :]).astype(jnp.int32), axis=0)
    off = off_all[:-1]
    counts = off_all[1:] - off
    padded = (counts + rows - 1) // rows * rows
    pend = jnp.cumsum(padded)
    poff = pend - padded
    step = jnp.diff(poff - off, prepend=0)
    dest = a_idx + jnp.sum(jnp.where(a_idx[:, None] >= off[None, :], step[None, :], 0), axis=1)
    _, pos = lax.sort((s_assign, dest), num_keys=1)
    n_blocks = (n_assign + N_EXPERTS * (rows - 1) + rows - 1) // rows
    cap = n_blocks * rows
    block_start = jnp.arange(n_blocks, dtype=jnp.int32) * rows
    block_e = jnp.minimum(jnp.sum((block_start[:, None] >= pend[None, :]).astype(jnp.int32), axis=1),
                          N_EXPERTS - 1)
    d = jnp.arange(cap, dtype=jnp.int32)
    e_d = jnp.repeat(block_e, rows)
    onehot_d = e_d[:, None] == experts[None, :]
    pick = lambda v: jnp.sum(jnp.where(onehot_d, v[None, :], 0), axis=1)
    j = d - pick(poff)
    valid = j < pick(counts)
    src = jnp.clip(pick(off) + j, 0, n_assign - 1)
    row_tok = jnp.where(valid, s_assign[src] % t, 0)
    n_used = (pend[-1] // rows).astype(jnp.int32).reshape(1)
    return row_tok, block_e, n_used, pos.reshape(TOP_K, t)


def _key_bounds(knorm, knorm_meta):
    both = jnp.maximum(jnp.max(knorm, axis=1), knorm_meta[0, 0][None])
    return jnp.sqrt(both[:, 1:1 + N_KV_A, 0]), jnp.sqrt(both[:, 0, 0:2 * N_HEADS_B])


def _mixer(x, shared):
    bsz, n_tok, d = x.shape
    assert n_tok % Q_TILE == 0 and n_tok % TOKEN_TILE == 0
    tq = Q_TILE
    tm = TOKEN_TILE
    cos_t, sin_t = _rope_tables(n_tok)
    qa_t, ka, va_t, qb_t, kb, vb_t, knorm = _inproj(
        x, shared['g1'], shared['wt'], shared['wkb'], cos_t, sin_t, shared['qn'], shared['kn'],
        tm=tm, chunk=KEY_CHUNK)
    kmax_a, kmax_b = _key_bounds(knorm, shared['knorm_meta'])
    oa = _attn_a(kmax_a, qa_t, ka, va_t, shared['km_a'], shared['vm_a'], tq=tq)
    mbias = _meta_bias(shared['rel_bias'], n_tok)
    ob = _attn_b(shared['cfar'], kmax_b, qb_t, kb, vb_t, shared['km_b'], shared['vm_b'], shared['band'], mbias,
                 shared['lq1'], shared['lk1'], shared['lq2'], shared['lk2'], shared['sn'], tq=tq)
    h, n2, eid, gates = _outproj(oa, ob, x, shared['wa'], shared['wb'], shared['g2'],
                                 shared['wrh'], shared['wrl'], shared['br'], tm=tm)
    t = bsz * n_tok
    eid = jnp.transpose(eid, (1, 0, 2)).reshape(TOP_K, t)
    gates = jnp.transpose(gates, (1, 0, 2)).reshape(TOP_K, t)
    row_tok, block_e, n_used, pos = _dispatch_plan(eid, MOE_ROWS)
    return h.reshape(t, d), n2.reshape(t * D_TILES, LANE), row_tok, block_e, n_used, pos, gates.T


def _moe_and_norm(mixed, shape, shared):
    h, n2, row_tok, block_e, n_used, pos, gates = mixed
    y = _experts(block_e, n_used, row_tok, n2, shared['wg'], shared['wu'], shared['wd'], rows=MOE_ROWS)
    y0 = jnp.take(y, pos[0], axis=0)
    y1 = jnp.take(y, pos[1], axis=0)
    return _final(h, y0, y1, gates, shared['gf'], tm=TOKEN_TILE).reshape(shape)


def kernel(x_prompt, x_sample, meta_tokens, rel_bias, norm1, w_in, q_norm, k_norm, lambda_q1, lambda_k1,
           lambda_q2, lambda_k2, sub_norm, w_out, norm2, w_router_group, b_router_group, w_router_expert,
           b_router_expert, w_gate, w_up, w_down, final_norm):
    i = LAYER
    d = D_MODEL
    o0 = WIDTH_A
    o1 = o0 + KV_A
    o2 = o1 + KV_A
    o3 = o2 + WIDTH_B
    o4 = o3 + WIDTH_B
    w = w_in[i]
    wt = jnp.concatenate([w[:, :o2], w[:, o2:o3], w[:, o4:]], axis=1).T.astype(BF16)
    wr = jnp.concatenate([w_router_expert[i], w_router_group[i],
                          jnp.zeros((d, ROUTER_ROWS - N_EXPERTS - N_GROUPS), F32)], axis=1).T
    wrh = wr.astype(BF16)
    br = jnp.concatenate([b_router_expert[i], b_router_group[i],
                          jnp.zeros((ROUTER_ROWS - N_EXPERTS - N_GROUPS,), F32)]).reshape(ROUTER_ROWS, 1)
    shared = {
        'g1': norm1[i].reshape(1, d), 'wt': wt, 'wkb': w[:, o3:o4].astype(BF16),
        'qn': q_norm[i].reshape(HEAD_DIM, 1), 'kn': k_norm[i].reshape(HEAD_DIM, 1),
        'rel_bias': rel_bias,
        'lq1': lambda_q1[i].reshape(1, HEAD_DIM), 'lk1': lambda_k1[i].reshape(1, HEAD_DIM),
        'lq2': lambda_q2[i].reshape(1, HEAD_DIM), 'lk2': lambda_k2[i].reshape(1, HEAD_DIM),
        'sn': sub_norm[i].reshape(2 * HEAD_DIM, 1),
        'wa': w_out[i][:WIDTH_A].astype(BF16), 'wb': w_out[i][WIDTH_A:].astype(BF16),
        'g2': norm2[i].reshape(1, d), 'wrh': wrh, 'wrl': (wr - wrh.astype(F32)).astype(BF16), 'br': br,
        'wg': w_gate[i].astype(BF16), 'wu': w_up[i].astype(BF16), 'wd': w_down[i].astype(BF16),
        'gf': final_norm.reshape(1, d),
    }
    xm = jnp.concatenate([meta_tokens.astype(F32), jnp.zeros((META_PAD - N_META, d), F32)], axis=0)[None]
    ones = jnp.ones((HEAD_DIM, META_PAD), F32)
    _, km_a, vm_a, _, km_b, vm_b, knorm_meta = _inproj(
        xm, shared['g1'], shared['wt'], shared['wkb'], ones, 0.0 * ones, shared['qn'], shared['kn'],
        tm=META_PAD, chunk=META_PAD)
    shared['km_a'] = km_a[0]
    shared['vm_a'] = vm_a[0, 0]
    shared['km_b'] = km_b[0]
    shared['vm_b'] = vm_b[0, 0]
    shared['knorm_meta'] = knorm_meta
    shared['band'], shared['cfar'] = _band_tables(rel_bias, Q_TILE)
    mixed_prompt = _mixer(x_prompt, shared)
    mixed_sample = _mixer(x_sample, shared)
    return (_moe_and_norm(mixed_prompt, x_prompt.shape, shared),
            _moe_and_norm(mixed_sample, x_sample.shape, shared))
```

```python
import functools
import math

import jax
import jax.numpy as jnp
from jax import lax
from jax.experimental import pallas as pl
from jax.experimental.pallas import tpu as pltpu

D_MODEL = 1024
HEAD_DIM = 64
N_HEADS_A = 8
N_KV_A = 2
GQA_GROUP = N_HEADS_A // N_KV_A
N_HEADS_B = 4
AXIS_DIM = HEAD_DIM // 2
ROPE_THETA = 10000.0
GRID_W = 64
N_META = 16
N_BUCKETS = 32
MAX_DISTANCE = 128
WIDTH_A = N_HEADS_A * HEAD_DIM
WIDTH_B = N_HEADS_B * 2 * HEAD_DIM
KV_A = N_KV_A * HEAD_DIM
N_GROUPS = 4
EXPERTS_PER_GROUP = 8
N_EXPERTS = N_GROUPS * EXPERTS_PER_GROUP
TOP_K = 2
D_EXPERT = D_MODEL // 2
EPS = 1e-6
LAYER = 0
LAM_INIT = 0.8 - 0.6 * math.exp(-0.3 * LAYER)

LOG2E = 1.4426950408889634
Q_SCALE = (HEAD_DIM ** -0.5) * LOG2E
MASK_VALUE = -1e30

LANE = 128
META_PAD = LANE
KEY_CHUNK = 256
TOKEN_TILE = 512
Q_TILE = 512
MOE_ROWS = 512
ROUTER_ROWS = 40
VMEM_LIMIT = 48 * 1024 * 1024

VA_ROWS = HEAD_DIM
VB_ROWS = 2 * HEAD_DIM
SUBLANES = 8
TILE_CHUNKS = Q_TILE // KEY_CHUNK
NEAR_CHUNKS = TILE_CHUNKS + 2
BAND_SHIFT = 2
BAND_ROWS = (TILE_CHUNKS + 5) * KEY_CHUNK

BOUND_MARGIN = 1.01
L_MIN = 2.0 ** -64
D_TILES = D_MODEL // LANE
NORM_ROWS = 8
PART_LANES = 512
FAST_UNROLL = 4

F32 = jnp.float32
BF16 = jnp.bfloat16


def _nt_dot(a, b):
    return lax.dot_general(a, b, (((1,), (1,)), ((), ())), preferred_element_type=F32)


def _inproj_kernel(x_ref, g1_ref, wt_ref, wkb_ref, cos_ref, sin_ref, qn_ref, kn_ref,
                   qa_ref, ka_ref, va_ref, qb_ref, kb_ref, vb_ref, knorm_ref, *, chunk):
    x = x_ref[0]
    ms = jnp.mean(x * x, axis=-1, keepdims=True)
    n = (x * lax.rsqrt(ms + EPS) * g1_ref[...]).astype(BF16)
    pt = _nt_dot(wt_ref[...], n)
    kb = jnp.dot(n, wkb_ref[...], preferred_element_type=F32).astype(BF16)
    kb_ref[0] = kb
    kbf = kb.astype(F32)
    seg = (lax.broadcasted_iota(jnp.int32, (WIDTH_B, LANE), 0) // HEAD_DIM
           == lax.broadcasted_iota(jnp.int32, (WIDTH_B, LANE), 1)).astype(BF16)
    kb_sq = jnp.dot((kbf * kbf).astype(BF16), seg, preferred_element_type=F32)
    kb_max = jnp.max(kb_sq, axis=0, keepdims=True)
    cos = cos_ref[...]
    sin = sin_ref[...]
    q = HEAD_DIM // 4

    def norm_rope(blk, gain):
        ss = jnp.mean(blk * blk, axis=0, keepdims=True)
        y = blk * lax.rsqrt(ss + EPS) * gain
        swapped = jnp.concatenate([y[q:2 * q], y[0:q], y[3 * q:4 * q], y[2 * q:3 * q]], axis=0)
        return y * cos + swapped * sin

    qn = qn_ref[...]
    kn = kn_ref[...]
    for h in range(N_HEADS_A):
        blk = pt[h * HEAD_DIM:(h + 1) * HEAD_DIM]
        qa_ref[0, h * HEAD_DIM:(h + 1) * HEAD_DIM, :] = (norm_rope(blk, qn) * Q_SCALE).astype(BF16)
    o0 = WIDTH_A
    ka = jnp.concatenate(
        [norm_rope(pt[o0 + j * HEAD_DIM:o0 + (j + 1) * HEAD_DIM], kn) for j in range(N_KV_A)], axis=0)
    ka_ref[0] = ka.T.astype(BF16)
    kaf = ka.astype(BF16).astype(F32)
    ka_sq = kaf * kaf
    ka_max = [jnp.max(jnp.sum(ka_sq[g * HEAD_DIM:(g + 1) * HEAD_DIM], axis=0, keepdims=True),
                      axis=1, keepdims=True) for g in range(N_KV_A)]
    knorm_ref[0, 0] = jnp.concatenate(
        [kb_max] + [jnp.broadcast_to(v, (1, LANE)) for v in ka_max]
        + [jnp.zeros((NORM_ROWS - 1 - N_KV_A, LANE), F32)], axis=0)
    o1 = o0 + KV_A
    o2 = o1 + KV_A
    o3 = o2 + WIDTH_B
    tm = x.shape[0]
    vd = 2 * HEAD_DIM
    for j in range(tm // chunk):
        cols = slice(j * chunk, (j + 1) * chunk)
        for g in range(N_KV_A):
            va_ref[0, j, g * VA_ROWS:(g + 1) * VA_ROWS] = (
                pt[o1 + g * HEAD_DIM:o1 + (g + 1) * HEAD_DIM, cols].astype(BF16))
        for h in range(N_HEADS_B):
            vb_ref[0, j, h * VB_ROWS:(h + 1) * VB_ROWS] = pt[o3 + h * vd:o3 + (h + 1) * vd, cols].astype(BF16)
    qb_ref[0] = (pt[o2:o3] * Q_SCALE).astype(BF16)


def _inproj(x, g1, wt, wkb, cos_t, sin_t, qn, kn, *, tm, chunk):
    bsz, n, d = x.shape
    rows_t = wt.shape[0]
    grid = (bsz, n // tm)
    full = lambda shape: pl.BlockSpec(shape, lambda b, i: (0,) * len(shape))
    va_rows = N_KV_A * VA_ROWS
    vb_rows = N_HEADS_B * VB_ROWS
    out_shape = (
        jax.ShapeDtypeStruct((bsz, WIDTH_A, n), BF16),
        jax.ShapeDtypeStruct((bsz, n, KV_A), BF16),
        jax.ShapeDtypeStruct((bsz, n // chunk, va_rows, chunk), BF16),
        jax.ShapeDtypeStruct((bsz, WIDTH_B, n), BF16),
        jax.ShapeDtypeStruct((bsz, n, WIDTH_B), BF16),
        jax.ShapeDtypeStruct((bsz, n // chunk, vb_rows, chunk), BF16),
        jax.ShapeDtypeStruct((bsz, n // tm, NORM_ROWS, LANE), F32),
    )
    return pl.pallas_call(
        functools.partial(_inproj_kernel, chunk=chunk),
        grid=grid,
        in_specs=[
            pl.BlockSpec((1, tm, d), lambda b, i: (b, i, 0)),
            full((1, d)),
            full((rows_t, d)),
            full((d, WIDTH_B)),
            pl.BlockSpec((HEAD_DIM, tm), lambda b, i: (0, i)),
            pl.BlockSpec((HEAD_DIM, tm), lambda b, i: (0, i)),
            full((HEAD_DIM, 1)),
            full((HEAD_DIM, 1)),
        ],
        out_specs=(
            pl.BlockSpec((1, WIDTH_A, tm), lambda b, i: (b, 0, i)),
            pl.BlockSpec((1, tm, KV_A), lambda b, i: (b, i, 0)),
            pl.BlockSpec((1, tm // chunk, va_rows, chunk), lambda b, i: (b, i, 0, 0)),
            pl.BlockSpec((1, WIDTH_B, tm), lambda b, i: (b, 0, i)),
            pl.BlockSpec((1, tm, WIDTH_B), lambda b, i: (b, i, 0)),
            pl.BlockSpec((1, tm // chunk, vb_rows, chunk), lambda b, i: (b, i, 0, 0)),
            pl.BlockSpec((1, 1, NORM_ROWS, LANE), lambda b, i: (b, i, 0, 0)),
        ),
        out_shape=out_shape,
        compiler_params=pltpu.CompilerParams(
            dimension_semantics=("arbitrary", "arbitrary"), vmem_limit_bytes=VMEM_LIMIT),
        name="inproj",
    )(x, g1, wt, wkb, cos_t, sin_t, qn, kn)


def _partial_sums(p):
    return jnp.sum(p.reshape(p.shape[0] // SUBLANES, SUBLANES, p.shape[1]), axis=0)


def _online_update(s, c_shift, m_ref, l_ref, acc_ref, vblk, idx):
    mc = jnp.max(s, axis=0, keepdims=True)
    if c_shift is not None:
        mc = mc + c_shift
    m_old = m_ref[idx]
    m_new = jnp.maximum(m_old, mc)
    alpha = jnp.exp2(m_old - m_new)
    shift = m_new if c_shift is None else m_new - c_shift
    p = jnp.exp2(s - shift)
    l_ref[idx] = alpha * l_ref[idx] + _partial_sums(p)
    pv = jnp.dot(vblk, p.astype(BF16), preferred_element_type=F32)
    acc_ref[idx] = alpha * acc_ref[idx] + pv
    m_ref[idx] = m_new


def _init_from_meta(s, m_ref, l_ref, acc_ref, vmeta, idx):
    m0 = jnp.max(s, axis=0, keepdims=True)
    p = jnp.exp2(s - m0)
    m_ref[idx] = m0
    l_ref[idx] = _partial_sums(p)
    acc_ref[idx] = jnp.dot(vmeta, p.astype(BF16), preferred_element_type=F32)


def _bounded_update(s, shift, l_ref, acc_ref, vblk, idx, first=False):
    p = jnp.exp2(s - shift)
    lp = _partial_sums(p)
    pv = jnp.dot(vblk, p.astype(BF16), preferred_element_type=F32)
    l_ref[idx] = lp if first else l_ref[idx] + lp
    acc_ref[idx] = pv if first else acc_ref[idx] + pv


def _part(p, tq):
    c0 = p * PART_LANES
    return c0 // tq, slice(c0 % tq, c0 % tq + PART_LANES), slice(c0, c0 + PART_LANES)


def _query_norm(q_bf16):
    qf = q_bf16.astype(F32)
    return jnp.sqrt(jnp.sum(qf * qf, axis=0, keepdims=True))


def _pipelined_chunks(count, parts, chunk_of, qk_into, consume, sa_ref, sb_ref, unroll=2, consume_tail=None):
    assert unroll % 2 == 0 and count >= unroll and count % unroll == 0
    bufs = (sa_ref, sb_ref)

    def step(u, nxt, cur, use=consume):
        for part in range(parts):
            if nxt is not None:
                qk_into(bufs[(u + 1) % 2], nxt, part)
            use(bufs[u % 2], cur, part)

    for part in range(parts):
        qk_into(sa_ref, chunk_of(0), part)

    def body(j, carry):
        i = unroll * j
        for u in range(unroll):
            step(u, chunk_of(i + u + 1), chunk_of(i + u))
        return carry

    lax.fori_loop(0, count // unroll - 1, body, 0)
    i = count - unroll
    for u in range(unroll):
        step(u, chunk_of(i + u + 1) if u + 1 < unroll else None, chunk_of(i + u), consume_tail or consume)


def _attn_a_kernel(kmax_ref, q_ref, k_ref, v_ref, km_ref, vm_ref, o_ref, qpad_ref, sa_ref, sb_ref, m_ref,
                   l_ref, acc_ref, *, tq, nk):
    b = pl.program_id(0)
    g = pl.program_id(1)
    kmax = kmax_ref[b, g] * BOUND_MARGIN
    zero = jnp.zeros((HEAD_DIM, tq), BF16)
    for h in range(GQA_GROUP):
        qh = q_ref[0, h * HEAD_DIM:(h + 1) * HEAD_DIM, :]
        qpad_ref[0:HEAD_DIM, h * tq:(h + 1) * tq] = jnp.where(g == 0, qh, zero)
        qpad_ref[HEAD_DIM:2 * HEAD_DIM, h * tq:(h + 1) * tq] = jnp.where(g == 1, qh, zero)
        m_ref[h] = _query_norm(qh) * kmax

    valid = lax.broadcasted_iota(jnp.int32, (META_PAD, 1), 0) < N_META

    def meta_scores():
        s = jnp.dot(km_ref[...], qpad_ref[...], preferred_element_type=F32)
        return jnp.where(valid, s, MASK_VALUE)

    parts = GQA_GROUP * tq // PART_LANES

    def qk_into(dst, kc, p):
        cols = _part(p, tq)[2]
        dst[:, cols] = jnp.dot(k_ref[0, kc], qpad_ref[:, cols], preferred_element_type=F32)

    s = meta_scores()
    for h in range(GQA_GROUP):
        _bounded_update(s[:, h * tq:(h + 1) * tq], m_ref[h], l_ref, acc_ref, vm_ref[...], h, first=True)

    def consume(src, kc, p):
        h, lanes, cols = _part(p, tq)
        _bounded_update(src[:, cols], m_ref[h, :, lanes], l_ref, acc_ref, v_ref[0, kc], (h, slice(None), lanes))

    _pipelined_chunks(nk, parts, lambda i: i, qk_into, consume, sa_ref, sb_ref, unroll=FAST_UNROLL)

    denom = lambda h: jnp.sum(l_ref[h], axis=0, keepdims=True)
    l_min = jnp.min(denom(0))
    for h in range(1, GQA_GROUP):
        l_min = jnp.minimum(l_min, jnp.min(denom(h)))

    @pl.when(jnp.logical_not(l_min >= L_MIN))
    def _():
        s = meta_scores()
        for h in range(GQA_GROUP):
            _init_from_meta(s[:, h * tq:(h + 1) * tq], m_ref, l_ref, acc_ref, vm_ref[...], h)

        def consume_exact(src, kc, p):
            h, lanes, cols = _part(p, tq)
            _online_update(src[:, cols], None, m_ref, l_ref, acc_ref, v_ref[0, kc], (h, slice(None), lanes))

        _pipelined_chunks(nk, parts, lambda i: i, qk_into, consume_exact, sa_ref, sb_ref)

    o = jnp.concatenate([acc_ref[h] / denom(h) for h in range(GQA_GROUP)], axis=0)
    o_ref[0] = o.T.astype(BF16)


def _attn_a(kmax, qa_t, ka, va_t, km, vm_t, *, tq):
    bsz, _, n = qa_t.shape
    nk = n // KEY_CHUNK
    ka4 = ka.reshape(bsz, nk, KEY_CHUNK, KV_A)
    gw = GQA_GROUP * HEAD_DIM
    return pl.pallas_call(
        functools.partial(_attn_a_kernel, tq=tq, nk=nk),
        grid=(bsz, N_KV_A, n // tq),
        in_specs=[
            pl.BlockSpec(memory_space=pltpu.SMEM),
            pl.BlockSpec((1, gw, tq), lambda b, g, i: (b, g, i)),
            pl.BlockSpec((1, nk, KEY_CHUNK, KV_A), lambda b, g, i: (b, 0, 0, 0)),
            pl.BlockSpec((1, nk, VA_ROWS, KEY_CHUNK), lambda b, g, i: (b, 0, g, 0)),
            pl.BlockSpec((META_PAD, KV_A), lambda b, g, i: (0, 0)),
            pl.BlockSpec((VA_ROWS, META_PAD), lambda b, g, i: (g, 0)),
        ],
        out_specs=pl.BlockSpec((1, tq, gw), lambda b, g, i: (b, i, g)),
        out_shape=jax.ShapeDtypeStruct((bsz, n, WIDTH_A), BF16),
        scratch_shapes=[
            pltpu.VMEM((KV_A, GQA_GROUP * tq), BF16),
            pltpu.VMEM((KEY_CHUNK, GQA_GROUP * tq), F32),
            pltpu.VMEM((KEY_CHUNK, GQA_GROUP * tq), F32),
            pltpu.VMEM((GQA_GROUP, 1, tq), F32),
            pltpu.VMEM((GQA_GROUP, SUBLANES, tq), F32),
            pltpu.VMEM((GQA_GROUP, VA_ROWS, tq), F32),
        ],
        compiler_params=pltpu.CompilerParams(
            dimension_semantics=("arbitrary", "arbitrary", "arbitrary"), vmem_limit_bytes=VMEM_LIMIT),
        name="attn_a",
    )(kmax, qa_t, ka4, va_t, km, vm_t)


def _attn_b_kernel(cfar_ref, kmax_ref, q_ref, k_ref, v_ref, km_ref, vm_ref, band_ref, mbias_ref,
                   lq1_ref, lk1_ref, lq2_ref, lk2_ref, sn_ref, o_ref,
                   qblk_ref, sa_ref, sb_ref, m_ref, l_ref, acc_ref, *, tq, nk):
    b = pl.program_id(0)
    h = pl.program_id(1)
    qi = pl.program_id(2)
    vd = 2 * HEAD_DIM
    zero = jnp.zeros((HEAD_DIM, tq), BF16)
    q1 = q_ref[0, 0:HEAD_DIM, :]
    q2 = q_ref[0, HEAD_DIM:vd, :]
    qblk_ref[0:HEAD_DIM, 0:tq] = q1
    qblk_ref[0:HEAD_DIM, tq:2 * tq] = zero
    qblk_ref[HEAD_DIM:vd, 0:tq] = zero
    qblk_ref[HEAD_DIM:vd, tq:2 * tq] = q2
    c_neg = cfar_ref[h, 0]
    c_pos = cfar_ref[h, 1]
    b_max = cfar_ref[h, 2]
    for j, qj in enumerate((q1, q2)):
        m_ref[j] = _query_norm(qj) * (kmax_ref[b, 2 * h + j] * BOUND_MARGIN) + b_max

    def meta_scores():
        return jnp.dot(km_ref[...], qblk_ref[...], preferred_element_type=F32)

    parts = 2 * tq // PART_LANES

    def qk_into(dst, info, p):
        cols = _part(p, tq)[2]
        dst[:, cols] = jnp.dot(k_ref[0, info[0]], qblk_ref[:, cols], preferred_element_type=F32)

    n_far = nk - NEAR_CHUNKS
    near_lo = jnp.clip(qi * TILE_CHUNKS - 1, 0, n_far)

    def chunk_of(i):
        right = i >= near_lo
        kc_far = jnp.where(right, i + NEAR_CHUNKS, i)
        kc = jnp.where(i >= n_far, near_lo + (i - n_far), kc_far)
        r0 = pl.multiple_of(jnp.clip(kc - qi * TILE_CHUNKS + BAND_SHIFT, 0, BAND_ROWS // KEY_CHUNK - 1)
                            * KEY_CHUNK, KEY_CHUNK)
        return kc, jnp.where(right, c_pos, c_neg), r0

    s = meta_scores()
    for j in range(2):
        _bounded_update(s[:, j * tq:(j + 1) * tq] + mbias_ref[0], m_ref[j], l_ref, acc_ref, vm_ref[...], j,
                        first=True)

    def consume_far(src, info, p):
        kc, c, _ = info
        j, lanes, cols = _part(p, tq)
        _bounded_update(src[:, cols], m_ref[j, :, lanes] - c, l_ref, acc_ref, v_ref[0, kc],
                        (j, slice(None), lanes))

    def consume_near(src, info, p):
        kc, _, r0 = info
        j, lanes, cols = _part(p, tq)
        bias = band_ref[0, pl.ds(r0, KEY_CHUNK), lanes]
        _bounded_update(src[:, cols] + bias, m_ref[j, :, lanes], l_ref, acc_ref, v_ref[0, kc],
                        (j, slice(None), lanes))

    _pipelined_chunks(nk, parts, chunk_of, qk_into, consume_far, sa_ref, sb_ref, unroll=FAST_UNROLL,
                      consume_tail=consume_near)

    denom = lambda j: jnp.sum(l_ref[j], axis=0, keepdims=True)
    l_min = jnp.minimum(jnp.min(denom(0)), jnp.min(denom(1)))

    @pl.when(jnp.logical_not(l_min >= L_MIN))
    def _():
        s = meta_scores()
        for j in range(2):
            _init_from_meta(s[:, j * tq:(j + 1) * tq] + mbias_ref[0], m_ref, l_ref, acc_ref, vm_ref[...], j)

        def exact_far(src, info, p):
            kc, c, _ = info
            j, lanes, cols = _part(p, tq)
            _online_update(src[:, cols], c, m_ref, l_ref, acc_ref, v_ref[0, kc], (j, slice(None), lanes))

        def exact_near(src, info, p):
            kc, _, r0 = info
            j, lanes, cols = _part(p, tq)
            bias = band_ref[0, pl.ds(r0, KEY_CHUNK), lanes]
            _online_update(src[:, cols] + bias, None, m_ref, l_ref, acc_ref, v_ref[0, kc],
                           (j, slice(None), lanes))

        _pipelined_chunks(nk, parts, chunk_of, qk_into, exact_far, sa_ref, sb_ref, unroll=NEAR_CHUNKS,
                          consume_tail=exact_near)

    lam = (jnp.exp(jnp.sum(lq1_ref[...] * lk1_ref[...], axis=-1, keepdims=True))
           - jnp.exp(jnp.sum(lq2_ref[...] * lk2_ref[...], axis=-1, keepdims=True)) + LAM_INIT)
    o = acc_ref[0] / denom(0) - lam * (acc_ref[1] / denom(1))
    ms = jnp.mean(o * o, axis=0, keepdims=True)
    o = o * lax.rsqrt(ms + EPS) * sn_ref[...] * (1.0 - LAM_INIT)
    o_ref[0] = o.T.astype(BF16)


def _attn_b(cfar, kmax, qb_t, kb, vb_t, kmb, vmb_t, band, mbias, lq1, lk1, lq2, lk2, sn, *, tq):
    bsz, _, n = qb_t.shape
    nk = n // KEY_CHUNK
    assert tq == Q_TILE and NEAR_CHUNKS == FAST_UNROLL and nk % FAST_UNROLL == 0 and nk > NEAR_CHUNKS
    vd = 2 * HEAD_DIM
    kb4 = kb.reshape(bsz, nk, KEY_CHUNK, WIDTH_B)
    vec = lambda: pl.BlockSpec((1, HEAD_DIM), lambda b, h, i: (0, 0))
    return pl.pallas_call(
        functools.partial(_attn_b_kernel, tq=tq, nk=nk),
        grid=(bsz, N_HEADS_B, n // tq),
        in_specs=[
            pl.BlockSpec(memory_space=pltpu.SMEM),
            pl.BlockSpec(memory_space=pltpu.SMEM),
            pl.BlockSpec((1, vd, tq), lambda b, h, i: (b, h, i)),
            pl.BlockSpec((1, nk, KEY_CHUNK, vd), lambda b, h, i: (b, 0, 0, h)),
            pl.BlockSpec((1, nk, VB_ROWS, KEY_CHUNK), lambda b, h, i: (b, 0, h, 0)),
            pl.BlockSpec((META_PAD, vd), lambda b, h, i: (0, h)),
            pl.BlockSpec((VB_ROWS, META_PAD), lambda b, h, i: (h, 0)),
            pl.BlockSpec((1, BAND_ROWS, tq), lambda b, h, i: (h, 0, 0)),
            pl.BlockSpec((1, META_PAD, tq), lambda b, h, i: (h, 0, i)),
            vec(), vec(), vec(), vec(),
            pl.BlockSpec((vd, 1), lambda b, h, i: (0, 0)),
        ],
        out_specs=pl.BlockSpec((1, tq, vd), lambda b, h, i: (b, i, h)),
        out_shape=jax.ShapeDtypeStruct((bsz, n, WIDTH_B), BF16),
        scratch_shapes=[
            pltpu.VMEM((vd, 2 * tq), BF16),
            pltpu.VMEM((KEY_CHUNK, 2 * tq), F32),
            pltpu.VMEM((KEY_CHUNK, 2 * tq), F32),
            pltpu.VMEM((2, 1, tq), F32),
            pltpu.VMEM((2, SUBLANES, tq), F32),
            pltpu.VMEM((2, VB_ROWS, tq), F32),
        ],
        compiler_params=pltpu.CompilerParams(
            dimension_semantics=("arbitrary", "arbitrary", "arbitrary"), vmem_limit_bytes=VMEM_LIMIT),
        name="attn_b",
    )(cfar, kmax, qb_t, kb4, vb_t, kmb, vmb_t, band, mbias, lq1, lk1, lq2, lk2, sn)


def _outproj_kernel(oa_ref, ob_ref, x_ref, wa_ref, wb_ref, g2_ref, wrh_ref, wrl_ref, br_ref,
                    h_ref, n2_ref, eid_ref, gate_ref):
    att = (jnp.dot(oa_ref[0], wa_ref[...], preferred_element_type=F32)
           + jnp.dot(ob_ref[0], wb_ref[...], preferred_element_type=F32))
    hres = x_ref[0] + att
    h_ref[0] = hres
    ms = jnp.mean(hres * hres, axis=-1, keepdims=True)
    n2 = hres * lax.rsqrt(ms + EPS) * g2_ref[...]
    tm = hres.shape[0]
    n2_hi = n2.astype(BF16)
    n2_lo = (n2 - n2_hi.astype(F32)).astype(BF16)
    for s in range(D_TILES):
        n2_ref[0, pl.ds(s, tm, stride=D_TILES), :] = n2[:, s * LANE:(s + 1) * LANE]
    wrh = wrh_ref[...]
    logits = _nt_dot(wrh, n2_hi) + _nt_dot(wrh, n2_lo) + _nt_dot(wrl_ref[...], n2_hi) + br_ref[...]
    gl = logits[N_EXPERTS:N_EXPERTS + N_GROUPS]
    gmax = jnp.max(gl, axis=0, keepdims=True)
    giota = lax.broadcasted_iota(jnp.int32, (N_GROUPS, tm), 0)
    g_idx = jnp.min(jnp.where(gl == gmax, giota, N_GROUPS), axis=0, keepdims=True)
    g_w = 1.0 / jnp.sum(jnp.exp(gl - gmax), axis=0, keepdims=True)
    sel = jnp.zeros((EXPERTS_PER_GROUP, tm), F32)
    for g in range(N_GROUPS):
        sel = jnp.where(g_idx == g, logits[g * EXPERTS_PER_GROUP:(g + 1) * EXPERTS_PER_GROUP], sel)
    eiota = lax.broadcasted_iota(jnp.int32, (EXPERTS_PER_GROUP, tm), 0)
    e1 = jnp.max(sel, axis=0, keepdims=True)
    i1 = jnp.min(jnp.where(sel == e1, eiota, EXPERTS_PER_GROUP), axis=0, keepdims=True)
    sel2 = jnp.where(eiota == i1, -jnp.inf, sel)
    e2 = jnp.max(sel2, axis=0, keepdims=True)
    i2 = jnp.min(jnp.where(sel2 == e2, eiota, EXPERTS_PER_GROUP), axis=0, keepdims=True)
    r = jnp.exp(e2 - e1)
    w1 = g_w / (1.0 + r)
    eid_ref[0] = jnp.concatenate([g_idx * EXPERTS_PER_GROUP + i1, g_idx * EXPERTS_PER_GROUP + i2], axis=0)
    gate_ref[0] = jnp.concatenate([w1, w1 * r], axis=0)


def _outproj(oa, ob, x, wa, wb, g2, wrh, wrl, br, *, tm):
    bsz, n, d = x.shape
    full = lambda shape: pl.BlockSpec(shape, lambda b, i: (0,) * len(shape))
    tok = lambda w: pl.BlockSpec((1, tm, w), lambda b, i: (b, i, 0))
    lane = lambda: pl.BlockSpec((1, TOP_K, tm), lambda b, i: (b, 0, i))
    return pl.pallas_call(
        _outproj_kernel,
        grid=(bsz, n // tm),
        in_specs=[tok(WIDTH_A), tok(WIDTH_B), tok(d), full((WIDTH_A, d)), full((WIDTH_B, d)), full((1, d)),
                  full((ROUTER_ROWS, d)), full((ROUTER_ROWS, d)), full((ROUTER_ROWS, 1))],
        out_specs=(tok(d), pl.BlockSpec((1, tm * D_TILES, LANE), lambda b, i: (b, i, 0)), lane(), lane()),
        out_shape=(
            jax.ShapeDtypeStruct((bsz, n, d), F32),
            jax.ShapeDtypeStruct((bsz, n * D_TILES, LANE), F32),
            jax.ShapeDtypeStruct((bsz, TOP_K, n), jnp.int32),
            jax.ShapeDtypeStruct((bsz, TOP_K, n), F32),
        ),
        compiler_params=pltpu.CompilerParams(
            dimension_semantics=("arbitrary", "arbitrary"), vmem_limit_bytes=VMEM_LIMIT),
        name="outproj_router",
    )(oa, ob, x, wa, wb, g2, wrh, wrl, br)


def _expert_kernel(be_ref, nb_ref, cur_ref, nxt_ref, x_hbm, wg_ref, wu_ref, wd_ref, y_ref, xbuf, sems, *, rows):
    i = pl.program_id(0)
    n_steps = pl.num_programs(0)
    n_used = nb_ref[0]
    slot = lax.rem(i, 2)

    def row_copy(tile_row, slot_, r):
        src = x_hbm.at[pl.ds(pl.multiple_of(tile_row, D_TILES), D_TILES), :]
        return pltpu.make_async_copy(src, xbuf.at[slot_, pl.ds(r * D_TILES, D_TILES), :], sems.at[slot_])

    def start_rows(idx_ref, slot_):
        for r in range(rows):
            row_copy(idx_ref[0, 0, r], slot_, r).start(priority=r % 2)

    def wait_rows(slot_):
        for r in range(rows):
            row_copy(0, slot_, r).wait()

    @pl.when(i == 0)
    def _():
        start_rows(cur_ref, 0)

    @pl.when(i < n_used)
    def _():
        wait_rows(slot)
        x = jnp.concatenate([xbuf[slot, pl.ds(s, rows, stride=D_TILES), :] for s in range(D_TILES)],
                            axis=1).astype(BF16)
        gt = jnp.dot(x, wg_ref[0], preferred_element_type=F32)
        up = jnp.dot(x, wu_ref[0], preferred_element_type=F32)
        act = gt * (1.0 / (1.0 + jnp.exp(-gt))) * up
        y = jnp.dot(act.astype(BF16), wd_ref[0], preferred_element_type=F32).astype(BF16)
        start_rows(nxt_ref, 1 - slot)
        y_ref[...] = y

    @pl.when(i >= n_used)
    def _():
        y_ref[...] = jnp.zeros_like(y_ref)

    @pl.when(i == n_used)
    def _():
        wait_rows(slot)

    @pl.when(jnp.logical_and(i == n_steps - 1, i < n_used))
    def _():
        wait_rows(1 - slot)


def _experts(block_e, n_used, row_tok, x, wg, wu, wd, *, rows):
    d = D_MODEL
    n_blocks = block_e.shape[0]
    idx = (row_tok * D_TILES).reshape(n_blocks, 1, rows)
    smem_rows = lambda shift: pl.BlockSpec(
        (1, 1, rows), lambda i, be, nb: (jnp.minimum(i + shift, n_blocks - 1), 0, 0), memory_space=pltpu.SMEM)
    grid_spec = pltpu.PrefetchScalarGridSpec(
        num_scalar_prefetch=2,
        grid=(n_blocks,),
        in_specs=[
            smem_rows(0),
            smem_rows(1),
            pl.BlockSpec(memory_space=pl.ANY),
            pl.BlockSpec((1, d, D_EXPERT), lambda i, be, nb: (be[i], 0, 0)),
            pl.BlockSpec((1, d, D_EXPERT), lambda i, be, nb: (be[i], 0, 0)),
            pl.BlockSpec((1, D_EXPERT, d), lambda i, be, nb: (be[i], 0, 0)),
        ],
        out_specs=pl.BlockSpec((rows, d), lambda i, be, nb: (i, 0)),
        scratch_shapes=[pltpu.VMEM((2, rows * D_TILES, LANE), F32), pltpu.SemaphoreType.DMA((2,))],
    )
    return pl.pallas_call(
        functools.partial(_expert_kernel, rows=rows),
        grid_spec=grid_spec,
        out_shape=jax.ShapeDtypeStruct((n_blocks * rows, d), BF16),
        compiler_params=pltpu.CompilerParams(
            dimension_semantics=("arbitrary",), vmem_limit_bytes=VMEM_LIMIT),
        name="experts",
    )(block_e, n_used, idx, idx, x, wg, wu, wd)


def _final_kernel(h_ref, y0_ref, y1_ref, gate_ref, gf_ref, o_ref):
    gates = gate_ref[...]
    hres = (h_ref[...] + gates[:, 0:1] * y0_ref[...].astype(F32)
            + gates[:, 1:2] * y1_ref[...].astype(F32))
    ms = jnp.mean(hres * hres, axis=-1, keepdims=True)
    o_ref[...] = hres * lax.rsqrt(ms + EPS) * gf_ref[...]


def _final(h, y0, y1, gates, gf, *, tm):
    t, d = h.shape
    tok = lambda w: pl.BlockSpec((tm, w), lambda i: (i, 0))
    return pl.pallas_call(
        _final_kernel,
        grid=(t // tm,),
        in_specs=[tok(d), tok(d), tok(d), tok(TOP_K), pl.BlockSpec((1, d), lambda i: (0, 0))],
        out_specs=tok(d),
        out_shape=jax.ShapeDtypeStruct((t, d), F32),
        compiler_params=pltpu.CompilerParams(
            dimension_semantics=("arbitrary",), vmem_limit_bytes=VMEM_LIMIT),
        name="combine_final_norm",
    )(h, y0, y1, gates, gf)


def _rope_tables(n_tok):
    rows = n_tok // GRID_W
    row = jnp.repeat(jnp.arange(rows, dtype=F32), GRID_W)
    col = jnp.tile(jnp.arange(GRID_W, dtype=F32), rows)
    inv = ROPE_THETA ** (-jnp.arange(0, AXIS_DIM, 2, dtype=F32) / AXIS_DIM)
    ar = row[:, None] * inv
    ac = col[:, None] * inv
    ang = jnp.concatenate([ar, ar, ac, ac], axis=-1)
    q = HEAD_DIM // 4
    sign = jnp.concatenate([-jnp.ones((q,), F32), jnp.ones((q,), F32)] * 2)
    return jnp.cos(ang).T, (jnp.sin(ang) * sign).T


def _t5_bucket(rp):
    nb = N_BUCKETS // 2
    max_exact = nb // 2
    ret = (rp > 0).astype(jnp.int32) * nb
    n = jnp.abs(rp)
    large = max_exact + (jnp.log(jnp.maximum(n, 1).astype(F32) / max_exact)
                         / math.log(MAX_DISTANCE / max_exact) * (nb - max_exact)).astype(jnp.int32)
    large = jnp.minimum(large, nb - 1)
    return ret + jnp.where(n < max_exact, n, large)


def _bias_lookup(rel_bias, d):
    return (rel_bias.astype(F32) * LOG2E)[_t5_bucket(d)].T


def _band_tables(rel_bias, tq):
    assert MAX_DISTANCE <= LANE
    far = jnp.array([-2 * MAX_DISTANCE, 2 * MAX_DISTANCE], jnp.int32)
    c_far = _bias_lookup(rel_bias, far)
    rev = _bias_lookup(rel_bias, 2 * LANE - 1 - jnp.arange(4 * LANE - 1, dtype=jnp.int32))
    strip = jnp.stack([rev[:, LANE - 1 - rho:4 * LANE - 1 - rho] for rho in range(LANE)], axis=1)
    const = lambda col: jnp.broadcast_to(c_far[:, col, None, None], (N_HEADS_B, LANE, LANE))

    def block(k):
        if k <= -2 or k >= 2:
            return const(0 if k < 0 else 1)
        return strip[:, :, (1 - k) * LANE:(2 - k) * LANE]

    shift_blocks = BAND_SHIFT * KEY_CHUNK // LANE
    band = jnp.concatenate(
        [jnp.concatenate([block(a - b - shift_blocks) for b in range(tq // LANE)], axis=2)
         for a in range(BAND_ROWS // LANE)], axis=1)
    b_max = jnp.max(rel_bias.astype(F32) * LOG2E, axis=0)[:, None]
    return band, jnp.concatenate([c_far, b_max], axis=1)


def _meta_bias(rel_bias, n_tok):
    v = _bias_lookup(rel_bias, -jnp.arange(n_tok + N_META + 1, dtype=jnp.int32))
    mb = jnp.stack([v[:, N_META - j:N_META - j + n_tok] for j in range(N_META)], axis=1)
    pad = jnp.full((N_HEADS_B, META_PAD - N_META, n_tok), MASK_VALUE, F32)
    return jnp.concatenate([mb, pad], axis=1)


def _dispatch_plan(eid, rows):
    t = eid.shape[1]
    n_assign = TOP_K * t
    flat = eid.reshape(-1)
    a_idx = jnp.arange(n_assign, dtype=jnp.int32)
    skey = jnp.sort(flat * n_assign + a_idx)
    s_assign = skey % n_assign
    experts = jnp.arange(N_EXPERTS, dtype=jnp.int32)
    bounds = jnp.arange(N_EXPERTS + 1, dtype=jnp.int32) * n_assign
    off_all = jnp.sum((skey[:, None] < bounds[None, :]).astype(jnp.int32), axis=0)
    off = off_all[:-1]
    counts = off_all[1:] - off
    padded = (counts + rows - 1) // rows * rows
    pend = jnp.cumsum(padded)
    poff = pend - padded
    step = jnp.diff(poff - off, prepend=0)
    dest = a_idx + jnp.sum(jnp.where(a_idx[:, None] >= off[None, :], step[None, :], 0), axis=1)
    _, pos = lax.sort((s_assign, dest), num_keys=1)
    n_blocks = (n_assign + N_EXPERTS * (rows - 1) + rows - 1) // rows
    cap = n_blocks * rows
    block_start = jnp.arange(n_blocks, dtype=jnp.int32) * rows
    block_e = jnp.minimum(jnp.sum((block_start[:, None] >= pend[None, :]).astype(jnp.int32), axis=1),
                          N_EXPERTS - 1)
    d = jnp.arange(cap, dtype=jnp.int32)
    e_d = jnp.repeat(block_e, rows)
    onehot_d = e_d[:, None] == experts[None, :]
    pick = lambda v: jnp.sum(jnp.where(onehot_d, v[None, :], 0), axis=1)
    j = d - pick(poff)
    valid = j < pick(counts)
    src = jnp.clip(pick(off) + j, 0, n_assign - 1)
    row_tok = jnp.where(valid, s_assign[src] % t, 0)
    n_used = (pend[-1] // rows).astype(jnp.int32).reshape(1)
    return row_tok, block_e, n_used, pos.reshape(TOP_K, t)


def _key_bounds(knorm, knorm_meta):
    both = jnp.maximum(jnp.max(knorm, axis=1), knorm_meta[0, 0][None])
    return jnp.sqrt(both[:, 1:1 + N_KV_A, 0]), jnp.sqrt(both[:, 0, 0:2 * N_HEADS_B])


def _mixer(x, shared):
    bsz, n_tok, d = x.shape
    assert n_tok % Q_TILE == 0 and n_tok % TOKEN_TILE == 0
    tq = Q_TILE
    tm = TOKEN_TILE
    cos_t, sin_t = _rope_tables(n_tok)
    qa_t, ka, va_t, qb_t, kb, vb_t, knorm = _inproj(
        x, shared['g1'], shared['wt'], shared['wkb'], cos_t, sin_t, shared['qn'], shared['kn'],
        tm=tm, chunk=KEY_CHUNK)
    kmax_a, kmax_b = _key_bounds(knorm, shared['knorm_meta'])
    oa = _attn_a(kmax_a, qa_t, ka, va_t, shared['km_a'], shared['vm_a'], tq=tq)
    mbias = _meta_bias(shared['rel_bias'], n_tok)
    ob = _attn_b(shared['cfar'], kmax_b, qb_t, kb, vb_t, shared['km_b'], shared['vm_b'], shared['band'], mbias,
                 shared['lq1'], shared['lk1'], shared['lq2'], shared['lk2'], shared['sn'], tq=tq)
    h, n2, eid, gates = _outproj(oa, ob, x, shared['wa'], shared['wb'], shared['g2'],
                                 shared['wrh'], shared['wrl'], shared['br'], tm=tm)
    t = bsz * n_tok
    eid = jnp.transpose(eid, (1, 0, 2)).reshape(TOP_K, t)
    gates = jnp.transpose(gates, (1, 0, 2)).reshape(TOP_K, t)
    row_tok, block_e, n_used, pos = _dispatch_plan(eid, MOE_ROWS)
    return h.reshape(t, d), n2.reshape(t * D_TILES, LANE), row_tok, block_e, n_used, pos, gates.T


def _moe_and_norm(mixed, shape, shared):
    h, n2, row_tok, block_e, n_used, pos, gates = mixed
    y = _experts(block_e, n_used, row_tok, n2, shared['wg'], shared['wu'], shared['wd'], rows=MOE_ROWS)
    y0 = jnp.take(y, pos[0], axis=0)
    y1 = jnp.take(y, pos[1], axis=0)
    return _final(h, y0, y1, gates, shared['gf'], tm=TOKEN_TILE).reshape(shape)


def kernel(x_prompt, x_sample, meta_tokens, rel_bias, norm1, w_in, q_norm, k_norm, lambda_q1, lambda_k1,
           lambda_q2, lambda_k2, sub_norm, w_out, norm2, w_router_group, b_router_group, w_router_expert,
           b_router_expert, w_gate, w_up, w_down, final_norm):
    i = LAYER
    d = D_MODEL
    o0 = WIDTH_A
    o1 = o0 + KV_A
    o2 = o1 + KV_A
    o3 = o2 + WIDTH_B
    o4 = o3 + WIDTH_B
    w = w_in[i]
    wt = jnp.concatenate([w[:, :o2], w[:, o2:o3], w[:, o4:]], axis=1).T.astype(BF16)
    wr = jnp.concatenate([w_router_expert[i], w_router_group[i],
                          jnp.zeros((d, ROUTER_ROWS - N_EXPERTS - N_GROUPS), F32)], axis=1).T
    wrh = wr.astype(BF16)
    br = jnp.concatenate([b_router_expert[i], b_router_group[i],
                          jnp.zeros((ROUTER_ROWS - N_EXPERTS - N_GROUPS,), F32)]).reshape(ROUTER_ROWS, 1)
    shared = {
        'g1': norm1[i].reshape(1, d), 'wt': wt, 'wkb': w[:, o3:o4].astype(BF16),
        'qn': q_norm[i].reshape(HEAD_DIM, 1), 'kn': k_norm[i].reshape(HEAD_DIM, 1),
        'rel_bias': rel_bias,
        'lq1': lambda_q1[i].reshape(1, HEAD_DIM), 'lk1': lambda_k1[i].reshape(1, HEAD_DIM),
        'lq2': lambda_q2[i].reshape(1, HEAD_DIM), 'lk2': lambda_k2[i].reshape(1, HEAD_DIM),
        'sn': sub_norm[i].reshape(2 * HEAD_DIM, 1),
        'wa': w_out[i][:WIDTH_A].astype(BF16), 'wb': w_out[i][WIDTH_A:].astype(BF16),
        'g2': norm2[i].reshape(1, d), 'wrh': wrh, 'wrl': (wr - wrh.astype(F32)).astype(BF16), 'br': br,
        'wg': w_gate[i].astype(BF16), 'wu': w_up[i].astype(BF16), 'wd': w_down[i].astype(BF16),
        'gf': final_norm.reshape(1, d),
    }
    xm = jnp.concatenate([meta_tokens.astype(F32), jnp.zeros((META_PAD - N_META, d), F32)], axis=0)[None]
    ones = jnp.ones((HEAD_DIM, META_PAD), F32)
    _, km_a, vm_a, _, km_b, vm_b, knorm_meta = _inproj(
        xm, shared['g1'], shared['wt'], shared['wkb'], ones, 0.0 * ones, shared['qn'], shared['kn'],
        tm=META_PAD, chunk=META_PAD)
    shared['km_a'] = km_a[0]
    shared['vm_a'] = vm_a[0, 0]
    shared['km_b'] = km_b[0]
    shared['vm_b'] = vm_b[0, 0]
    shared['knorm_meta'] = knorm_meta
    shared['band'], shared['cfar'] = _band_tables(rel_bias, Q_TILE)
    mixed_prompt = _mixer(x_prompt, shared)
    mixed_sample = _mixer(x_sample, shared)
    return (_moe_and_norm(mixed_prompt, x_prompt.shape, shared),
            _moe_and_norm(mixed_sample, x_sample.shape, shared))
```

```python
import functools
import math

import jax
import jax.numpy as jnp
from jax import lax
from jax.experimental import pallas as pl
from jax.experimental.pallas import tpu as pltpu

D_MODEL = 1024
HEAD_DIM = 64
N_HEADS_A = 8
N_KV_A = 2
GQA_GROUP = N_HEADS_A // N_KV_A
N_HEADS_B = 4
AXIS_DIM = HEAD_DIM // 2
ROPE_THETA = 10000.0
GRID_W = 64
N_META = 16
N_BUCKETS = 32
MAX_DISTANCE = 128
WIDTH_A = N_HEADS_A * HEAD_DIM
WIDTH_B = N_HEADS_B * 2 * HEAD_DIM
KV_A = N_KV_A * HEAD_DIM
N_GROUPS = 4
EXPERTS_PER_GROUP = 8
N_EXPERTS = N_GROUPS * EXPERTS_PER_GROUP
TOP_K = 2
D_EXPERT = D_MODEL // 2
EPS = 1e-6
LAYER = 0
LAM_INIT = 0.8 - 0.6 * math.exp(-0.3 * LAYER)

LOG2E = 1.4426950408889634
Q_SCALE = (HEAD_DIM ** -0.5) * LOG2E
MASK_VALUE = -1e30

LANE = 128
META_PAD = LANE
KEY_CHUNK = 256
TOKEN_TILE = 512
Q_TILE = 512
MOE_ROWS = 512
ROUTER_ROWS = 40
VMEM_LIMIT = 48 * 1024 * 1024

BF16_ROWS = 16
VA_ROWS = HEAD_DIM + BF16_ROWS
VB_ROWS = 2 * HEAD_DIM + BF16_ROWS
TILE_CHUNKS = Q_TILE // KEY_CHUNK
NEAR_CHUNKS = TILE_CHUNKS + 2
BAND_SHIFT = 2
BAND_ROWS = (TILE_CHUNKS + 5) * KEY_CHUNK

BOUND_MARGIN = 1.01
L_MIN = 2.0 ** -64
D_TILES = D_MODEL // LANE
NORM_ROWS = 8
PART_LANES = 512
FAST_UNROLL = 4

F32 = jnp.float32
BF16 = jnp.bfloat16


def _nt_dot(a, b):
    return lax.dot_general(a, b, (((1,), (1,)), ((), ())), preferred_element_type=F32)


def _inproj_kernel(x_ref, g1_ref, wt_ref, wkb_ref, cos_ref, sin_ref, qn_ref, kn_ref,
                   qa_ref, ka_ref, va_ref, qb_ref, kb_ref, vb_ref, knorm_ref, *, chunk):
    x = x_ref[0]
    ms = jnp.mean(x * x, axis=-1, keepdims=True)
    n = (x * lax.rsqrt(ms + EPS) * g1_ref[...]).astype(BF16)
    pt = _nt_dot(wt_ref[...], n)
    kb = jnp.dot(n, wkb_ref[...], preferred_element_type=F32).astype(BF16)
    kb_ref[0] = kb
    kbf = kb.astype(F32)
    seg = (lax.broadcasted_iota(jnp.int32, (WIDTH_B, LANE), 0) // HEAD_DIM
           == lax.broadcasted_iota(jnp.int32, (WIDTH_B, LANE), 1)).astype(BF16)
    kb_sq = jnp.dot((kbf * kbf).astype(BF16), seg, preferred_element_type=F32)
    kb_max = jnp.max(kb_sq, axis=0, keepdims=True)
    cos = cos_ref[...]
    sin = sin_ref[...]
    q = HEAD_DIM // 4

    def norm_rope(blk, gain):
        ss = jnp.mean(blk * blk, axis=0, keepdims=True)
        y = blk * lax.rsqrt(ss + EPS) * gain
        swapped = jnp.concatenate([y[q:2 * q], y[0:q], y[3 * q:4 * q], y[2 * q:3 * q]], axis=0)
        return y * cos + swapped * sin

    qn = qn_ref[...]
    kn = kn_ref[...]
    for h in range(N_HEADS_A):
        blk = pt[h * HEAD_DIM:(h + 1) * HEAD_DIM]
        qa_ref[0, h * HEAD_DIM:(h + 1) * HEAD_DIM, :] = (norm_rope(blk, qn) * Q_SCALE).astype(BF16)
    o0 = WIDTH_A
    ka = jnp.concatenate(
        [norm_rope(pt[o0 + j * HEAD_DIM:o0 + (j + 1) * HEAD_DIM], kn) for j in range(N_KV_A)], axis=0)
    ka_ref[0] = ka.T.astype(BF16)
    kaf = ka.astype(BF16).astype(F32)
    ka_sq = kaf * kaf
    ka_max = [jnp.max(jnp.sum(ka_sq[g * HEAD_DIM:(g + 1) * HEAD_DIM], axis=0, keepdims=True),
                      axis=1, keepdims=True) for g in range(N_KV_A)]
    knorm_ref[0, 0] = jnp.concatenate(
        [kb_max] + [jnp.broadcast_to(v, (1, LANE)) for v in ka_max]
        + [jnp.zeros((NORM_ROWS - 1 - N_KV_A, LANE), F32)], axis=0)
    o1 = o0 + KV_A
    o2 = o1 + KV_A
    o3 = o2 + WIDTH_B
    tm = x.shape[0]
    vd = 2 * HEAD_DIM
    ones_rows = (lax.broadcasted_iota(jnp.int32, (BF16_ROWS, chunk), 0) == 0).astype(BF16)
    for j in range(tm // chunk):
        cols = slice(j * chunk, (j + 1) * chunk)
        for g in range(N_KV_A):
            va_ref[0, j, g * VA_ROWS:g * VA_ROWS + HEAD_DIM] = (
                pt[o1 + g * HEAD_DIM:o1 + (g + 1) * HEAD_DIM, cols].astype(BF16))
            va_ref[0, j, g * VA_ROWS + HEAD_DIM:(g + 1) * VA_ROWS] = ones_rows
        for h in range(N_HEADS_B):
            vb_ref[0, j, h * VB_ROWS:h * VB_ROWS + vd] = pt[o3 + h * vd:o3 + (h + 1) * vd, cols].astype(BF16)
            vb_ref[0, j, h * VB_ROWS + vd:(h + 1) * VB_ROWS] = ones_rows
    qb_ref[0] = (pt[o2:o3] * Q_SCALE).astype(BF16)


def _inproj(x, g1, wt, wkb, cos_t, sin_t, qn, kn, *, tm, chunk):
    bsz, n, d = x.shape
    rows_t = wt.shape[0]
    grid = (bsz, n // tm)
    full = lambda shape: pl.BlockSpec(shape, lambda b, i: (0,) * len(shape))
    va_rows = N_KV_A * VA_ROWS
    vb_rows = N_HEADS_B * VB_ROWS
    out_shape = (
        jax.ShapeDtypeStruct((bsz, WIDTH_A, n), BF16),
        jax.ShapeDtypeStruct((bsz, n, KV_A), BF16),
        jax.ShapeDtypeStruct((bsz, n // chunk, va_rows, chunk), BF16),
        jax.ShapeDtypeStruct((bsz, WIDTH_B, n), BF16),
        jax.ShapeDtypeStruct((bsz, n, WIDTH_B), BF16),
        jax.ShapeDtypeStruct((bsz, n // chunk, vb_rows, chunk), BF16),
        jax.ShapeDtypeStruct((bsz, n // tm, NORM_ROWS, LANE), F32),
    )
    return pl.pallas_call(
        functools.partial(_inproj_kernel, chunk=chunk),
        grid=grid,
        in_specs=[
            pl.BlockSpec((1, tm, d), lambda b, i: (b, i, 0)),
            full((1, d)),
            full((rows_t, d)),
            full((d, WIDTH_B)),
            pl.BlockSpec((HEAD_DIM, tm), lambda b, i: (0, i)),
            pl.BlockSpec((HEAD_DIM, tm), lambda b, i: (0, i)),
            full((HEAD_DIM, 1)),
            full((HEAD_DIM, 1)),
        ],
        out_specs=(
            pl.BlockSpec((1, WIDTH_A, tm), lambda b, i: (b, 0, i)),
            pl.BlockSpec((1, tm, KV_A), lambda b, i: (b, i, 0)),
            pl.BlockSpec((1, tm // chunk, va_rows, chunk), lambda b, i: (b, i, 0, 0)),
            pl.BlockSpec((1, WIDTH_B, tm), lambda b, i: (b, 0, i)),
            pl.BlockSpec((1, tm, WIDTH_B), lambda b, i: (b, i, 0)),
            pl.BlockSpec((1, tm // chunk, vb_rows, chunk), lambda b, i: (b, i, 0, 0)),
            pl.BlockSpec((1, 1, NORM_ROWS, LANE), lambda b, i: (b, i, 0, 0)),
        ),
        out_shape=out_shape,
        compiler_params=pltpu.CompilerParams(
            dimension_semantics=("arbitrary", "arbitrary"), vmem_limit_bytes=VMEM_LIMIT),
        name="inproj",
    )(x, g1, wt, wkb, cos_t, sin_t, qn, kn)


def _online_update(s, c_shift, m_ref, acc_ref, vblk, idx):
    mc = jnp.max(s, axis=0, keepdims=True)
    if c_shift is not None:
        mc = mc + c_shift
    m_old = m_ref[idx]
    m_new = jnp.maximum(m_old, mc)
    alpha = jnp.exp2(m_old - m_new)
    shift = m_new if c_shift is None else m_new - c_shift
    p = jnp.exp2(s - shift).astype(BF16)
    pv = jnp.dot(vblk, p, preferred_element_type=F32)
    acc_ref[idx] = alpha * acc_ref[idx] + pv
    m_ref[idx] = m_new


def _init_from_meta(s, m_ref, acc_ref, vmeta, idx):
    m0 = jnp.max(s, axis=0, keepdims=True)
    p = jnp.exp2(s - m0).astype(BF16)
    m_ref[idx] = m0
    acc_ref[idx] = jnp.dot(vmeta, p, preferred_element_type=F32)


def _bounded_update(s, shift, acc_ref, vblk, idx, first=False):
    p = jnp.exp2(s - shift).astype(BF16)
    pv = jnp.dot(vblk, p, preferred_element_type=F32)
    acc_ref[idx] = pv if first else acc_ref[idx] + pv


def _part(p, tq):
    c0 = p * PART_LANES
    return c0 // tq, slice(c0 % tq, c0 % tq + PART_LANES), slice(c0, c0 + PART_LANES)


def _query_norm(q_bf16):
    qf = q_bf16.astype(F32)
    return jnp.sqrt(jnp.sum(qf * qf, axis=0, keepdims=True))


def _pipelined_chunks(count, parts, chunk_of, qk_into, consume, sa_ref, sb_ref, unroll=2, consume_tail=None):
    assert unroll % 2 == 0 and count >= unroll and count % unroll == 0
    bufs = (sa_ref, sb_ref)

    def step(u, nxt, cur, use=consume):
        for part in range(parts):
            if nxt is not None:
                qk_into(bufs[(u + 1) % 2], nxt, part)
            use(bufs[u % 2], cur, part)

    for part in range(parts):
        qk_into(sa_ref, chunk_of(0), part)

    def body(j, carry):
        i = unroll * j
        for u in range(unroll):
            step(u, chunk_of(i + u + 1), chunk_of(i + u))
        return carry

    lax.fori_loop(0, count // unroll - 1, body, 0)
    i = count - unroll
    for u in range(unroll):
        step(u, chunk_of(i + u + 1) if u + 1 < unroll else None, chunk_of(i + u), consume_tail or consume)


def _attn_a_kernel(kmax_ref, q_ref, k_ref, v_ref, km_ref, vm_ref, o_ref, qpad_ref, sa_ref, sb_ref, m_ref,
                   acc_ref, *, tq, nk):
    b = pl.program_id(0)
    g = pl.program_id(1)
    kmax = kmax_ref[b, g] * BOUND_MARGIN
    zero = jnp.zeros((HEAD_DIM, tq), BF16)
    for h in range(GQA_GROUP):
        qh = q_ref[0, h * HEAD_DIM:(h + 1) * HEAD_DIM, :]
        qpad_ref[0:HEAD_DIM, h * tq:(h + 1) * tq] = jnp.where(g == 0, qh, zero)
        qpad_ref[HEAD_DIM:2 * HEAD_DIM, h * tq:(h + 1) * tq] = jnp.where(g == 1, qh, zero)
        m_ref[h] = _query_norm(qh) * kmax

    valid = lax.broadcasted_iota(jnp.int32, (META_PAD, 1), 0) < N_META

    def meta_scores():
        s = jnp.dot(km_ref[...], qpad_ref[...], preferred_element_type=F32)
        return jnp.where(valid, s, MASK_VALUE)

    parts = GQA_GROUP * tq // PART_LANES

    def qk_into(dst, kc, p):
        cols = _part(p, tq)[2]
        dst[:, cols] = jnp.dot(k_ref[0, kc], qpad_ref[:, cols], preferred_element_type=F32)

    s = meta_scores()
    for h in range(GQA_GROUP):
        _bounded_update(s[:, h * tq:(h + 1) * tq], m_ref[h], acc_ref, vm_ref[...], h, first=True)

    def consume(src, kc, p):
        h, lanes, cols = _part(p, tq)
        _bounded_update(src[:, cols], m_ref[h, :, lanes], acc_ref, v_ref[0, kc], (h, slice(None), lanes))

    _pipelined_chunks(nk, parts, lambda i: i, qk_into, consume, sa_ref, sb_ref, unroll=FAST_UNROLL)

    denom = lambda h: acc_ref[h, HEAD_DIM:HEAD_DIM + 1, :]
    l_min = jnp.min(denom(0))
    for h in range(1, GQA_GROUP):
        l_min = jnp.minimum(l_min, jnp.min(denom(h)))

    @pl.when(jnp.logical_not(l_min >= L_MIN))
    def _():
        s = meta_scores()
        for h in range(GQA_GROUP):
            _init_from_meta(s[:, h * tq:(h + 1) * tq], m_ref, acc_ref, vm_ref[...], h)

        def consume_exact(src, kc, p):
            h, lanes, cols = _part(p, tq)
            _online_update(src[:, cols], None, m_ref, acc_ref, v_ref[0, kc], (h, slice(None), lanes))

        _pipelined_chunks(nk, parts, lambda i: i, qk_into, consume_exact, sa_ref, sb_ref)

    o = jnp.concatenate([acc_ref[h, 0:HEAD_DIM, :] / denom(h) for h in range(GQA_GROUP)], axis=0)
    o_ref[0] = o.T.astype(BF16)


def _attn_a(kmax, qa_t, ka, va_t, km, vm_t, *, tq):
    bsz, _, n = qa_t.shape
    nk = n // KEY_CHUNK
    ka4 = ka.reshape(bsz, nk, KEY_CHUNK, KV_A)
    gw = GQA_GROUP * HEAD_DIM
    return pl.pallas_call(
        functools.partial(_attn_a_kernel, tq=tq, nk=nk),
        grid=(bsz, N_KV_A, n // tq),
        in_specs=[
            pl.BlockSpec(memory_space=pltpu.SMEM),
            pl.BlockSpec((1, gw, tq), lambda b, g, i: (b, g, i)),
            pl.BlockSpec((1, nk, KEY_CHUNK, KV_A), lambda b, g, i: (b, 0, 0, 0)),
            pl.BlockSpec((1, nk, VA_ROWS, KEY_CHUNK), lambda b, g, i: (b, 0, g, 0)),
            pl.BlockSpec((META_PAD, KV_A), lambda b, g, i: (0, 0)),
            pl.BlockSpec((VA_ROWS, META_PAD), lambda b, g, i: (g, 0)),
        ],
        out_specs=pl.BlockSpec((1, tq, gw), lambda b, g, i: (b, i, g)),
        out_shape=jax.ShapeDtypeStruct((bsz, n, WIDTH_A), BF16),
        scratch_shapes=[
            pltpu.VMEM((KV_A, GQA_GROUP * tq), BF16),
            pltpu.VMEM((KEY_CHUNK, GQA_GROUP * tq), F32),
            pltpu.VMEM((KEY_CHUNK, GQA_GROUP * tq), F32),
            pltpu.VMEM((GQA_GROUP, 1, tq), F32),
            pltpu.VMEM((GQA_GROUP, VA_ROWS, tq), F32),
        ],
        compiler_params=pltpu.CompilerParams(
            dimension_semantics=("arbitrary", "arbitrary", "arbitrary"), vmem_limit_bytes=VMEM_LIMIT),
        name="attn_a",
    )(kmax, qa_t, ka4, va_t, km, vm_t)


def _attn_b_kernel(cfar_ref, kmax_ref, q_ref, k_ref, v_ref, km_ref, vm_ref, band_ref, mbias_ref,
                   lq1_ref, lk1_ref, lq2_ref, lk2_ref, sn_ref, o_ref,
                   qblk_ref, sa_ref, sb_ref, m_ref, acc_ref, *, tq, nk):
    b = pl.program_id(0)
    h = pl.program_id(1)
    qi = pl.program_id(2)
    vd = 2 * HEAD_DIM
    zero = jnp.zeros((HEAD_DIM, tq), BF16)
    q1 = q_ref[0, 0:HEAD_DIM, :]
    q2 = q_ref[0, HEAD_DIM:vd, :]
    qblk_ref[0:HEAD_DIM, 0:tq] = q1
    qblk_ref[0:HEAD_DIM, tq:2 * tq] = zero
    qblk_ref[HEAD_DIM:vd, 0:tq] = zero
    qblk_ref[HEAD_DIM:vd, tq:2 * tq] = q2
    c_neg = cfar_ref[h, 0]
    c_pos = cfar_ref[h, 1]
    b_max = cfar_ref[h, 2]
    for j, qj in enumerate((q1, q2)):
        m_ref[j] = _query_norm(qj) * (kmax_ref[b, 2 * h + j] * BOUND_MARGIN) + b_max

    def meta_scores():
        return jnp.dot(km_ref[...], qblk_ref[...], preferred_element_type=F32)

    parts = 2 * tq // PART_LANES

    def qk_into(dst, info, p):
        cols = _part(p, tq)[2]
        dst[:, cols] = jnp.dot(k_ref[0, info[0]], qblk_ref[:, cols], preferred_element_type=F32)

    n_far = nk - NEAR_CHUNKS
    near_lo = jnp.clip(qi * TILE_CHUNKS - 1, 0, n_far)

    def chunk_of(i):
        right = i >= near_lo
        kc_far = jnp.where(right, i + NEAR_CHUNKS, i)
        kc = jnp.where(i >= n_far, near_lo + (i - n_far), kc_far)
        r0 = pl.multiple_of(jnp.clip(kc - qi * TILE_CHUNKS + BAND_SHIFT, 0, BAND_ROWS // KEY_CHUNK - 1)
                            * KEY_CHUNK, KEY_CHUNK)
        return kc, jnp.where(right, c_pos, c_neg), r0

    s = meta_scores()
    for j in range(2):
        _bounded_update(s[:, j * tq:(j + 1) * tq] + mbias_ref[0], m_ref[j], acc_ref, vm_ref[...], j, first=True)

    def consume_far(src, info, p):
        kc, c, _ = info
        j, lanes, cols = _part(p, tq)
        _bounded_update(src[:, cols], m_ref[j, :, lanes] - c, acc_ref, v_ref[0, kc],
                        (j, slice(None), lanes))

    def consume_near(src, info, p):
        kc, _, r0 = info
        j, lanes, cols = _part(p, tq)
        bias = band_ref[0, pl.ds(r0, KEY_CHUNK), lanes]
        _bounded_update(src[:, cols] + bias, m_ref[j, :, lanes], acc_ref, v_ref[0, kc],
                        (j, slice(None), lanes))

    _pipelined_chunks(nk, parts, chunk_of, qk_into, consume_far, sa_ref, sb_ref, unroll=FAST_UNROLL,
                      consume_tail=consume_near)

    denom = lambda j: acc_ref[j, vd:vd + 1, :]
    l_min = jnp.minimum(jnp.min(denom(0)), jnp.min(denom(1)))

    @pl.when(jnp.logical_not(l_min >= L_MIN))
    def _():
        s = meta_scores()
        for j in range(2):
            _init_from_meta(s[:, j * tq:(j + 1) * tq] + mbias_ref[0], m_ref, acc_ref, vm_ref[...], j)

        def exact_far(src, info, p):
            kc, c, _ = info
            j, lanes, cols = _part(p, tq)
            _online_update(src[:, cols], c, m_ref, acc_ref, v_ref[0, kc], (j, slice(None), lanes))

        def exact_near(src, info, p):
            kc, _, r0 = info
            j, lanes, cols = _part(p, tq)
            bias = band_ref[0, pl.ds(r0, KEY_CHUNK), lanes]
            _online_update(src[:, cols] + bias, None, m_ref, acc_ref, v_ref[0, kc], (j, slice(None), lanes))

        _pipelined_chunks(nk, parts, chunk_of, qk_into, exact_far, sa_ref, sb_ref, unroll=NEAR_CHUNKS,
                          consume_tail=exact_near)

    lam = (jnp.exp(jnp.sum(lq1_ref[...] * lk1_ref[...], axis=-1, keepdims=True))
           - jnp.exp(jnp.sum(lq2_ref[...] * lk2_ref[...], axis=-1, keepdims=True)) + LAM_INIT)
    o = acc_ref[0, 0:vd, :] / denom(0) - lam * (acc_ref[1, 0:vd, :] / denom(1))
    ms = jnp.mean(o * o, axis=0, keepdims=True)
    o = o * lax.rsqrt(ms + EPS) * sn_ref[...] * (1.0 - LAM_INIT)
    o_ref[0] = o.T.astype(BF16)


def _attn_b(cfar, kmax, qb_t, kb, vb_t, kmb, vmb_t, band, mbias, lq1, lk1, lq2, lk2, sn, *, tq):
    bsz, _, n = qb_t.shape
    nk = n // KEY_CHUNK
    assert tq == Q_TILE and NEAR_CHUNKS == FAST_UNROLL and nk % FAST_UNROLL == 0 and nk > NEAR_CHUNKS
    vd = 2 * HEAD_DIM
    kb4 = kb.reshape(bsz, nk, KEY_CHUNK, WIDTH_B)
    vec = lambda: pl.BlockSpec((1, HEAD_DIM), lambda b, h, i: (0, 0))
    return pl.pallas_call(
        functools.partial(_attn_b_kernel, tq=tq, nk=nk),
        grid=(bsz, N_HEADS_B, n // tq),
        in_specs=[
            pl.BlockSpec(memory_space=pltpu.SMEM),
            pl.BlockSpec(memory_space=pltpu.SMEM),
            pl.BlockSpec((1, vd, tq), lambda b, h, i: (b, h, i)),
            pl.BlockSpec((1, nk, KEY_CHUNK, vd), lambda b, h, i: (b, 0, 0, h)),
            pl.BlockSpec((1, nk, VB_ROWS, KEY_CHUNK), lambda b, h, i: (b, 0, h, 0)),
            pl.BlockSpec((META_PAD, vd), lambda b, h, i: (0, h)),
            pl.BlockSpec((VB_ROWS, META_PAD), lambda b, h, i: (h, 0)),
            pl.BlockSpec((1, BAND_ROWS, tq), lambda b, h, i: (h, 0, 0)),
            pl.BlockSpec((1, META_PAD, tq), lambda b, h, i: (h, 0, i)),
            vec(), vec(), vec(), vec(),
            pl.BlockSpec((vd, 1), lambda b, h, i: (0, 0)),
        ],
        out_specs=pl.BlockSpec((1, tq, vd), lambda b, h, i: (b, i, h)),
        out_shape=jax.ShapeDtypeStruct((bsz, n, WIDTH_B), BF16),
        scratch_shapes=[
            pltpu.VMEM((vd, 2 * tq), BF16),
            pltpu.VMEM((KEY_CHUNK, 2 * tq), F32),
            pltpu.VMEM((KEY_CHUNK, 2 * tq), F32),
            pltpu.VMEM((2, 1, tq), F32),
            pltpu.VMEM((2, VB_ROWS, tq), F32),
        ],
        compiler_params=pltpu.CompilerParams(
            dimension_semantics=("arbitrary", "arbitrary", "arbitrary"), vmem_limit_bytes=VMEM_LIMIT),
        name="attn_b",
    )(cfar, kmax, qb_t, kb4, vb_t, kmb, vmb_t, band, mbias, lq1, lk1, lq2, lk2, sn)


def _outproj_kernel(oa_ref, ob_ref, x_ref, wa_ref, wb_ref, g2_ref, wrh_ref, wrl_ref, br_ref,
                    h_ref, n2_ref, eid_ref, gate_ref):
    att = (jnp.dot(oa_ref[0], wa_ref[...], preferred_element_type=F32)
           + jnp.dot(ob_ref[0], wb_ref[...], preferred_element_type=F32))
    hres = x_ref[0] + att
    h_ref[0] = hres
    ms = jnp.mean(hres * hres, axis=-1, keepdims=True)
    n2 = hres * lax.rsqrt(ms + EPS) * g2_ref[...]
    tm = hres.shape[0]
    n2_hi = n2.astype(BF16)
    n2_lo = (n2 - n2_hi.astype(F32)).astype(BF16)
    for s in range(D_TILES):
        n2_ref[0, pl.ds(s, tm, stride=D_TILES), :] = n2[:, s * LANE:(s + 1) * LANE]
    wrh = wrh_ref[...]
    logits = _nt_dot(wrh, n2_hi) + _nt_dot(wrh, n2_lo) + _nt_dot(wrl_ref[...], n2_hi) + br_ref[...]
    gl = logits[N_EXPERTS:N_EXPERTS + N_GROUPS]
    gmax = jnp.max(gl, axis=0, keepdims=True)
    giota = lax.broadcasted_iota(jnp.int32, (N_GROUPS, tm), 0)
    g_idx = jnp.min(jnp.where(gl == gmax, giota, N_GROUPS), axis=0, keepdims=True)
    g_w = 1.0 / jnp.sum(jnp.exp(gl - gmax), axis=0, keepdims=True)
    sel = jnp.zeros((EXPERTS_PER_GROUP, tm), F32)
    for g in range(N_GROUPS):
        sel = jnp.where(g_idx == g, logits[g * EXPERTS_PER_GROUP:(g + 1) * EXPERTS_PER_GROUP], sel)
    eiota = lax.broadcasted_iota(jnp.int32, (EXPERTS_PER_GROUP, tm), 0)
    e1 = jnp.max(sel, axis=0, keepdims=True)
    i1 = jnp.min(jnp.where(sel == e1, eiota, EXPERTS_PER_GROUP), axis=0, keepdims=True)
    sel2 = jnp.where(eiota == i1, -jnp.inf, sel)
    e2 = jnp.max(sel2, axis=0, keepdims=True)
    i2 = jnp.min(jnp.where(sel2 == e2, eiota, EXPERTS_PER_GROUP), axis=0, keepdims=True)
    r = jnp.exp(e2 - e1)
    w1 = g_w / (1.0 + r)
    eid_ref[0] = jnp.concatenate([g_idx * EXPERTS_PER_GROUP + i1, g_idx * EXPERTS_PER_GROUP + i2], axis=0)
    gate_ref[0] = jnp.concatenate([w1, w1 * r], axis=0)


def _outproj(oa, ob, x, wa, wb, g2, wrh, wrl, br, *, tm):
    bsz, n, d = x.shape
    full = lambda shape: pl.BlockSpec(shape, lambda b, i: (0,) * len(shape))
    tok = lambda w: pl.BlockSpec((1, tm, w), lambda b, i: (b, i, 0))
    lane = lambda: pl.BlockSpec((1, TOP_K, tm), lambda b, i: (b, 0, i))
    return pl.pallas_call(
        _outproj_kernel,
        grid=(bsz, n // tm),
        in_specs=[tok(WIDTH_A), tok(WIDTH_B), tok(d), full((WIDTH_A, d)), full((WIDTH_B, d)), full((1, d)),
                  full((ROUTER_ROWS, d)), full((ROUTER_ROWS, d)), full((ROUTER_ROWS, 1))],
        out_specs=(tok(d), pl.BlockSpec((1, tm * D_TILES, LANE), lambda b, i: (b, i, 0)), lane(), lane()),
        out_shape=(
            jax.ShapeDtypeStruct((bsz, n, d), F32),
            jax.ShapeDtypeStruct((bsz, n * D_TILES, LANE), F32),
            jax.ShapeDtypeStruct((bsz, TOP_K, n), jnp.int32),
            jax.ShapeDtypeStruct((bsz, TOP_K, n), F32),
        ),
        compiler_params=pltpu.CompilerParams(
            dimension_semantics=("arbitrary", "arbitrary"), vmem_limit_bytes=VMEM_LIMIT),
        name="outproj_router",
    )(oa, ob, x, wa, wb, g2, wrh, wrl, br)


def _expert_kernel(be_ref, nb_ref, cur_ref, nxt_ref, x_hbm, wg_ref, wu_ref, wd_ref, y_ref, xbuf, sems, *, rows):
    i = pl.program_id(0)
    n_steps = pl.num_programs(0)
    n_used = nb_ref[0]
    slot = lax.rem(i, 2)

    def row_copy(tile_row, slot_, r):
        src = x_hbm.at[pl.ds(pl.multiple_of(tile_row, D_TILES), D_TILES), :]
        return pltpu.make_async_copy(src, xbuf.at[slot_, pl.ds(r * D_TILES, D_TILES), :], sems.at[slot_])

    def start_rows(idx_ref, slot_):
        for r in range(rows):
            row_copy(idx_ref[0, 0, r], slot_, r).start(priority=r % 2)

    def wait_rows(slot_):
        for r in range(rows):
            row_copy(0, slot_, r).wait()

    @pl.when(i == 0)
    def _():
        start_rows(cur_ref, 0)

    @pl.when(i < n_used)
    def _():
        wait_rows(slot)
        x = jnp.concatenate([xbuf[slot, pl.ds(s, rows, stride=D_TILES), :] for s in range(D_TILES)],
                            axis=1).astype(BF16)
        gt = jnp.dot(x, wg_ref[0], preferred_element_type=F32)
        up = jnp.dot(x, wu_ref[0], preferred_element_type=F32)
        act = gt * (1.0 / (1.0 + jnp.exp(-gt))) * up
        y = jnp.dot(act.astype(BF16), wd_ref[0], preferred_element_type=F32).astype(BF16)
        start_rows(nxt_ref, 1 - slot)
        y_ref[...] = y

    @pl.when(i >= n_used)
    def _():
        y_ref[...] = jnp.zeros_like(y_ref)

    @pl.when(i == n_used)
    def _():
        wait_rows(slot)

    @pl.when(jnp.logical_and(i == n_steps - 1, i < n_used))
    def _():
        wait_rows(1 - slot)


def _experts(block_e, n_used, row_tok, x, wg, wu, wd, *, rows):
    d = D_MODEL
    n_blocks = block_e.shape[0]
    idx = (row_tok * D_TILES).reshape(n_blocks, 1, rows)
    smem_rows = lambda shift: pl.BlockSpec(
        (1, 1, rows), lambda i, be, nb: (jnp.minimum(i + shift, n_blocks - 1), 0, 0), memory_space=pltpu.SMEM)
    grid_spec = pltpu.PrefetchScalarGridSpec(
        num_scalar_prefetch=2,
        grid=(n_blocks,),
        in_specs=[
            smem_rows(0),
            smem_rows(1),
            pl.BlockSpec(memory_space=pl.ANY),
            pl.BlockSpec((1, d, D_EXPERT), lambda i, be, nb: (be[i], 0, 0)),
            pl.BlockSpec((1, d, D_EXPERT), lambda i, be, nb: (be[i], 0, 0)),
            pl.BlockSpec((1, D_EXPERT, d), lambda i, be, nb: (be[i], 0, 0)),
        ],
        out_specs=pl.BlockSpec((rows, d), lambda i, be, nb: (i, 0)),
        scratch_shapes=[pltpu.VMEM((2, rows * D_TILES, LANE), F32), pltpu.SemaphoreType.DMA((2,))],
    )
    return pl.pallas_call(
        functools.partial(_expert_kernel, rows=rows),
        grid_spec=grid_spec,
        out_shape=jax.ShapeDtypeStruct((n_blocks * rows, d), BF16),
        compiler_params=pltpu.CompilerParams(
            dimension_semantics=("arbitrary",), vmem_limit_bytes=VMEM_LIMIT),
        name="experts",
    )(block_e, n_used, idx, idx, x, wg, wu, wd)


def _final_kernel(h_ref, y0_ref, y1_ref, gate_ref, gf_ref, o_ref):
    gates = gate_ref[...]
    hres = (h_ref[...] + gates[:, 0:1] * y0_ref[...].astype(F32)
            + gates[:, 1:2] * y1_ref[...].astype(F32))
    ms = jnp.mean(hres * hres, axis=-1, keepdims=True)
    o_ref[...] = hres * lax.rsqrt(ms + EPS) * gf_ref[...]


def _final(h, y0, y1, gates, gf, *, tm):
    t, d = h.shape
    tok = lambda w: pl.BlockSpec((tm, w), lambda i: (i, 0))
    return pl.pallas_call(
        _final_kernel,
        grid=(t // tm,),
        in_specs=[tok(d), tok(d), tok(d), tok(TOP_K), pl.BlockSpec((1, d), lambda i: (0, 0))],
        out_specs=tok(d),
        out_shape=jax.ShapeDtypeStruct((t, d), F32),
        compiler_params=pltpu.CompilerParams(
            dimension_semantics=("arbitrary",), vmem_limit_bytes=VMEM_LIMIT),
        name="combine_final_norm",
    )(h, y0, y1, gates, gf)


def _rope_tables(n_tok):
    rows = n_tok // GRID_W
    row = jnp.repeat(jnp.arange(rows, dtype=F32), GRID_W)
    col = jnp.tile(jnp.arange(GRID_W, dtype=F32), rows)
    inv = ROPE_THETA ** (-jnp.arange(0, AXIS_DIM, 2, dtype=F32) / AXIS_DIM)
    ar = row[:, None] * inv
    ac = col[:, None] * inv
    ang = jnp.concatenate([ar, ar, ac, ac], axis=-1)
    q = HEAD_DIM // 4
    sign = jnp.concatenate([-jnp.ones((q,), F32), jnp.ones((q,), F32)] * 2)
    return jnp.cos(ang).T, (jnp.sin(ang) * sign).T


def _t5_bucket(rp):
    nb = N_BUCKETS // 2
    max_exact = nb // 2
    ret = (rp > 0).astype(jnp.int32) * nb
    n = jnp.abs(rp)
    large = max_exact + (jnp.log(jnp.maximum(n, 1).astype(F32) / max_exact)
                         / math.log(MAX_DISTANCE / max_exact) * (nb - max_exact)).astype(jnp.int32)
    large = jnp.minimum(large, nb - 1)
    return ret + jnp.where(n < max_exact, n, large)


def _bias_lookup(rel_bias, d):
    return (rel_bias.astype(F32) * LOG2E)[_t5_bucket(d)].T


def _band_tables(rel_bias, tq):
    assert MAX_DISTANCE <= LANE
    far = jnp.array([-2 * MAX_DISTANCE, 2 * MAX_DISTANCE], jnp.int32)
    c_far = _bias_lookup(rel_bias, far)
    rev = _bias_lookup(rel_bias, 2 * LANE - 1 - jnp.arange(4 * LANE - 1, dtype=jnp.int32))
    strip = jnp.stack([rev[:, LANE - 1 - rho:4 * LANE - 1 - rho] for rho in range(LANE)], axis=1)
    const = lambda col: jnp.broadcast_to(c_far[:, col, None, None], (N_HEADS_B, LANE, LANE))

    def block(k):
        if k <= -2 or k >= 2:
            return const(0 if k < 0 else 1)
        return strip[:, :, (1 - k) * LANE:(2 - k) * LANE]

    shift_blocks = BAND_SHIFT * KEY_CHUNK // LANE
    band = jnp.concatenate(
        [jnp.concatenate([block(a - b - shift_blocks) for b in range(tq // LANE)], axis=2)
         for a in range(BAND_ROWS // LANE)], axis=1)
    b_max = jnp.max(rel_bias.astype(F32) * LOG2E, axis=0)[:, None]
    return band, jnp.concatenate([c_far, b_max], axis=1)


def _meta_bias(rel_bias, n_tok):
    v = _bias_lookup(rel_bias, -jnp.arange(n_tok + N_META + 1, dtype=jnp.int32))
    mb = jnp.stack([v[:, N_META - j:N_META - j + n_tok] for j in range(N_META)], axis=1)
    pad = jnp.full((N_HEADS_B, META_PAD - N_META, n_tok), MASK_VALUE, F32)
    return jnp.concatenate([mb, pad], axis=1)


def _dispatch_plan(eid, rows):
    t = eid.shape[1]
    n_assign = TOP_K * t
    flat = eid.reshape(-1)
    a_idx = jnp.arange(n_assign, dtype=jnp.int32)
    skey = jnp.sort(flat * n_assign + a_idx)
    s_assign = skey % n_assign
    experts = jnp.arange(N_EXPERTS, dtype=jnp.int32)
    bounds = jnp.arange(N_EXPERTS + 1, dtype=jnp.int32) * n_assign
    off_all = jnp.sum((skey[:, None] < bounds[None, :]).astype(jnp.int32), axis=0)
    off = off_all[:-1]
    counts = off_all[1:] - off
    padded = (counts + rows - 1) // rows * rows
    pend = jnp.cumsum(padded)
    poff = pend - padded
    step = jnp.diff(poff - off, prepend=0)
    dest = a_idx + jnp.sum(jnp.where(a_idx[:, None] >= off[None, :], step[None, :], 0), axis=1)
    _, pos = lax.sort((s_assign, dest), num_keys=1)
    n_blocks = (n_assign + N_EXPERTS * (rows - 1) + rows - 1) // rows
    cap = n_blocks * rows
    block_start = jnp.arange(n_blocks, dtype=jnp.int32) * rows
    block_e = jnp.minimum(jnp.sum((block_start[:, None] >= pend[None, :]).astype(jnp.int32), axis=1),
                          N_EXPERTS - 1)
    d = jnp.arange(cap, dtype=jnp.int32)
    e_d = jnp.repeat(block_e, rows)
    onehot_d = e_d[:, None] == experts[None, :]
    pick = lambda v: jnp.sum(jnp.where(onehot_d, v[None, :], 0), axis=1)
    j = d - pick(poff)
    valid = j < pick(counts)
    src = jnp.clip(pick(off) + j, 0, n_assign - 1)
    row_tok = jnp.where(valid, s_assign[src] % t, 0)
    n_used = (pend[-1] // rows).astype(jnp.int32).reshape(1)
    return row_tok, block_e, n_used, pos.reshape(TOP_K, t)


def _key_bounds(knorm, knorm_meta):
    both = jnp.maximum(jnp.max(knorm, axis=1), knorm_meta[0, 0][None])
    return jnp.sqrt(both[:, 1:1 + N_KV_A, 0]), jnp.sqrt(both[:, 0, 0:2 * N_HEADS_B])


def _mixer(x, shared):
    bsz, n_tok, d = x.shape
    assert n_tok % Q_TILE == 0 and n_tok % TOKEN_TILE == 0
    tq = Q_TILE
    tm = TOKEN_TILE
    cos_t, sin_t = _rope_tables(n_tok)
    qa_t, ka, va_t, qb_t, kb, vb_t, knorm = _inproj(
        x, shared['g1'], shared['wt'], shared['wkb'], cos_t, sin_t, shared['qn'], shared['kn'],
        tm=tm, chunk=KEY_CHUNK)
    kmax_a, kmax_b = _key_bounds(knorm, shared['knorm_meta'])
    oa = _attn_a(kmax_a, qa_t, ka, va_t, shared['km_a'], shared['vm_a'], tq=tq)
    mbias = _meta_bias(shared['rel_bias'], n_tok)
    ob = _attn_b(shared['cfar'], kmax_b, qb_t, kb, vb_t, shared['km_b'], shared['vm_b'], shared['band'], mbias,
                 shared['lq1'], shared['lk1'], shared['lq2'], shared['lk2'], shared['sn'], tq=tq)
    h, n2, eid, gates = _outproj(oa, ob, x, shared['wa'], shared['wb'], shared['g2'],
                                 shared['wrh'], shared['wrl'], shared['br'], tm=tm)
    t = bsz * n_tok
    eid = jnp.transpose(eid, (1, 0, 2)).reshape(TOP_K, t)
    gates = jnp.transpose(gates, (1, 0, 2)).reshape(TOP_K, t)
    row_tok, block_e, n_used, pos = _dispatch_plan(eid, MOE_ROWS)
    return h.reshape(t, d), n2.reshape(t * D_TILES, LANE), row_tok, block_e, n_used, pos, gates.T


def _moe_and_norm(mixed, shape, shared):
    h, n2, row_tok, block_e, n_used, pos, gates = mixed
    y = _experts(block_e, n_used, row_tok, n2, shared['wg'], shared['wu'], shared['wd'], rows=MOE_ROWS)
    y0 = jnp.take(y, pos[0], axis=0)
    y1 = jnp.take(y, pos[1], axis=0)
    return _final(h, y0, y1, gates, shared['gf'], tm=TOKEN_TILE).reshape(shape)


def kernel(x_prompt, x_sample, meta_tokens, rel_bias, norm1, w_in, q_norm, k_norm, lambda_q1, lambda_k1,
           lambda_q2, lambda_k2, sub_norm, w_out, norm2, w_router_group, b_router_group, w_router_expert,
           b_router_expert, w_gate, w_up, w_down, final_norm):
    i = LAYER
    d = D_MODEL
    o0 = WIDTH_A
    o1 = o0 + KV_A
    o2 = o1 + KV_A
    o3 = o2 + WIDTH_B
    o4 = o3 + WIDTH_B
    w = w_in[i]
    wt = jnp.concatenate([w[:, :o2], w[:, o2:o3], w[:, o4:]], axis=1).T.astype(BF16)
    wr = jnp.concatenate([w_router_expert[i], w_router_group[i],
                          jnp.zeros((d, ROUTER_ROWS - N_EXPERTS - N_GROUPS), F32)], axis=1).T
    wrh = wr.astype(BF16)
    br = jnp.concatenate([b_router_expert[i], b_router_group[i],
                          jnp.zeros((ROUTER_ROWS - N_EXPERTS - N_GROUPS,), F32)]).reshape(ROUTER_ROWS, 1)
    shared = {
        'g1': norm1[i].reshape(1, d), 'wt': wt, 'wkb': w[:, o3:o4].astype(BF16),
        'qn': q_norm[i].reshape(HEAD_DIM, 1), 'kn': k_norm[i].reshape(HEAD_DIM, 1),
        'rel_bias': rel_bias,
        'lq1': lambda_q1[i].reshape(1, HEAD_DIM), 'lk1': lambda_k1[i].reshape(1, HEAD_DIM),
        'lq2': lambda_q2[i].reshape(1, HEAD_DIM), 'lk2': lambda_k2[i].reshape(1, HEAD_DIM),
        'sn': sub_norm[i].reshape(2 * HEAD_DIM, 1),
        'wa': w_out[i][:WIDTH_A].astype(BF16), 'wb': w_out[i][WIDTH_A:].astype(BF16),
        'g2': norm2[i].reshape(1, d), 'wrh': wrh, 'wrl': (wr - wrh.astype(F32)).astype(BF16), 'br': br,
        'wg': w_gate[i].astype(BF16), 'wu': w_up[i].astype(BF16), 'wd': w_down[i].astype(BF16),
        'gf': final_norm.reshape(1, d),
    }
    xm = jnp.concatenate([meta_tokens.astype(F32), jnp.zeros((META_PAD - N_META, d), F32)], axis=0)[None]
    ones = jnp.ones((HEAD_DIM, META_PAD), F32)
    _, km_a, vm_a, _, km_b, vm_b, knorm_meta = _inproj(
        xm, shared['g1'], shared['wt'], shared['wkb'], ones, 0.0 * ones, shared['qn'], shared['kn'],
        tm=META_PAD, chunk=META_PAD)
    shared['km_a'] = km_a[0]
    shared['vm_a'] = vm_a[0, 0]
    shared['km_b'] = km_b[0]
    shared['vm_b'] = vm_b[0, 0]
    shared['knorm_meta'] = knorm_meta
    shared['band'], shared['cfar'] = _band_tables(rel_bias, Q_TILE)
    mixed_prompt = _mixer(x_prompt, shared)
    mixed_sample = _mixer(x_sample, shared)
    return (_moe_and_norm(mixed_prompt, x_prompt.shape, shared),
            _moe_and_norm(mixed_sample, x_sample.shape, shared))
```

```python
import functools
import math

import jax
import jax.numpy as jnp
from jax import lax
from jax.experimental import pallas as pl
from jax.experimental.pallas import tpu as pltpu

D_MODEL = 1024
HEAD_DIM = 64
N_HEADS_A = 8
N_KV_A = 2
GQA_GROUP = N_HEADS_A // N_KV_A
N_HEADS_B = 4
AXIS_DIM = HEAD_DIM // 2
ROPE_THETA = 10000.0
GRID_W = 64
N_META = 16
N_BUCKETS = 32
MAX_DISTANCE = 128
WIDTH_A = N_HEADS_A * HEAD_DIM
WIDTH_B = N_HEADS_B * 2 * HEAD_DIM
KV_A = N_KV_A * HEAD_DIM
N_GROUPS = 4
EXPERTS_PER_GROUP = 8
N_EXPERTS = N_GROUPS * EXPERTS_PER_GROUP
TOP_K = 2
D_EXPERT = D_MODEL // 2
EPS = 1e-6
LAYER = 0
LAM_INIT = 0.8 - 0.6 * math.exp(-0.3 * LAYER)

LOG2E = 1.4426950408889634
Q_SCALE = (HEAD_DIM ** -0.5) * LOG2E
MASK_VALUE = -1e30

LANE = 128
BF16_ROWS = 16
META_PAD = LANE
KEY_CHUNK = 256
TOKEN_TILE = 512
Q_TILE = 512
MOE_ROWS = 512
ROUTER_ROWS = 40
VMEM_LIMIT = 48 * 1024 * 1024

VA_ROWS = HEAD_DIM + BF16_ROWS
VB_ROWS = 2 * HEAD_DIM + BF16_ROWS
TILE_CHUNKS = Q_TILE // KEY_CHUNK
NEAR_CHUNKS = TILE_CHUNKS + 2
BAND_SHIFT = 2
BAND_ROWS = (TILE_CHUNKS + 5) * KEY_CHUNK

BOUND_MARGIN = 1.01
L_MIN = 2.0 ** -64
D_TILES = D_MODEL // LANE
NORM_ROWS = 8
PART_LANES = 512
FAST_UNROLL = 4

F32 = jnp.float32
BF16 = jnp.bfloat16


def _nt_dot(a, b):
    return lax.dot_general(a, b, (((1,), (1,)), ((), ())), preferred_element_type=F32)


def _inproj_kernel(x_ref, g1_ref, wt_ref, wkb_ref, cos_ref, sin_ref, qn_ref, kn_ref,
                   qa_ref, ka_ref, va_ref, qb_ref, kb_ref, vb_ref, knorm_ref, *, chunk):
    x = x_ref[0]
    ms = jnp.mean(x * x, axis=-1, keepdims=True)
    n = (x * lax.rsqrt(ms + EPS) * g1_ref[...]).astype(BF16)
    pt = _nt_dot(wt_ref[...], n)
    kb = jnp.dot(n, wkb_ref[...], preferred_element_type=F32).astype(BF16)
    kb_ref[0] = kb
    kbf = kb.astype(F32)
    seg = (lax.broadcasted_iota(jnp.int32, (WIDTH_B, LANE), 0) // HEAD_DIM
           == lax.broadcasted_iota(jnp.int32, (WIDTH_B, LANE), 1)).astype(BF16)
    kb_sq = jnp.dot((kbf * kbf).astype(BF16), seg, preferred_element_type=F32)
    kb_max = jnp.max(kb_sq, axis=0, keepdims=True)
    cos = cos_ref[...]
    sin = sin_ref[...]
    q = HEAD_DIM // 4

    def norm_rope(blk, gain):
        ss = jnp.mean(blk * blk, axis=0, keepdims=True)
        y = blk * lax.rsqrt(ss + EPS) * gain
        swapped = jnp.concatenate([y[q:2 * q], y[0:q], y[3 * q:4 * q], y[2 * q:3 * q]], axis=0)
        return y * cos + swapped * sin

    qn = qn_ref[...]
    kn = kn_ref[...]
    for h in range(N_HEADS_A):
        blk = pt[h * HEAD_DIM:(h + 1) * HEAD_DIM]
        qa_ref[0, h * HEAD_DIM:(h + 1) * HEAD_DIM, :] = (norm_rope(blk, qn) * Q_SCALE).astype(BF16)
    o0 = WIDTH_A
    ka = jnp.concatenate(
        [norm_rope(pt[o0 + j * HEAD_DIM:o0 + (j + 1) * HEAD_DIM], kn) for j in range(N_KV_A)], axis=0)
    ka_ref[0] = ka.T.astype(BF16)
    kaf = ka.astype(BF16).astype(F32)
    ka_sq = kaf * kaf
    ka_max = [jnp.max(jnp.sum(ka_sq[g * HEAD_DIM:(g + 1) * HEAD_DIM], axis=0, keepdims=True),
                      axis=1, keepdims=True) for g in range(N_KV_A)]
    knorm_ref[0, 0] = jnp.concatenate(
        [kb_max] + [jnp.broadcast_to(v, (1, LANE)) for v in ka_max]
        + [jnp.zeros((NORM_ROWS - 1 - N_KV_A, LANE), F32)], axis=0)
    o1 = o0 + KV_A
    o2 = o1 + KV_A
    o3 = o2 + WIDTH_B
    tm = x.shape[0]
    vd = 2 * HEAD_DIM
    ones_rows = (lax.broadcasted_iota(jnp.int32, (BF16_ROWS, chunk), 0) == 0).astype(BF16)
    for j in range(tm // chunk):
        cols = slice(j * chunk, (j + 1) * chunk)
        for g in range(N_KV_A):
            va_ref[0, j, g * VA_ROWS:g * VA_ROWS + HEAD_DIM] = (
                pt[o1 + g * HEAD_DIM:o1 + (g + 1) * HEAD_DIM, cols].astype(BF16))
            va_ref[0, j, g * VA_ROWS + HEAD_DIM:(g + 1) * VA_ROWS] = ones_rows
        for h in range(N_HEADS_B):
            vb_ref[0, j, h * VB_ROWS:h * VB_ROWS + vd] = pt[o3 + h * vd:o3 + (h + 1) * vd, cols].astype(BF16)
            vb_ref[0, j, h * VB_ROWS + vd:(h + 1) * VB_ROWS] = ones_rows
    qb_ref[0] = (pt[o2:o3] * Q_SCALE).astype(BF16)


def _inproj(x, g1, wt, wkb, cos_t, sin_t, qn, kn, *, tm, chunk):
    bsz, n, d = x.shape
    rows_t = wt.shape[0]
    grid = (bsz, n // tm)
    full = lambda shape: pl.BlockSpec(shape, lambda b, i: (0,) * len(shape))
    va_rows = N_KV_A * VA_ROWS
    vb_rows = N_HEADS_B * VB_ROWS
    out_shape = (
        jax.ShapeDtypeStruct((bsz, WIDTH_A, n), BF16),
        jax.ShapeDtypeStruct((bsz, n, KV_A), BF16),
        jax.ShapeDtypeStruct((bsz, n // chunk, va_rows, chunk), BF16),
        jax.ShapeDtypeStruct((bsz, WIDTH_B, n), BF16),
        jax.ShapeDtypeStruct((bsz, n, WIDTH_B), BF16),
        jax.ShapeDtypeStruct((bsz, n // chunk, vb_rows, chunk), BF16),
        jax.ShapeDtypeStruct((bsz, n // tm, NORM_ROWS, LANE), F32),
    )
    return pl.pallas_call(
        functools.partial(_inproj_kernel, chunk=chunk),
        grid=grid,
        in_specs=[
            pl.BlockSpec((1, tm, d), lambda b, i: (b, i, 0)),
            full((1, d)),
            full((rows_t, d)),
            full((d, WIDTH_B)),
            pl.BlockSpec((HEAD_DIM, tm), lambda b, i: (0, i)),
            pl.BlockSpec((HEAD_DIM, tm), lambda b, i: (0, i)),
            full((HEAD_DIM, 1)),
            full((HEAD_DIM, 1)),
        ],
        out_specs=(
            pl.BlockSpec((1, WIDTH_A, tm), lambda b, i: (b, 0, i)),
            pl.BlockSpec((1, tm, KV_A), lambda b, i: (b, i, 0)),
            pl.BlockSpec((1, tm // chunk, va_rows, chunk), lambda b, i: (b, i, 0, 0)),
            pl.BlockSpec((1, WIDTH_B, tm), lambda b, i: (b, 0, i)),
            pl.BlockSpec((1, tm, WIDTH_B), lambda b, i: (b, i, 0)),
            pl.BlockSpec((1, tm // chunk, vb_rows, chunk), lambda b, i: (b, i, 0, 0)),
            pl.BlockSpec((1, 1, NORM_ROWS, LANE), lambda b, i: (b, i, 0, 0)),
        ),
        out_shape=out_shape,
        compiler_params=pltpu.CompilerParams(
            dimension_semantics=("arbitrary", "arbitrary"), vmem_limit_bytes=VMEM_LIMIT),
        name="inproj",
    )(x, g1, wt, wkb, cos_t, sin_t, qn, kn)


def _online_update(s, c_shift, m_ref, acc_ref, vblk, m_idx, acc_idx):
    mc = jnp.max(s, axis=0, keepdims=True)
    if c_shift is not None:
        mc = mc + c_shift
    m_old = m_ref[m_idx]
    m_new = jnp.maximum(m_old, mc)
    alpha = jnp.exp2(m_old - m_new)
    shift = m_new if c_shift is None else m_new - c_shift
    p = jnp.exp2(s - shift).astype(BF16)
    pv = jnp.dot(vblk, p, preferred_element_type=F32)
    acc_ref[acc_idx] = alpha * acc_ref[acc_idx] + pv
    m_ref[m_idx] = m_new


def _init_from_meta(s, m_ref, acc_ref, vmeta, idx):
    m0 = jnp.max(s, axis=0, keepdims=True)
    p = jnp.exp2(s - m0).astype(BF16)
    m_ref[idx] = m0
    acc_ref[idx] = jnp.dot(vmeta, p, preferred_element_type=F32)


def _bounded_update(s, shift, acc_ref, vblk, idx, first=False):
    p = jnp.exp2(s - shift).astype(BF16)
    pv = jnp.dot(vblk, p, preferred_element_type=F32)
    acc_ref[idx] = pv if first else acc_ref[idx] + pv


def _part(p, tq):
    c0 = p * PART_LANES
    return c0 // tq, slice(c0 % tq, c0 % tq + PART_LANES), slice(c0, c0 + PART_LANES)


def _query_norm(q_bf16):
    qf = q_bf16.astype(F32)
    return jnp.sqrt(jnp.sum(qf * qf, axis=0, keepdims=True))


def _pipelined_chunks(count, parts, chunk_of, qk_into, consume, sa_ref, sb_ref, unroll=2):
    assert unroll % 2 == 0 and count >= unroll and count % unroll == 0
    bufs = (sa_ref, sb_ref)

    def step(u, nxt, cur):
        for part in range(parts):
            if nxt is not None:
                qk_into(bufs[(u + 1) % 2], nxt, part)
            consume(bufs[u % 2], cur, part)

    for part in range(parts):
        qk_into(sa_ref, chunk_of(0), part)

    def body(j, carry):
        i = unroll * j
        for u in range(unroll):
            step(u, chunk_of(i + u + 1), chunk_of(i + u))
        return carry

    lax.fori_loop(0, count // unroll - 1, body, 0)
    i = count - unroll
    for u in range(unroll):
        step(u, chunk_of(i + u + 1) if u + 1 < unroll else None, chunk_of(i + u))


def _attn_a_kernel(kmax_ref, q_ref, k_ref, v_ref, km_ref, vm_ref, o_ref, qpad_ref, sa_ref, sb_ref, m_ref,
                   acc_ref, *, tq, nk):
    b = pl.program_id(0)
    g = pl.program_id(1)
    kmax = kmax_ref[b, g] * BOUND_MARGIN
    zero = jnp.zeros((HEAD_DIM, tq), BF16)
    for h in range(GQA_GROUP):
        qh = q_ref[0, h * HEAD_DIM:(h + 1) * HEAD_DIM, :]
        qpad_ref[0:HEAD_DIM, h * tq:(h + 1) * tq] = jnp.where(g == 0, qh, zero)
        qpad_ref[HEAD_DIM:2 * HEAD_DIM, h * tq:(h + 1) * tq] = jnp.where(g == 1, qh, zero)
        m_ref[h] = _query_norm(qh) * kmax

    valid = lax.broadcasted_iota(jnp.int32, (META_PAD, 1), 0) < N_META

    def meta_scores():
        s = jnp.dot(km_ref[...], qpad_ref[...], preferred_element_type=F32)
        return jnp.where(valid, s, MASK_VALUE)

    parts = GQA_GROUP * tq // PART_LANES

    def qk_into(dst, kc, p):
        cols = _part(p, tq)[2]
        dst[:, cols] = jnp.dot(k_ref[0, kc], qpad_ref[:, cols], preferred_element_type=F32)

    s = meta_scores()
    for h in range(GQA_GROUP):
        _bounded_update(s[:, h * tq:(h + 1) * tq], m_ref[h], acc_ref, vm_ref[...], h, first=True)

    def consume(src, kc, p):
        h, lanes, cols = _part(p, tq)
        _bounded_update(src[:, cols], m_ref[h, :, lanes], acc_ref, v_ref[0, kc], (h, slice(None), lanes))

    _pipelined_chunks(nk, parts, lambda i: i, qk_into, consume, sa_ref, sb_ref, unroll=2 * FAST_UNROLL)

    l_min = jnp.min(acc_ref[0, HEAD_DIM:HEAD_DIM + 1, :])
    for h in range(1, GQA_GROUP):
        l_min = jnp.minimum(l_min, jnp.min(acc_ref[h, HEAD_DIM:HEAD_DIM + 1, :]))

    @pl.when(jnp.logical_not(l_min >= L_MIN))
    def _():
        s = meta_scores()
        for h in range(GQA_GROUP):
            _init_from_meta(s[:, h * tq:(h + 1) * tq], m_ref, acc_ref, vm_ref[...], h)

        def consume_exact(src, kc, p):
            h, lanes, cols = _part(p, tq)
            _online_update(src[:, cols], None, m_ref, acc_ref, v_ref[0, kc],
                           (h, slice(None), lanes), (h, slice(None), lanes))

        _pipelined_chunks(nk, parts, lambda i: i, qk_into, consume_exact, sa_ref, sb_ref)

    outs = []
    for h in range(GQA_GROUP):
        outs.append(acc_ref[h, 0:HEAD_DIM, :] / acc_ref[h, HEAD_DIM:HEAD_DIM + 1, :])
    o = jnp.concatenate(outs, axis=0)
    o_ref[0] = o.T.astype(BF16)


def _attn_a(kmax, qa_t, ka, va_t, km, vm_t, *, tq):
    bsz, _, n = qa_t.shape
    nk = n // KEY_CHUNK
    ka4 = ka.reshape(bsz, nk, KEY_CHUNK, KV_A)
    gw = GQA_GROUP * HEAD_DIM
    return pl.pallas_call(
        functools.partial(_attn_a_kernel, tq=tq, nk=nk),
        grid=(bsz, N_KV_A, n // tq),
        in_specs=[
            pl.BlockSpec(memory_space=pltpu.SMEM),
            pl.BlockSpec((1, gw, tq), lambda b, g, i: (b, g, i)),
            pl.BlockSpec((1, nk, KEY_CHUNK, KV_A), lambda b, g, i: (b, 0, 0, 0)),
            pl.BlockSpec((1, nk, VA_ROWS, KEY_CHUNK), lambda b, g, i: (b, 0, g, 0)),
            pl.BlockSpec((META_PAD, KV_A), lambda b, g, i: (0, 0)),
            pl.BlockSpec((VA_ROWS, META_PAD), lambda b, g, i: (g, 0)),
        ],
        out_specs=pl.BlockSpec((1, tq, gw), lambda b, g, i: (b, i, g)),
        out_shape=jax.ShapeDtypeStruct((bsz, n, WIDTH_A), BF16),
        scratch_shapes=[
            pltpu.VMEM((KV_A, GQA_GROUP * tq), BF16),
            pltpu.VMEM((KEY_CHUNK, GQA_GROUP * tq), F32),
            pltpu.VMEM((KEY_CHUNK, GQA_GROUP * tq), F32),
            pltpu.VMEM((GQA_GROUP, 1, tq), F32),
            pltpu.VMEM((GQA_GROUP, VA_ROWS, tq), F32),
        ],
        compiler_params=pltpu.CompilerParams(
            dimension_semantics=("arbitrary", "arbitrary", "arbitrary"), vmem_limit_bytes=VMEM_LIMIT),
        name="attn_a",
    )(kmax, qa_t, ka4, va_t, km, vm_t)


def _attn_b_kernel(cfar_ref, kmax_ref, q_ref, k_ref, v_ref, km_ref, vm_ref, band_ref, mbias_ref,
                   lq1_ref, lk1_ref, lq2_ref, lk2_ref, sn_ref, o_ref,
                   qblk_ref, sa_ref, sb_ref, m_ref, acc_ref, *, tq, nk):
    b = pl.program_id(0)
    h = pl.program_id(1)
    qi = pl.program_id(2)
    vd = 2 * HEAD_DIM
    zero = jnp.zeros((HEAD_DIM, tq), BF16)
    q1 = q_ref[0, 0:HEAD_DIM, :]
    q2 = q_ref[0, HEAD_DIM:vd, :]
    qblk_ref[0:HEAD_DIM, 0:tq] = q1
    qblk_ref[0:HEAD_DIM, tq:2 * tq] = zero
    qblk_ref[HEAD_DIM:vd, 0:tq] = zero
    qblk_ref[HEAD_DIM:vd, tq:2 * tq] = q2
    c_neg = cfar_ref[h, 0]
    c_pos = cfar_ref[h, 1]
    b_max = cfar_ref[h, 2]
    for j, qj in enumerate((q1, q2)):
        m_ref[j] = _query_norm(qj) * (kmax_ref[b, 2 * h + j] * BOUND_MARGIN) + b_max

    def meta_scores():
        return jnp.dot(km_ref[...], qblk_ref[...], preferred_element_type=F32)

    parts = 2 * tq // PART_LANES

    def qk_into(dst, info, p):
        cols = _part(p, tq)[2]
        dst[:, cols] = jnp.dot(k_ref[0, info[0]], qblk_ref[:, cols], preferred_element_type=F32)

    near_lo = jnp.clip(qi * TILE_CHUNKS - 1, 0, nk - NEAR_CHUNKS)

    def far_chunk(i):
        right = i >= near_lo
        return jnp.where(right, i + NEAR_CHUNKS, i), jnp.where(right, c_pos, c_neg)

    def near_chunk(i):
        kc = near_lo + i
        return kc, pl.multiple_of((kc - qi * TILE_CHUNKS + BAND_SHIFT) * KEY_CHUNK, KEY_CHUNK)

    s = meta_scores()
    for j in range(2):
        _bounded_update(s[:, j * tq:(j + 1) * tq] + mbias_ref[0], m_ref[j], acc_ref, vm_ref[...], j, first=True)

    def consume_far(src, info, p):
        kc, c = info
        j, lanes, cols = _part(p, tq)
        _bounded_update(src[:, cols], m_ref[j, :, lanes] - c, acc_ref, v_ref[0, kc],
                        (j, slice(None), lanes))

    def consume_near(src, info, p):
        kc, r0 = info
        j, lanes, cols = _part(p, tq)
        bias = band_ref[0, pl.ds(r0, KEY_CHUNK), lanes]
        _bounded_update(src[:, cols] + bias, m_ref[j, :, lanes], acc_ref, v_ref[0, kc],
                        (j, slice(None), lanes))

    _pipelined_chunks(nk - NEAR_CHUNKS, parts, far_chunk, qk_into, consume_far, sa_ref, sb_ref,
                      unroll=FAST_UNROLL)
    _pipelined_chunks(NEAR_CHUNKS, parts, near_chunk, qk_into, consume_near, sa_ref, sb_ref,
                      unroll=FAST_UNROLL)

    l_min = jnp.minimum(jnp.min(acc_ref[0, vd:vd + 1, :]), jnp.min(acc_ref[1, vd:vd + 1, :]))

    @pl.when(jnp.logical_not(l_min >= L_MIN))
    def _():
        s = meta_scores()
        for j in range(2):
            _init_from_meta(s[:, j * tq:(j + 1) * tq] + mbias_ref[0], m_ref, acc_ref, vm_ref[...], j)

        def exact_far(src, info, p):
            kc, c = info
            j, lanes, cols = _part(p, tq)
            _online_update(src[:, cols], c, m_ref, acc_ref, v_ref[0, kc],
                           (j, slice(None), lanes), (j, slice(None), lanes))

        def exact_near(src, info, p):
            kc, r0 = info
            j, lanes, cols = _part(p, tq)
            bias = band_ref[0, pl.ds(r0, KEY_CHUNK), lanes]
            _online_update(src[:, cols] + bias, None, m_ref, acc_ref, v_ref[0, kc],
                           (j, slice(None), lanes), (j, slice(None), lanes))

        _pipelined_chunks(nk - NEAR_CHUNKS, parts, far_chunk, qk_into, exact_far, sa_ref, sb_ref)
        _pipelined_chunks(NEAR_CHUNKS, parts, near_chunk, qk_into, exact_near, sa_ref, sb_ref)

    lam =(jnp.exp(jnp.sum(lq1_ref[...] * lk1_ref[...], axis=-1, keepdims=True))
           - jnp.exp(jnp.sum(lq2_ref[...] * lk2_ref[...], axis=-1, keepdims=True)) + LAM_INIT)
    o = (acc_ref[0, 0:vd, :] / acc_ref[0, vd:vd + 1, :]
         - lam * (acc_ref[1, 0:vd, :] / acc_ref[1, vd:vd + 1, :]))
    ms = jnp.mean(o * o, axis=0, keepdims=True)
    o = o * lax.rsqrt(ms + EPS) * sn_ref[...] * (1.0 - LAM_INIT)
    o_ref[0] = o.T.astype(BF16)


def _attn_b(cfar, kmax, qb_t, kb, vb_t, kmb, vmb_t, band, mbias, lq1, lk1, lq2, lk2, sn, *, tq):
    bsz, _, n = qb_t.shape
    nk = n // KEY_CHUNK
    assert tq == Q_TILE and NEAR_CHUNKS % 2 == 0 and (nk - NEAR_CHUNKS) % 2 == 0 and nk - NEAR_CHUNKS >= 2
    vd = 2 * HEAD_DIM
    kb4 = kb.reshape(bsz, nk, KEY_CHUNK, WIDTH_B)
    vec = lambda: pl.BlockSpec((1, HEAD_DIM), lambda b, h, i: (0, 0))
    return pl.pallas_call(
        functools.partial(_attn_b_kernel, tq=tq, nk=nk),
        grid=(bsz, N_HEADS_B, n // tq),
        in_specs=[
            pl.BlockSpec(memory_space=pltpu.SMEM),
            pl.BlockSpec(memory_space=pltpu.SMEM),
            pl.BlockSpec((1, vd, tq), lambda b, h, i: (b, h, i)),
            pl.BlockSpec((1, nk, KEY_CHUNK, vd), lambda b, h, i: (b, 0, 0, h)),
            pl.BlockSpec((1, nk, VB_ROWS, KEY_CHUNK), lambda b, h, i: (b, 0, h, 0)),
            pl.BlockSpec((META_PAD, vd), lambda b, h, i: (0, h)),
            pl.BlockSpec((VB_ROWS, META_PAD), lambda b, h, i: (h, 0)),
            pl.BlockSpec((1, BAND_ROWS, tq), lambda b, h, i: (h, 0, 0)),
            pl.BlockSpec((1, META_PAD, tq), lambda b, h, i: (h, 0, i)),
            vec(), vec(), vec(), vec(),
            pl.BlockSpec((vd, 1), lambda b, h, i: (0, 0)),
        ],
        out_specs=pl.BlockSpec((1, tq, vd), lambda b, h, i: (b, i, h)),
        out_shape=jax.ShapeDtypeStruct((bsz, n, WIDTH_B), BF16),
        scratch_shapes=[
            pltpu.VMEM((vd, 2 * tq), BF16),
            pltpu.VMEM((KEY_CHUNK, 2 * tq), F32),
            pltpu.VMEM((KEY_CHUNK, 2 * tq), F32),
            pltpu.VMEM((2, 1, tq), F32),
            pltpu.VMEM((2, VB_ROWS, tq), F32),
        ],
        compiler_params=pltpu.CompilerParams(
            dimension_semantics=("arbitrary", "arbitrary", "arbitrary"), vmem_limit_bytes=VMEM_LIMIT),
        name="attn_b",
    )(cfar, kmax, qb_t, kb4, vb_t, kmb, vmb_t, band, mbias, lq1, lk1, lq2, lk2, sn)


def _outproj_kernel(oa_ref, ob_ref, x_ref, wa_ref, wb_ref, g2_ref, wrh_ref, wrl_ref, br_ref,
                    h_ref, n2_ref, eid_ref, gate_ref):
    att = (jnp.dot(oa_ref[0], wa_ref[...], preferred_element_type=F32)
           + jnp.dot(ob_ref[0], wb_ref[...], preferred_element_type=F32))
    hres = x_ref[0] + att
    h_ref[0] = hres
    ms = jnp.mean(hres * hres, axis=-1, keepdims=True)
    n2 = hres * lax.rsqrt(ms + EPS) * g2_ref[...]
    tm = hres.shape[0]
    n2_hi = n2.astype(BF16)
    n2_lo = (n2 - n2_hi.astype(F32)).astype(BF16)
    for s in range(D_TILES):
        n2_ref[0, pl.ds(s, tm, stride=D_TILES), :] = n2[:, s * LANE:(s + 1) * LANE]
    wrh = wrh_ref[...]
    logits = _nt_dot(wrh, n2_hi) + _nt_dot(wrh, n2_lo) + _nt_dot(wrl_ref[...], n2_hi) + br_ref[...]
    gl = logits[N_EXPERTS:N_EXPERTS + N_GROUPS]
    gmax = jnp.max(gl, axis=0, keepdims=True)
    giota = lax.broadcasted_iota(jnp.int32, (N_GROUPS, tm), 0)
    g_idx = jnp.min(jnp.where(gl == gmax, giota, N_GROUPS), axis=0, keepdims=True)
    g_w = 1.0 / jnp.sum(jnp.exp(gl - gmax), axis=0, keepdims=True)
    sel = jnp.zeros((EXPERTS_PER_GROUP, tm), F32)
    for g in range(N_GROUPS):
        sel = jnp.where(g_idx == g, logits[g * EXPERTS_PER_GROUP:(g + 1) * EXPERTS_PER_GROUP], sel)
    eiota = lax.broadcasted_iota(jnp.int32, (EXPERTS_PER_GROUP, tm), 0)
    e1 = jnp.max(sel, axis=0, keepdims=True)
    i1 = jnp.min(jnp.where(sel == e1, eiota, EXPERTS_PER_GROUP), axis=0, keepdims=True)
    sel2 = jnp.where(eiota == i1, -jnp.inf, sel)
    e2 = jnp.max(sel2, axis=0, keepdims=True)
    i2 = jnp.min(jnp.where(sel2 == e2, eiota, EXPERTS_PER_GROUP), axis=0, keepdims=True)
    r = jnp.exp(e2 - e1)
    w1 = g_w / (1.0 + r)
    eid_ref[0] = jnp.concatenate([g_idx * EXPERTS_PER_GROUP + i1, g_idx * EXPERTS_PER_GROUP + i2], axis=0)
    gate_ref[0] = jnp.concatenate([w1, w1 * r], axis=0)


def _outproj(oa, ob, x, wa, wb, g2, wrh, wrl, br, *, tm):
    bsz, n, d = x.shape
    full = lambda shape: pl.BlockSpec(shape, lambda b, i: (0,) * len(shape))
    tok = lambda w: pl.BlockSpec((1, tm, w), lambda b, i: (b, i, 0))
    lane = lambda: pl.BlockSpec((1, TOP_K, tm), lambda b, i: (b, 0, i))
    return pl.pallas_call(
        _outproj_kernel,
        grid=(bsz, n // tm),
        in_specs=[tok(WIDTH_A), tok(WIDTH_B), tok(d), full((WIDTH_A, d)), full((WIDTH_B, d)), full((1, d)),
                  full((ROUTER_ROWS, d)), full((ROUTER_ROWS, d)), full((ROUTER_ROWS, 1))],
        out_specs=(tok(d), pl.BlockSpec((1, tm * D_TILES, LANE), lambda b, i: (b, i, 0)), lane(), lane()),
        out_shape=(
            jax.ShapeDtypeStruct((bsz, n, d), F32),
            jax.ShapeDtypeStruct((bsz, n * D_TILES, LANE), F32),
            jax.ShapeDtypeStruct((bsz, TOP_K, n), jnp.int32),
            jax.ShapeDtypeStruct((bsz, TOP_K, n), F32),
        ),
        compiler_params=pltpu.CompilerParams(
            dimension_semantics=("arbitrary", "arbitrary"), vmem_limit_bytes=VMEM_LIMIT),
        name="outproj_router",
    )(oa, ob, x, wa, wb, g2, wrh, wrl, br)


def _expert_kernel(be_ref, nb_ref, cur_ref, nxt_ref, x_hbm, wg_ref, wu_ref, wd_ref, y_ref, xbuf, sems, *, rows):
    i = pl.program_id(0)
    n_steps = pl.num_programs(0)
    n_used = nb_ref[0]
    slot = lax.rem(i, 2)

    def row_copy(tile_row, slot_, r):
        src = x_hbm.at[pl.ds(pl.multiple_of(tile_row, D_TILES), D_TILES), :]
        return pltpu.make_async_copy(src, xbuf.at[slot_, pl.ds(r * D_TILES, D_TILES), :], sems.at[slot_])

    def start_rows(idx_ref, slot_):
        for r in range(rows):
            row_copy(idx_ref[0, 0, r], slot_, r).start(priority=r % 2)

    def wait_rows(slot_):
        for r in range(rows):
            row_copy(0, slot_, r).wait()

    @pl.when(i == 0)
    def _():
        start_rows(cur_ref, 0)

    @pl.when(i < n_used)
    def _():
        wait_rows(slot)
        x = jnp.concatenate([xbuf[slot, pl.ds(s, rows, stride=D_TILES), :] for s in range(D_TILES)],
                            axis=1).astype(BF16)
        gt = jnp.dot(x, wg_ref[0], preferred_element_type=F32)
        up = jnp.dot(x, wu_ref[0], preferred_element_type=F32)
        act = gt * (1.0 / (1.0 + jnp.exp(-gt))) * up
        y = jnp.dot(act.astype(BF16), wd_ref[0], preferred_element_type=F32).astype(BF16)
        start_rows(nxt_ref, 1 - slot)
        y_ref[...] = y

    @pl.when(i >= n_used)
    def _():
        y_ref[...] = jnp.zeros_like(y_ref)

    @pl.when(i == n_used)
    def _():
        wait_rows(slot)

    @pl.when(jnp.logical_and(i == n_steps - 1, i < n_used))
    def _():
        wait_rows(1 - slot)


def _experts(block_e, n_used, row_tok, x, wg, wu, wd, *, rows):
    d = D_MODEL
    n_blocks = block_e.shape[0]
    idx = (row_tok * D_TILES).reshape(n_blocks, 1, rows)
    smem_rows = lambda shift: pl.BlockSpec(
        (1, 1, rows), lambda i, be, nb: (jnp.minimum(i + shift, n_blocks - 1), 0, 0), memory_space=pltpu.SMEM)
    grid_spec = pltpu.PrefetchScalarGridSpec(
        num_scalar_prefetch=2,
        grid=(n_blocks,),
        in_specs=[
            smem_rows(0),
            smem_rows(1),
            pl.BlockSpec(memory_space=pl.ANY),
            pl.BlockSpec((1, d, D_EXPERT), lambda i, be, nb: (be[i], 0, 0)),
            pl.BlockSpec((1, d, D_EXPERT), lambda i, be, nb: (be[i], 0, 0)),
            pl.BlockSpec((1, D_EXPERT, d), lambda i, be, nb: (be[i], 0, 0)),
        ],
        out_specs=pl.BlockSpec((rows, d), lambda i, be, nb: (i, 0)),
        scratch_shapes=[pltpu.VMEM((2, rows * D_TILES, LANE), F32), pltpu.SemaphoreType.DMA((2,))],
    )
    return pl.pallas_call(
        functools.partial(_expert_kernel, rows=rows),
        grid_spec=grid_spec,
        out_shape=jax.ShapeDtypeStruct((n_blocks * rows, d), BF16),
        compiler_params=pltpu.CompilerParams(
            dimension_semantics=("arbitrary",), vmem_limit_bytes=VMEM_LIMIT),
        name="experts",
    )(block_e, n_used, idx, idx, x, wg, wu, wd)


def _final_kernel(h_ref, y0_ref, y1_ref, gate_ref, gf_ref, o_ref):
    gates = gate_ref[...]
    hres = (h_ref[...] + gates[:, 0:1] * y0_ref[...].astype(F32)
            + gates[:, 1:2] * y1_ref[...].astype(F32))
    ms = jnp.mean(hres * hres, axis=-1, keepdims=True)
    o_ref[...] = hres * lax.rsqrt(ms + EPS) * gf_ref[...]


def _final(h, y0, y1, gates, gf, *, tm):
    t, d = h.shape
    tok = lambda w: pl.BlockSpec((tm, w), lambda i: (i, 0))
    return pl.pallas_call(
        _final_kernel,
        grid=(t // tm,),
        in_specs=[tok(d), tok(d), tok(d), tok(TOP_K), pl.BlockSpec((1, d), lambda i: (0, 0))],
        out_specs=tok(d),
        out_shape=jax.ShapeDtypeStruct((t, d), F32),
        compiler_params=pltpu.CompilerParams(
            dimension_semantics=("arbitrary",), vmem_limit_bytes=VMEM_LIMIT),
        name="combine_final_norm",
    )(h, y0, y1, gates, gf)


def _rope_tables(n_tok):
    rows = n_tok // GRID_W
    row = jnp.repeat(jnp.arange(rows, dtype=F32), GRID_W)
    col = jnp.tile(jnp.arange(GRID_W, dtype=F32), rows)
    inv = ROPE_THETA ** (-jnp.arange(0, AXIS_DIM, 2, dtype=F32) / AXIS_DIM)
    ar = row[:, None] * inv
    ac = col[:, None] * inv
    ang = jnp.concatenate([ar, ar, ac, ac], axis=-1)
    q = HEAD_DIM // 4
    sign = jnp.concatenate([-jnp.ones((q,), F32), jnp.ones((q,), F32)] * 2)
    return jnp.cos(ang).T, (jnp.sin(ang) * sign).T


def _t5_bucket(rp):
    nb = N_BUCKETS // 2
    max_exact = nb // 2
    ret = (rp > 0).astype(jnp.int32) * nb
    n = jnp.abs(rp)
    large = max_exact + (jnp.log(jnp.maximum(n, 1).astype(F32) / max_exact)
                         / math.log(MAX_DISTANCE / max_exact) * (nb - max_exact)).astype(jnp.int32)
    large = jnp.minimum(large, nb - 1)
    return ret + jnp.where(n < max_exact, n, large)


def _bias_lookup(rel_bias, d):
    return (rel_bias.astype(F32) * LOG2E)[_t5_bucket(d)].T


def _band_tables(rel_bias, tq):
    assert MAX_DISTANCE <= LANE
    far = jnp.array([-2 * MAX_DISTANCE, 2 * MAX_DISTANCE], jnp.int32)
    c_far = _bias_lookup(rel_bias, far)
    rev = _bias_lookup(rel_bias, 2 * LANE - 1 - jnp.arange(4 * LANE - 1, dtype=jnp.int32))
    strip = jnp.stack([rev[:, LANE - 1 - rho:4 * LANE - 1 - rho] for rho in range(LANE)], axis=1)
    const = lambda col: jnp.broadcast_to(c_far[:, col, None, None], (N_HEADS_B, LANE, LANE))

    def block(k):
        if k <= -2 or k >= 2:
            return const(0 if k < 0 else 1)
        return strip[:, :, (1 - k) * LANE:(2 - k) * LANE]

    shift_blocks = BAND_SHIFT * KEY_CHUNK // LANE
    band = jnp.concatenate(
        [jnp.concatenate([block(a - b - shift_blocks) for b in range(tq // LANE)], axis=2)
         for a in range(BAND_ROWS // LANE)], axis=1)
    b_max = jnp.max(rel_bias.astype(F32) * LOG2E, axis=0)[:, None]
    return band, jnp.concatenate([c_far, b_max], axis=1)


def _meta_bias(rel_bias, n_tok):
    v = _bias_lookup(rel_bias, -jnp.arange(n_tok + N_META + 1, dtype=jnp.int32))
    mb = jnp.stack([v[:, N_META - j:N_META - j + n_tok] for j in range(N_META)], axis=1)
    pad = jnp.full((N_HEADS_B, META_PAD - N_META, n_tok), MASK_VALUE, F32)
    return jnp.concatenate([mb, pad], axis=1)


def _dispatch_plan(eid, rows):
    t = eid.shape[1]
    n_assign = TOP_K * t
    flat = eid.reshape(-1)
    a_idx = jnp.arange(n_assign, dtype=jnp.int32)
    skey = jnp.sort(flat * n_assign + a_idx)
    s_assign = skey % n_assign
    experts = jnp.arange(N_EXPERTS, dtype=jnp.int32)
    bounds = jnp.arange(N_EXPERTS + 1, dtype=jnp.int32) * n_assign
    off_all = jnp.sum((skey[:, None] < bounds[None, :]).astype(jnp.int32), axis=0)
    off = off_all[:-1]
    counts = off_all[1:] - off
    padded = (counts + rows - 1) // rows * rows
    pend = jnp.cumsum(padded)
    poff = pend - padded
    step = jnp.diff(poff - off, prepend=0)
    dest = a_idx + jnp.sum(jnp.where(a_idx[:, None] >= off[None, :], step[None, :], 0), axis=1)
    _, pos = lax.sort((s_assign, dest), num_keys=1)
    n_blocks = (n_assign + N_EXPERTS * (rows - 1) + rows - 1) // rows
    cap = n_blocks * rows
    block_start = jnp.arange(n_blocks, dtype=jnp.int32) * rows
    block_e = jnp.minimum(jnp.sum((block_start[:, None] >= pend[None, :]).astype(jnp.int32), axis=1),
                          N_EXPERTS - 1)
    d = jnp.arange(cap, dtype=jnp.int32)
    e_d = jnp.repeat(block_e, rows)
    onehot_d = e_d[:, None] == experts[None, :]
    pick = lambda v: jnp.sum(jnp.where(onehot_d, v[None, :], 0), axis=1)
    j = d - pick(poff)
    valid = j < pick(counts)
    src = jnp.clip(pick(off) + j, 0, n_assign - 1)
    row_tok = jnp.where(valid, s_assign[src] % t, 0)
    n_used = (pend[-1] // rows).astype(jnp.int32).reshape(1)
    return row_tok, block_e, n_used, pos.reshape(TOP_K, t)


def _key_bounds(knorm, knorm_meta):
    both = jnp.maximum(jnp.max(knorm, axis=1), knorm_meta[0, 0][None])
    return jnp.sqrt(both[:, 1:1 + N_KV_A, 0]), jnp.sqrt(both[:, 0, 0:2 * N_HEADS_B])


def _mixer(x, shared):
    bsz, n_tok, d = x.shape
    assert n_tok % Q_TILE == 0 and n_tok % TOKEN_TILE == 0
    tq = Q_TILE
    tm = TOKEN_TILE
    cos_t, sin_t = _rope_tables(n_tok)
    qa_t, ka, va_t, qb_t, kb, vb_t, knorm = _inproj(
        x, shared['g1'], shared['wt'], shared['wkb'], cos_t, sin_t, shared['qn'], shared['kn'],
        tm=tm, chunk=KEY_CHUNK)
    kmax_a, kmax_b = _key_bounds(knorm, shared['knorm_meta'])
    oa = _attn_a(kmax_a, qa_t, ka, va_t, shared['km_a'], shared['vm_a'], tq=tq)
    mbias = _meta_bias(shared['rel_bias'], n_tok)
    ob = _attn_b(shared['cfar'], kmax_b, qb_t, kb, vb_t, shared['km_b'], shared['vm_b'], shared['band'], mbias,
                 shared['lq1'], shared['lk1'], shared['lq2'], shared['lk2'], shared['sn'], tq=tq)
    h, n2, eid, gates = _outproj(oa, ob, x, shared['wa'], shared['wb'], shared['g2'],
                                 shared['wrh'], shared['wrl'], shared['br'], tm=tm)
    t = bsz * n_tok
    eid = jnp.transpose(eid, (1, 0, 2)).reshape(TOP_K, t)
    gates = jnp.transpose(gates, (1, 0, 2)).reshape(TOP_K, t)
    row_tok, block_e, n_used, pos = _dispatch_plan(eid, MOE_ROWS)
    return h.reshape(t, d), n2.reshape(t * D_TILES, LANE), row_tok, block_e, n_used, pos, gates.T


def _moe_and_norm(mixed, shape, shared):
    h, n2, row_tok, block_e, n_used, pos, gates = mixed
    y = _experts(block_e, n_used, row_tok, n2, shared['wg'], shared['wu'], shared['wd'], rows=MOE_ROWS)
    y0 = jnp.take(y, pos[0], axis=0)
    y1 = jnp.take(y, pos[1], axis=0)
    return _final(h, y0, y1, gates, shared['gf'], tm=TOKEN_TILE).reshape(shape)


def kernel(x_prompt, x_sample, meta_tokens, rel_bias, norm1, w_in, q_norm, k_norm, lambda_q1, lambda_k1,
           lambda_q2, lambda_k2, sub_norm, w_out, norm2, w_router_group, b_router_group, w_router_expert,
           b_router_expert, w_gate, w_up, w_down, final_norm):
    i = LAYER
    d = D_MODEL
    o0 = WIDTH_A
    o1 = o0 + KV_A
    o2 = o1 + KV_A
    o3 = o2 + WIDTH_B
    o4 = o3 + WIDTH_B
    w = w_in[i]
    wt = jnp.concatenate([w[:, :o2], w[:, o2:o3], w[:, o4:]], axis=1).T.astype(BF16)
    wr = jnp.concatenate([w_router_expert[i], w_router_group[i],
                          jnp.zeros((d, ROUTER_ROWS - N_EXPERTS - N_GROUPS), F32)], axis=1).T
    wrh = wr.astype(BF16)
    br = jnp.concatenate([b_router_expert[i], b_router_group[i],
                          jnp.zeros((ROUTER_ROWS - N_EXPERTS - N_GROUPS,), F32)]).reshape(ROUTER_ROWS, 1)
    shared = {
        'g1': norm1[i].reshape(1, d), 'wt': wt, 'wkb': w[:, o3:o4].astype(BF16),
        'qn': q_norm[i].reshape(HEAD_DIM, 1), 'kn': k_norm[i].reshape(HEAD_DIM, 1),
        'rel_bias': rel_bias,
        'lq1': lambda_q1[i].reshape(1, HEAD_DIM), 'lk1': lambda_k1[i].reshape(1, HEAD_DIM),
        'lq2': lambda_q2[i].reshape(1, HEAD_DIM), 'lk2': lambda_k2[i].reshape(1, HEAD_DIM),
        'sn': sub_norm[i].reshape(2 * HEAD_DIM, 1),
        'wa': w_out[i][:WIDTH_A].astype(BF16), 'wb': w_out[i][WIDTH_A:].astype(BF16),
        'g2': norm2[i].reshape(1, d), 'wrh': wrh, 'wrl': (wr - wrh.astype(F32)).astype(BF16), 'br': br,
        'wg': w_gate[i].astype(BF16), 'wu': w_up[i].astype(BF16), 'wd': w_down[i].astype(BF16),
        'gf': final_norm.reshape(1, d),
    }
    xm = jnp.concatenate([meta_tokens.astype(F32), jnp.zeros((META_PAD - N_META, d), F32)], axis=0)[None]
    ones = jnp.ones((HEAD_DIM, META_PAD), F32)
    _, km_a, vm_a, _, km_b, vm_b, knorm_meta = _inproj(
        xm, shared['g1'], shared['wt'], shared['wkb'], ones, 0.0 * ones, shared['qn'], shared['kn'],
        tm=META_PAD, chunk=META_PAD)
    shared['km_a'] = km_a[0]
    shared['vm_a'] = vm_a[0, 0]
    shared['km_b'] = km_b[0]
    shared['vm_b'] = vm_b[0, 0]
    shared['knorm_meta'] = knorm_meta
    shared['band'], shared['cfar'] = _band_tables(rel_bias, Q_TILE)
    mixed_prompt = _mixer(x_prompt, shared)
    mixed_sample = _mixer(x_sample, shared)
    return (_moe_and_norm(mixed_prompt, x_prompt.shape, shared),
            _moe_and_norm(mixed_sample, x_sample.shape, shared))
```
